```python
import jax, jax.numpy as jnp
from jax import lax
import numpy as np

D_MODEL = 1024
BATCH = 2
SEQ = 8192
DEPTH = 2
DEC_BATCH = 16
DEC_SEQ = 4096
PAST_LEN = 128

N_EVEN = (DEPTH + 1) // 2
N_ODD = DEPTH // 2
EPS = 1e-6
MLA_HEADS = 8
Q_LORA = 256
KV_LORA = 128
QK_NOPE = 64
QK_ROPE = 32
V_HEAD = 64
ROPE_THETA = 10000.0
Q_BLOCK = 128
MLA_WIDTH = MLA_HEADS * V_HEAD
M_HEADS = 4
M_HEAD_DIM = 128
M_WIDTH = M_HEADS * M_HEAD_DIM
M_CHUNK = 128
AB_IN = Q_LORA + KV_LORA + QK_ROPE + 4 * M_WIDTH + 4 * M_HEADS
AB_CUTS = [int(c) for c in np.cumsum([Q_LORA, KV_LORA, QK_ROPE, 2 * M_WIDTH, M_WIDTH, M_WIDTH])]
G_GROUPS = 8
G_CHUNK = 128
G_WIDTH = D_MODEL
G_CH = G_WIDTH // G_GROUPS
D_FF = 2816
CONV_W = 3

kernel_name = 'hybrid_mla_mlstm_gmlp_encoder'


def rmsnorm(x, g):
    xf = x.astype(jnp.float32)
    y = xf * lax.rsqrt(jnp.mean(xf * xf, axis=-1, keepdims=True) + EPS)
    return (y * g.astype(jnp.float32)).astype(x.dtype)


def dwconv3(x, w, b):
    xp = jnp.pad(x, ((0, 0), (1, 1), (0, 0)))
    return xp[:, :-2] * w[0] + xp[:, 1:-1] * w[1] + xp[:, 2:] * w[2] + b


def rope_tables(S):
    pos = jnp.arange(S, dtype=jnp.float32)
    inv = 1.0 / (ROPE_THETA ** (jnp.arange(0, QK_ROPE, 2, dtype=jnp.float32) / QK_ROPE))
    ang = pos[:, None] * inv[None, :]
    return jnp.cos(ang), jnp.sin(ang)


def apply_rope(x, cos, sin):
    x1, x2 = jnp.split(x, 2, axis=-1)
    cos = cos.astype(x.dtype)
    sin = sin.astype(x.dtype)
    return jnp.concatenate([x1 * cos - x2 * sin, x2 * cos + x1 * sin], axis=-1)


def mla_attention(q_nope, q_rope, k_nope, k_rope, v):
    B, S, H, _ = q_nope.shape
    nq = S // Q_BLOCK
    scale = (QK_NOPE + QK_ROPE) ** -0.5
    qn_b = q_nope.reshape(B, nq, Q_BLOCK, H, QK_NOPE).transpose(1, 0, 2, 3, 4)
    qr_b = q_rope.reshape(B, nq, Q_BLOCK, H, QK_ROPE).transpose(1, 0, 2, 3, 4)

    def block(args):
        qn_i, qr_i = args
        s = jnp.einsum('bqhd,bkhd->bhqk', qn_i, k_nope) + jnp.einsum('bqhr,bkr->bhqk', qr_i, k_rope)
        p = jax.nn.softmax(s.astype(jnp.float32) * scale, axis=-1).astype(v.dtype)
        return jnp.einsum('bhqk,bkhd->bqhd', p, v)

    o = lax.map(block, (qn_b, qr_b))
    return o.transpose(1, 0, 2, 3, 4).reshape(B, S, H * V_HEAD)


def mlstm_chunkwise(q, k, v, li, lf):
    B, H, S, dh = q.shape
    L = M_CHUNK
    nc = S // L
    q = q.reshape(B, H, nc, L, dh)
    k = k.reshape(B, H, nc, L, dh)
    v = v.reshape(B, H, nc, L, dh)
    li = li.reshape(B, H, nc, L)
    lf = lf.reshape(B, H, nc, L)
    b = jnp.cumsum(lf, axis=-1)
    bL = b[..., -1]
    mask = jnp.tril(jnp.ones((L, L), dtype=bool))
    dlog = jnp.where(mask, b[..., :, None] - b[..., None, :] + li[..., None, :], -jnp.inf)
    g = bL[..., None] - b + li
    m_loc = jnp.max(g, axis=-1)
    w_loc = jnp.exp(g - m_loc[..., None])
    C_loc = jnp.einsum('bhnl,bhnld,bhnle->bhnde', w_loc, v, k)
    n_loc = jnp.einsum('bhnl,bhnld->bhnd', w_loc, k)

    def step(carry, xs):
        C, n, m = carry
        bL_c, C_l, n_l, m_l = xs
        m_new = jnp.maximum(bL_c + m, m_l)
        a = jnp.exp(bL_c + m - m_new)
        c = jnp.exp(m_l - m_new)
        C_new = a[..., None, None] * C + c[..., None, None] * C_l
        n_new = a[..., None] * n + c[..., None] * n_l
        return (C_new, n_new, m_new), (C, n, m)

    init = (jnp.zeros((B, H, dh, dh), jnp.float32), jnp.zeros((B, H, dh), jnp.float32), jnp.zeros((B, H), jnp.float32))
    xs = (jnp.moveaxis(bL, 2, 0), jnp.moveaxis(C_loc, 2, 0), jnp.moveaxis(n_loc, 2, 0), jnp.moveaxis(m_loc, 2, 0))
    _, (C_prev, n_prev, m_prev) = lax.scan(step, init, xs)
    C_prev = jnp.moveaxis(C_prev, 0, 2)
    n_prev = jnp.moveaxis(n_prev, 0, 2)
    m_prev = jnp.moveaxis(m_prev, 0, 2)
    a_t = b + m_prev[..., None]
    m_t = jnp.maximum(a_t, jnp.max(dlog, axis=-1))
    inter = jnp.exp(a_t - m_t)
    qk = jnp.einsum('bhntd,bhnsd->bhnts', q, k) * jnp.exp(dlog - m_t[..., None])
    num = inter[..., None] * jnp.einsum('bhnde,bhnte->bhntd', C_prev, q) + jnp.einsum('bhnts,bhnsd->bhntd', qk, v)
    den = inter * jnp.einsum('bhnd,bhntd->bhnt', n_prev, q) + jnp.sum(qk, axis=-1)
    h = num / jnp.maximum(jnp.abs(den), jnp.exp(-m_t))[..., None]
    return h.reshape(B, H, S, dh)


def mixer_ab(xn, w_in, q_norm, w_uq, kv_norm, w_ukv, conv_w, conv_b, gate_bias, out_norm, w_out):
    B, S, _ = xn.shape
    h = xn @ w_in
    c_q, c_kv, k_r, m_qk, m_v, m_o, m_g = jnp.split(h, AB_CUTS, axis=-1)
    cos, sin = rope_tables(S)
    q = (rmsnorm(c_q, q_norm) @ w_uq).reshape(B, S, MLA_HEADS, QK_NOPE + QK_ROPE)
    q_nope = q[..., :QK_NOPE]
    q_rope = apply_rope(q[..., QK_NOPE:], cos[:, None], sin[:, None])
    kv = (rmsnorm(c_kv, kv_norm) @ w_ukv).reshape(B, S, MLA_HEADS, QK_NOPE + V_HEAD)
    k_nope = kv[..., :QK_NOPE]
    v_att = kv[..., QK_NOPE:]
    k_rope = apply_rope(k_r, cos, sin)
    y_att = mla_attention(q_nope, q_rope, k_nope, k_rope, v_att)
    qk = jax.nn.silu(dwconv3(m_qk, conv_w, conv_b))

    def heads(t):
        return t.reshape(B, S, M_HEADS, M_HEAD_DIM).transpose(0, 2, 1, 3).astype(jnp.float32)

    mq = heads(qk[..., :M_WIDTH])
    mk = heads(qk[..., M_WIDTH:]) * (M_HEAD_DIM ** -0.5)
    mv = heads(m_v)
    g = (m_g.astype(jnp.float32) + gate_bias.astype(jnp.float32)).transpose(0, 2, 1)
    i_f, f_f, i_b, f_b = jnp.split(g, 4, axis=1)
    h_f = mlstm_chunkwise(mq, mk, mv, i_f, jax.nn.log_sigmoid(f_f))
    fl = lambda t: jnp.flip(t, axis=2)
    h_b = fl(mlstm_chunkwise(fl(mq), fl(mk), fl(mv), fl(i_b), fl(jax.nn.log_sigmoid(f_b))))
    hs = (h_f + h_b).transpose(0, 2, 1, 3)
    hs = hs * lax.rsqrt(jnp.mean(hs * hs, axis=-1, keepdims=True) + EPS)
    hs = hs.reshape(B, S, M_WIDTH) * out_norm.astype(jnp.float32)
    y_m = (jax.nn.sigmoid(m_o.astype(jnp.float32)) * hs).astype(xn.dtype)
    return jnp.concatenate([y_att, y_m], axis=-1) @ w_out


def mixer_c(xn, w_in, v_norm, w_s, b_s, w_out):
    B, S, _ = xn.shape
    h = jax.nn.gelu(xn @ w_in, approximate=False)
    u, v = jnp.split(h, 2, axis=-1)
    v = rmsnorm(v, v_norm).reshape(B, S // G_CHUNK, G_CHUNK, G_GROUPS, G_CH)
    sv = jnp.einsum('gts,bnsgc->bntgc', w_s, v) + b_s.T[:, :, None]
    return (u * sv.reshape(B, S, G_WIDTH)) @ w_out


def conv_ffn(xn, w_up, conv_w, conv_b, w_down):
    a = dwconv3(xn @ w_up, conv_w, conv_b)
    gate, val = jnp.split(a, 2, axis=-1)
    return (jax.nn.silu(gate) * val) @ w_down


def forward(x, p):
    for layer in range(DEPTH):
        if layer % 2 == 0:
            i = layer // 2
            x = x + mixer_ab(rmsnorm(x, p['ab_norm'][i]), p['ab_w_in'][i], p['mla_q_norm'][i], p['mla_w_uq'][i],
                             p['mla_kv_norm'][i], p['mla_w_ukv'][i], p['mlstm_conv_w'][i], p['mlstm_conv_b'][i],
                             p['mlstm_gate_bias'][i], p['mlstm_out_norm'][i], p['ab_w_out'][i])
        else:
            i = layer // 2
            x = x + mixer_c(rmsnorm(x, p['c_norm'][i]), p['c_w_in'][i], p['c_v_norm'][i], p['c_w_spatial'][i],
                            p['c_b_spatial'][i], p['c_w_out'][i])
        x = x + conv_ffn(rmsnorm(x, p['ffn_norm'][layer]), p['ffn_w_up'][layer], p['ffn_conv_w'][layer],
                         p['ffn_conv_b'][layer], p['ffn_w_down'][layer])
    return rmsnorm(x, p['final_norm'])


def setup_inputs(seed: int = 0) -> dict:
    key = jax.random.key(seed)
    ks = jax.random.split(key, 40)
    f32 = jnp.float32

    def nrm(k, shape, scale):
        return jax.random.normal(k, shape, f32) * scale

    def gain(k, shape):
        return 1.0 + 0.05 * jax.random.normal(k, shape, f32)

    f_lin = jnp.linspace(3.0, 6.0, M_HEADS, dtype=f32)[None, :]
    gate_bias = jnp.concatenate([
        nrm(ks[10], (N_EVEN, M_HEADS), 0.1),
        f_lin + nrm(ks[11], (N_EVEN, M_HEADS), 0.1),
        nrm(ks[12], (N_EVEN, M_HEADS), 0.1),
        f_lin + nrm(ks[13], (N_EVEN, M_HEADS), 0.1)], axis=-1)
    return {
        'x_prompt': nrm(ks[0], (BATCH, SEQ, D_MODEL), 1.0),
        'x_sample': nrm(ks[1], (DEC_BATCH, DEC_SEQ, D_MODEL), 1.0),
        'ab_norm': gain(ks[2], (N_EVEN, D_MODEL)),
        'ab_w_in': nrm(ks[3], (N_EVEN, D_MODEL, AB_IN), D_MODEL ** -0.5),
        'mla_q_norm': gain(ks[4], (N_EVEN, Q_LORA)),
        'mla_w_uq': nrm(ks[5], (N_EVEN, Q_LORA, MLA_HEADS * (QK_NOPE + QK_ROPE)), Q_LORA ** -0.5),
        'mla_kv_norm': gain(ks[6], (N_EVEN, KV_LORA)),
        'mla_w_ukv': nrm(ks[7], (N_EVEN, KV_LORA, MLA_HEADS * (QK_NOPE + V_HEAD)), KV_LORA ** -0.5),
        'mlstm_conv_w': nrm(ks[8], (N_EVEN, CONV_W, 2 * M_WIDTH), CONV_W ** -0.5),
        'mlstm_conv_b': nrm(ks[9], (N_EVEN, 2 * M_WIDTH), 0.01),
        'mlstm_gate_bias': gate_bias,
        'mlstm_out_norm': gain(ks[14], (N_EVEN, M_WIDTH)),
        'ab_w_out': nrm(ks[15], (N_EVEN, MLA_WIDTH + M_WIDTH, D_MODEL), (MLA_WIDTH + M_WIDTH) ** -0.5),
        'c_norm': gain(ks[16], (N_ODD, D_MODEL)),
        'c_w_in': nrm(ks[17], (N_ODD, D_MODEL, 2 * G_WIDTH), D_MODEL ** -0.5),
        'c_v_norm': gain(ks[18], (N_ODD, G_WIDTH)),
        'c_w_spatial': nrm(ks[19], (N_ODD, G_GROUPS, G_CHUNK, G_CHUNK), G_CHUNK ** -0.5),
        'c_b_spatial': nrm(ks[20], (N_ODD, G_GROUPS, G_CHUNK), 0.01),
        'c_w_out': nrm(ks[21], (N_ODD, G_WIDTH, D_MODEL), G_WIDTH ** -0.5),
        'ffn_norm': gain(ks[22], (DEPTH, D_MODEL)),
        'ffn_w_up': nrm(ks[23], (DEPTH, D_MODEL, 2 * D_FF), D_MODEL ** -0.5),
        'ffn_conv_w': nrm(ks[24], (DEPTH, CONV_W, 2 * D_FF), CONV_W ** -0.5),
        'ffn_conv_b': nrm(ks[25], (DEPTH, 2 * D_FF), 0.01),
        'ffn_w_down': nrm(ks[26], (DEPTH, D_FF, D_MODEL), D_FF ** -0.5),
        'final_norm': gain(ks[27], (D_MODEL,)),
    }


def reference(x_prompt, x_sample, ab_norm, ab_w_in, mla_q_norm, mla_w_uq, mla_kv_norm, mla_w_ukv, mlstm_conv_w,
              mlstm_conv_b, mlstm_gate_bias, mlstm_out_norm, ab_w_out, c_norm, c_w_in, c_v_norm, c_w_spatial,
              c_b_spatial, c_w_out, ffn_norm, ffn_w_up, ffn_conv_w, ffn_conv_b, ffn_w_down, final_norm):
    p = {
        'ab_norm': ab_norm, 'ab_w_in': ab_w_in, 'mla_q_norm': mla_q_norm, 'mla_w_uq': mla_w_uq,
        'mla_kv_norm': mla_kv_norm, 'mla_w_ukv': mla_w_ukv, 'mlstm_conv_w': mlstm_conv_w,
        'mlstm_conv_b': mlstm_conv_b, 'mlstm_gate_bias': mlstm_gate_bias, 'mlstm_out_norm': mlstm_out_norm,
        'ab_w_out': ab_w_out, 'c_norm': c_norm, 'c_w_in': c_w_in, 'c_v_norm': c_v_norm,
        'c_w_spatial': c_w_spatial, 'c_b_spatial': c_b_spatial, 'c_w_out': c_w_out, 'ffn_norm': ffn_norm,
        'ffn_w_up': ffn_w_up, 'ffn_conv_w': ffn_conv_w, 'ffn_conv_b': ffn_conv_b, 'ffn_w_down': ffn_w_down,
        'final_norm': final_norm,
    }
    y_prompt = forward(x_prompt, p)
    y_sample = forward(x_sample, p)
    return (y_prompt, y_sample)
```

```python
import functools

import jax
import jax.numpy as jnp
import numpy as np
from jax import lax
from jax.experimental import pallas as pl
from jax.experimental.pallas import tpu as pltpu

F32 = jnp.float32
BF16 = jnp.bfloat16

EPS = 1e-6
D_MODEL = 1024
MLA_HEADS = 8
Q_LORA = 256
KV_LORA = 128
QK_NOPE = 64
QK_ROPE = 32
V_HEAD = 64
ROPE_THETA = 10000.0
M_HEADS = 4
M_HEAD_DIM = 128
M_WIDTH = M_HEADS * M_HEAD_DIM
CHUNK = 128
G_GROUPS = 8
D_FF = 2816
FF_COLS = 256
LANES = 128
HALO = 8
VMEM_LIMIT = 56 * 1024 * 1024

_C_Q, _C_KV, _C_KRA, _C_KRB, _C_MQK, _C_MV, _C_MO, _C_G, _C_END = (
    0, 256, 384, 512, 640, 1664, 2176, 2688, 2816)


def _rms(x, g):
    return x * lax.rsqrt(jnp.mean(x * x, axis=-1, keepdims=True) + EPS) * g


def _const_spec(shape):
    nd = len(shape)
    return pl.BlockSpec(shape, lambda *_: (0,) * nd, pipeline_mode=pl.Buffered(1))


def _params(*sem):
    return pltpu.CompilerParams(dimension_semantics=sem, vmem_limit_bytes=VMEM_LIMIT)


def _dot(a, b):
    return jnp.dot(a, b, preferred_element_type=F32)


def _dot_nt(a, b):
    return lax.dot_general(a, b, (((1,), (1,)), ((), ())), preferred_element_type=F32)


def _dot_tn(a, b):
    return lax.dot_general(a, b, (((0,), (0,)), ((), ())), preferred_element_type=F32)


def _ab_in_kernel(x_ref, cos_ref, sin_ref, nrm_ref, w_ref, qn_ref, wqa_ref, wqb_ref, kvn_ref, wka_ref, wv_ref,
                  gb_ref, q_ref, k_ref, v_ref, mqk_ref, mv_ref, mo_ref, g_ref, *, tm):
    xn = _rms(x_ref[...], nrm_ref[...]).astype(BF16)

    def proj(a, b):
        return _dot(xn, w_ref[:, a:b])

    mqk_ref[...] = proj(_C_MQK, _C_MV)
    mv_ref[...] = proj(_C_MV, _C_MO).astype(BF16)
    mo_ref[...] = proj(_C_MO, _C_G)
    gt = proj(_C_G, _C_END).T
    for c in range(tm // CHUNK):
        g_ref[c] = gt[0:16, c * CHUNK:(c + 1) * CHUNK] + gb_ref[:, 0:1]

    cos = cos_ref[...]
    sin = sin_ref[...]
    scale = (QK_NOPE + QK_ROPE) ** -0.5
    cqn = _rms(proj(_C_Q, _C_KV), qn_ref[...]).astype(BF16)
    qa = _dot(cqn, wqa_ref[...])
    qb = _dot(cqn, wqb_ref[...])
    cos_q = cos * scale
    sin_q = sin * scale
    for h in range(MLA_HEADS):
        blk = slice(h * LANES, (h + 1) * LANES)
        q_ref[:, blk] = (qa[:, blk] * cos_q + qb[:, blk] * sin_q).astype(BF16)

    ckvn = _rms(proj(_C_KV, _C_KRA), kvn_ref[...]).astype(BF16)
    ka = _dot(ckvn, wka_ref[...])
    kr = proj(_C_KRA, _C_KRB) * cos + proj(_C_KRB, _C_MQK) * sin
    for h in range(MLA_HEADS):
        blk = slice(h * LANES, (h + 1) * LANES)
        k_ref[:, blk] = (ka[:, blk] + kr).astype(BF16)
    v_ref[...] = _dot(ckvn, wv_ref[...]).astype(BF16)


def _ab_in(x, cos_t, sin_t, nrm, w_big, qn, wqa, wqb, kvn, wka, wv, gb, *, seq, tm):
    n = x.shape[0]
    nt = n // tm
    tps = seq // tm
    row = lambda i: (i, 0)
    tab = lambda i: (i % tps, 0)
    out_shape = (
        jax.ShapeDtypeStruct((n, MLA_HEADS * LANES), BF16),
        jax.ShapeDtypeStruct((n, MLA_HEADS * LANES), BF16),
        jax.ShapeDtypeStruct((n, MLA_HEADS * V_HEAD), BF16),
        jax.ShapeDtypeStruct((n, 2 * M_WIDTH), F32),
        jax.ShapeDtypeStruct((n, M_WIDTH), BF16),
        jax.ShapeDtypeStruct((n, M_WIDTH), F32),
        jax.ShapeDtypeStruct((n // CHUNK, 16, CHUNK), F32),
    )
    return pl.pallas_call(
        functools.partial(_ab_in_kernel, tm=tm),
        grid=(nt,),
        in_specs=[
            pl.BlockSpec((tm, D_MODEL), row),
            pl.BlockSpec((tm, LANES), tab),
            pl.BlockSpec((tm, LANES), tab),
            _const_spec(nrm.shape), _const_spec(w_big.shape), _const_spec(qn.shape), _const_spec(wqa.shape),
            _const_spec(wqb.shape), _const_spec(kvn.shape), _const_spec(wka.shape), _const_spec(wv.shape),
            _const_spec(gb.shape),
        ],
        out_specs=(
            pl.BlockSpec((tm, MLA_HEADS * LANES), row),
            pl.BlockSpec((tm, MLA_HEADS * LANES), row),
            pl.BlockSpec((tm, MLA_HEADS * V_HEAD), row),
            pl.BlockSpec((tm, 2 * M_WIDTH), row),
            pl.BlockSpec((tm, M_WIDTH), row),
            pl.BlockSpec((tm, M_WIDTH), row),
            pl.BlockSpec((tm // CHUNK, 16, CHUNK), lambda i: (i, 0, 0)),
        ),
        out_shape=out_shape,
        compiler_params=_params("parallel"),
        name="ab_in",
    )(x, cos_t, sin_t, nrm, w_big, qn, wqa, wqb, kvn, wka, wv, gb)


def _attn_kernel(q_ref, k_ref, v_ref, o_ref, *, tq, tk):
    nk = k_ref.shape[0] // tk
    q0 = q_ref[:, 0:LANES]
    q1 = q_ref[:, LANES:2 * LANES]

    def one(q, kc, vc, m, l, acc):
        s = _dot_nt(q, kc)
        m_new = jnp.maximum(m, jnp.max(s, axis=-1, keepdims=True))
        alpha = jnp.exp(m - m_new)
        p = jnp.exp(s - m_new)
        l = alpha * l + jnp.sum(p, axis=-1, keepdims=True)
        acc = alpha * acc + _dot(p.astype(BF16), vc)
        return m_new, l, acc

    def body(c, carry):
        m0, l0, a0, m1, l1, a1 = carry
        off = pl.multiple_of(c * tk, tk)
        vc = v_ref[pl.ds(off, tk), :]
        m0, l0, a0 = one(q0, k_ref[pl.ds(off, tk), 0:LANES], vc, m0, l0, a0)
        m1, l1, a1 = one(q1, k_ref[pl.ds(off, tk), LANES:2 * LANES], vc, m1, l1, a1)
        return m0, l0, a0, m1, l1, a1

    mi = jnp.full((tq, 1), -jnp.inf, F32)
    li = jnp.zeros((tq, 1), F32)
    ai = jnp.zeros((tq, LANES), F32)
    m0, l0, a0, m1, l1, a1 = lax.fori_loop(0, nk, body, (mi, li, ai, mi, li, ai))
    lane = lax.broadcasted_iota(jnp.int32, (tq, LANES), 1)
    o_ref[...] = jnp.where(lane < V_HEAD, a0 / l0, a1 / l1).astype(BF16)


def _attn(q, k, v, *, batch, seq, tq, tk):
    n = q.shape[0]
    nq = seq // tq
    pairs = MLA_HEADS // 2
    return pl.pallas_call(
        functools.partial(_attn_kernel, tq=tq, tk=tk),
        grid=(batch, pairs, nq),
        in_specs=[
            pl.BlockSpec((tq, 2 * LANES), lambda b, p, i: (b * nq + i, p)),
            pl.BlockSpec((seq, 2 * LANES), lambda b, p, i: (b, p)),
            pl.BlockSpec((seq, LANES), lambda b, p, i: (b, p)),
        ],
        out_specs=pl.BlockSpec((tq, LANES), lambda b, p, i: (b * nq + i, p)),
        out_shape=jax.ShapeDtypeStruct((n, MLA_HEADS * V_HEAD), BF16),
        compiler_params=_params("parallel", "parallel", "arbitrary"),
        name="attn",
    )(q, k, v)


def _log_sigmoid(x):
    return jnp.minimum(x, 0.0) - jnp.log1p(jnp.exp(-jnp.abs(x)))


def _mlstm_kernel(x_ref, xp_ref, xn_ref, v_ref, g_ref, cw_ref, cb_ref, h_ref, xs, qs, ks, c_st, n_st, m_st,
                  *, ts, nblk):
    d = pl.program_id(1)
    i = pl.program_id(2)
    j = jnp.where(d == 0, i, nblk - 1 - i)
    nch = ts // CHUNK

    @pl.when(i == 0)
    def _():
        c_st[...] = jnp.zeros_like(c_st)
        n_st[...] = jnp.zeros_like(n_st)
        m_st[...] = jnp.zeros_like(m_st)

    xs[HALO:HALO + ts, :] = x_ref[...]
    xs[0:HALO, :] = jnp.where(j == 0, 0.0, xp_ref[...])
    xs[HALO + ts:2 * HALO + ts, :] = jnp.where(j == nblk - 1, 0.0, xn_ref[...])
    for cb in range(2 * M_HEADS):
        cols = slice(cb * LANES, (cb + 1) * LANES)
        y = (xs[HALO - 1:HALO - 1 + ts, cols] * cw_ref[0:1, cols] + xs[HALO:HALO + ts, cols] * cw_ref[1:2, cols]
             + xs[HALO + 1:HALO + 1 + ts, cols] * cw_ref[2:3, cols] + cb_ref[:, cols])
        y = y * jax.nn.sigmoid(y)
        if cb < M_HEADS:
            qs[:, cols] = y.astype(BF16)
        else:
            ks[:, (cb - M_HEADS) * LANES:(cb - M_HEADS + 1) * LANES] = (y * (M_HEAD_DIM ** -0.5)).astype(BF16)

    row = lax.broadcasted_iota(jnp.int32, (CHUNK, CHUNK), 0)
    col = lax.broadcasted_iota(jnp.int32, (CHUNK, CHUNK), 1)
    mask = (row - col) * (1 - 2 * d) >= 0
    lane8 = lax.broadcasted_iota(jnp.int32, (8, CHUNK), 1)

    def chunk(cc, carry):
        ci = jnp.where(d == 0, cc, nch - 1 - cc)
        r0 = pl.multiple_of(ci * CHUNK, CHUNK)
        gg = g_ref[ci, pl.ds(pl.multiple_of(8 * d, 8), 8), :]
        lf = _log_sigmoid(gg)
        pre = lf
        for sh in (1, 2, 4, 8, 16, 32, 64):
            pre = pre + jnp.where(lane8 >= sh, pltpu.roll(pre, sh, axis=1), 0.0)
        tot = pre[:, CHUNK - 1:CHUNK]
        bcum = jnp.where(d == 0, pre, tot - pre + lf)[4:8]
        b_end = tot[4:8]
        li = gg[0:4]
        grow = b_end - bcum + li
        m_loc = jnp.max(grow, axis=1, keepdims=True)
        cols_t = jnp.concatenate([bcum, grow, jnp.zeros((CHUNK - 8, CHUNK), F32)], axis=0).T

        for hh in range(M_HEADS):
            hc = slice(hh * LANES, (hh + 1) * LANES)
            q = qs[pl.ds(r0, CHUNK), hc]
            k = ks[pl.ds(r0, CHUNK), hc]
            v = v_ref[pl.ds(r0, CHUNK), hc]
            b_t = cols_t[:, hh:hh + 1]
            g_t = cols_t[:, 4 + hh:5 + hh]
            b_s = bcum[hh:hh + 1, :]
            li_s = li[hh:hh + 1, :]
            m_prev = m_st[hh:hh + 1, 0:1]
            n_prev = n_st[hh:hh + 1, :]
            c_prev = c_st[hh]

            dlog = jnp.where(mask, b_t - b_s + li_s, -jnp.inf)
            a_t = b_t + m_prev
            m_t = jnp.maximum(a_t, jnp.max(dlog, axis=-1, keepdims=True))
            inter = jnp.exp(a_t - m_t)
            qk = _dot_nt(q, k) * jnp.exp(dlog - m_t)
            num = inter * _dot(q, c_prev.astype(BF16)) + _dot(qk.astype(BF16), v)
            den = (inter * jnp.sum(q.astype(F32) * n_prev, axis=-1, keepdims=True)
                   + jnp.sum(qk, axis=-1, keepdims=True))
            h_ref[pl.ds(r0, CHUNK), hc] = num / jnp.maximum(jnp.abs(den), jnp.exp(-m_t))

            ml = m_loc[hh:hh + 1, :]
            kw = k.astype(F32) * jnp.exp(g_t - ml)
            c_loc = _dot_tn(kw.astype(BF16), v)
            n_loc = jnp.sum(kw, axis=0, keepdims=True)
            be = b_end[hh:hh + 1, :]
            m_new = jnp.maximum(be + m_prev, ml)
            fa = jnp.exp(be + m_prev - m_new)
            fc = jnp.exp(ml - m_new)
            c_st[hh] = fa * c_prev + fc * c_loc
            n_st[hh:hh + 1, :] = fa * n_prev + fc * n_loc
            m_st[hh:hh + 1, :] = jnp.broadcast_to(m_new, (1, LANES))
        return carry

    lax.fori_loop(0, nch, chunk, 0)


def _mlstm(mqk, mv, g, conv_w, conv_b, *, batch, seq, ts):
    n = mqk.shape[0]
    nblk = seq // ts
    tsh = ts // HALO
    nh = n // HALO

    def blk(b, d, i):
        return b * nblk + jnp.where(d == 0, i, nblk - 1 - i)

    return pl.pallas_call(
        functools.partial(_mlstm_kernel, ts=ts, nblk=nblk),
        grid=(batch, 2, nblk),
        in_specs=[
            pl.BlockSpec((ts, 2 * M_WIDTH), lambda b, d, i: (blk(b, d, i), 0)),
            pl.BlockSpec((HALO, 2 * M_WIDTH), lambda b, d, i: (jnp.maximum(blk(b, d, i) * tsh - 1, 0), 0)),
            pl.BlockSpec((HALO, 2 * M_WIDTH), lambda b, d, i: (jnp.minimum((blk(b, d, i) + 1) * tsh, nh - 1), 0)),
            pl.BlockSpec((ts, M_WIDTH), lambda b, d, i: (blk(b, d, i), 0)),
            pl.BlockSpec((ts // CHUNK, 16, CHUNK), lambda b, d, i: (blk(b, d, i), 0, 0)),
            _const_spec(conv_w.shape), _const_spec(conv_b.shape),
        ],
        out_specs=pl.BlockSpec((None, ts, M_WIDTH), lambda b, d, i: (d, blk(b, d, i), 0)),
        out_shape=jax.ShapeDtypeStruct((2, n, M_WIDTH), F32),
        scratch_shapes=[
            pltpu.VMEM((ts + 2 * HALO, 2 * M_WIDTH), F32),
            pltpu.VMEM((ts, M_WIDTH), BF16),
            pltpu.VMEM((ts, M_WIDTH), BF16),
            pltpu.VMEM((M_HEADS, M_HEAD_DIM, M_HEAD_DIM), F32),
            pltpu.VMEM((8, LANES), F32),
            pltpu.VMEM((8, LANES), F32),
        ],
        compiler_params=_params("parallel", "parallel", "arbitrary"),
        name="mlstm",
    )(mqk, mqk, mqk, mv, g, conv_w, conv_b)


def _ab_out_kernel(x_ref, ya_ref, h_ref, mo_ref, on_ref, wa_ref, wm_ref, o_ref):
    hs = h_ref[0] + h_ref[1]
    parts = []
    for hh in range(M_HEADS):
        hc = slice(hh * LANES, (hh + 1) * LANES)
        t = hs[:, hc]
        t = t * lax.rsqrt(jnp.mean(t * t, axis=-1, keepdims=True) + EPS) * on_ref[:, hc]
        parts.append((jax.nn.sigmoid(mo_ref[:, hc]) * t).astype(BF16))
    ym = jnp.concatenate(parts, axis=1)
    o_ref[...] = x_ref[...] + _dot(ya_ref[...], wa_ref[...]) + _dot(ym, wm_ref[...])


def _ab_out(x, ya, h2, mo, on, wa, wm, *, tm):
    n = x.shape[0]
    row = lambda i: (i, 0)
    return pl.pallas_call(
        _ab_out_kernel,
        grid=(n // tm,),
        in_specs=[
            pl.BlockSpec((tm, D_MODEL), row),
            pl.BlockSpec((tm, MLA_HEADS * V_HEAD), row),
            pl.BlockSpec((2, tm, M_WIDTH), lambda i: (0, i, 0)),
            pl.BlockSpec((tm, M_WIDTH), row),
            _const_spec(on.shape), _const_spec(wa.shape), _const_spec(wm.shape),
        ],
        out_specs=pl.BlockSpec((tm, D_MODEL), row),
        out_shape=jax.ShapeDtypeStruct((n, D_MODEL), F32),
        compiler_params=_params("parallel"),
        name="ab_out",
    )(x, ya, h2, mo, on, wa, wm)


def _ffn_kernel(x_ref, xp_ref, xn_ref, nrm_ref, wup_ref, cw_ref, cb_ref, wdn_ref, fin_ref, o_ref, us, act,
                *, tm, tps, final):
    jj = pl.program_id(0) % tps
    xa = jnp.concatenate([
        jnp.where(jj == 0, 0.0, xp_ref[...]),
        x_ref[...],
        jnp.where(jj == tps - 1, 0.0, xn_ref[...]),
    ], axis=0)
    xn = _rms(xa, nrm_ref[...]).astype(BF16)

    def conv(slot, cols):
        u = us.at[slot]
        return (u[HALO - 1:HALO - 1 + tm, :] * cw_ref[0:1, cols] + u[HALO:HALO + tm, :] * cw_ref[1:2, cols]
                + u[HALO + 1:HALO + 1 + tm, :] * cw_ref[2:3, cols] + cb_ref[:, cols])

    for c in range(D_FF // FF_COLS):
        gcols = slice(c * FF_COLS, (c + 1) * FF_COLS)
        vcols = slice(D_FF + c * FF_COLS, D_FF + (c + 1) * FF_COLS)
        us[0] = _dot(xn, wup_ref[:, gcols])
        us[1] = _dot(xn, wup_ref[:, vcols])
        gate = conv(0, gcols)
        val = conv(1, vcols)
        act[:, gcols] = (gate * jax.nn.sigmoid(gate) * val).astype(BF16)
    y = x_ref[...] + _dot(act[...], wdn_ref[...])
    if final:
        y = _rms(y, fin_ref[...])
    o_ref[...] = y


def _ffn(x, nrm, wup, cw, cb, wdn, fin, *, seq, tm, final):
    n = x.shape[0]
    nt = n // tm
    tps = seq // tm
    tmh = tm // HALO
    nh = n // HALO
    row = lambda i: (i, 0)
    return pl.pallas_call(
        functools.partial(_ffn_kernel, tm=tm, tps=tps, final=final),
        grid=(nt,),
        in_specs=[
            pl.BlockSpec((tm, D_MODEL), row),
            pl.BlockSpec((HALO, D_MODEL), lambda i: (jnp.maximum(i * tmh - 1, 0), 0)),
            pl.BlockSpec((HALO, D_MODEL), lambda i: (jnp.minimum((i + 1) * tmh, nh - 1), 0)),
            _const_spec(nrm.shape), _const_spec(wup.shape), _const_spec(cw.shape), _const_spec(cb.shape),
            _const_spec(wdn.shape), _const_spec(fin.shape),
        ],
        out_specs=pl.BlockSpec((tm, D_MODEL), row),
        out_shape=jax.ShapeDtypeStruct((n, D_MODEL), F32),
        scratch_shapes=[
            pltpu.VMEM((2, tm + 2 * HALO, FF_COLS), F32),
            pltpu.VMEM((tm, D_FF), BF16),
        ],
        compiler_params=_params("parallel"),
        name="ffn_final" if final else "ffn",
    )(x, x, x, nrm, wup, cw, cb, wdn, fin)


def _gelu(x):
    return 0.5 * x * (1.0 + lax.erf(x * (2.0 ** -0.5)))


def _mix_c_kernel(x_ref, nrm_ref, win_ref, vn_ref, ws_ref, bs_ref, wout_ref, o_ref, zs, *, tm):
    x = x_ref[...]
    xn = _rms(x, nrm_ref[...]).astype(BF16)
    u = _gelu(_dot(xn, win_ref[:, 0:D_MODEL]))
    v = _gelu(_dot(xn, win_ref[:, D_MODEL:2 * D_MODEL]))
    vn = _rms(v, vn_ref[...]).astype(BF16)
    nch = tm // CHUNK
    for g in range(G_GROUPS):
        cols = slice(g * LANES, (g + 1) * LANES)
        rhs = jnp.concatenate([vn[c * CHUNK:(c + 1) * CHUNK, cols] for c in range(nch)], axis=1)
        sv = _dot(ws_ref[g], rhs) + bs_ref[:, g:g + 1]
        for c in range(nch):
            zs[c * CHUNK:(c + 1) * CHUNK, cols] = (
                u[c * CHUNK:(c + 1) * CHUNK, cols] * sv[:, c * LANES:(c + 1) * LANES]).astype(BF16)
    o_ref[...] = x + _dot(zs[...], wout_ref[...])


def _mix_c(x, nrm, win, vn, ws, bs, wout, *, tm):
    n = x.shape[0]
    row = lambda i: (i, 0)
    return pl.pallas_call(
        functools.partial(_mix_c_kernel, tm=tm),
        grid=(n // tm,),
        in_specs=[
            pl.BlockSpec((tm, D_MODEL), row),
            _const_spec(nrm.shape), _const_spec(win.shape), _const_spec(vn.shape), _const_spec(ws.shape),
            _const_spec(bs.shape), _const_spec(wout.shape),
        ],
        out_specs=pl.BlockSpec((tm, D_MODEL), row),
        out_shape=jax.ShapeDtypeStruct((n, D_MODEL), F32),
        scratch_shapes=[pltpu.VMEM((tm, D_MODEL), BF16)],
        compiler_params=_params("parallel"),
        name="mix_c",
    )(x, nrm, win, vn, ws, bs, wout)


def _rope_tables(seq):
    pos = jnp.arange(seq, dtype=F32)
    inv = 1.0 / (ROPE_THETA ** (jnp.arange(0, QK_ROPE, 2, dtype=F32) / QK_ROPE))
    ang = pos[:, None] * inv[None, :]
    cos, sin = jnp.cos(ang), jnp.sin(ang)
    one = jnp.ones((seq, QK_NOPE), F32)
    zero_n = jnp.zeros((seq, QK_NOPE), F32)
    pad = jnp.zeros((seq, LANES - QK_NOPE - QK_ROPE), F32)
    return (jnp.concatenate([one, cos, cos, pad], axis=1), jnp.concatenate([zero_n, sin, sin, pad], axis=1))


def _head_block(nope, x1, x2):
    pad = jnp.zeros((nope.shape[0], LANES - QK_NOPE - QK_ROPE), nope.dtype)
    return jnp.concatenate([nope, x1, x2, pad], axis=1)


def _prep_even(i, ab_norm, ab_w_in, mla_q_norm, mla_w_uq, mla_kv_norm, mla_w_ukv, mlstm_conv_w, mlstm_conv_b,
               mlstm_gate_bias, mlstm_out_norm, ab_w_out):
    w_in = ab_w_in[i]
    half = QK_ROPE // 2
    zq = jnp.zeros((D_MODEL, QK_NOPE), F32)
    kr1 = w_in[:, 384:384 + half]
    kr2 = w_in[:, 384 + half:416]
    gates = jnp.pad(w_in[:, 2464:2480], ((0, 0), (0, LANES - 16)))
    w_big = jnp.concatenate([
        w_in[:, 0:384],
        _head_block(zq, kr1, kr2),
        _head_block(zq, -kr2, kr1),
        w_in[:, 416:2464],
        gates,
    ], axis=1).astype(BF16)
    w_uq = mla_w_uq[i].reshape(Q_LORA, MLA_HEADS, QK_NOPE + QK_ROPE)
    zn = jnp.zeros((Q_LORA, QK_NOPE), F32)
    wqa = jnp.concatenate([_head_block(w_uq[:, h, :QK_NOPE], w_uq[:, h, QK_NOPE:QK_NOPE + half],
                                       w_uq[:, h, QK_NOPE + half:]) for h in range(MLA_HEADS)], axis=1)
    wqb = jnp.concatenate([_head_block(zn, -w_uq[:, h, QK_NOPE + half:], w_uq[:, h, QK_NOPE:QK_NOPE + half])
                           for h in range(MLA_HEADS)], axis=1)
    w_ukv = mla_w_ukv[i].reshape(KV_LORA, MLA_HEADS, QK_NOPE + V_HEAD)
    wka = jnp.pad(w_ukv[:, :, :QK_NOPE], ((0, 0), (0, 0), (0, LANES - QK_NOPE))).reshape(KV_LORA, MLA_HEADS * LANES)
    wv = w_ukv[:, :, QK_NOPE:].reshape(KV_LORA, MLA_HEADS * V_HEAD)
    w_out = ab_w_out[i].astype(BF16)
    return dict(
        nrm=ab_norm[i][None, :], w_big=w_big, qn=mla_q_norm[i][None, :], wqa=wqa.astype(BF16),
        wqb=wqb.astype(BF16), kvn=mla_kv_norm[i][None, :], wka=wka.astype(BF16), wv=wv.astype(BF16),
        gb=jnp.broadcast_to(mlstm_gate_bias[i][:, None], (16, LANES)),
        conv_w=mlstm_conv_w[i], conv_b=mlstm_conv_b[i][None, :], on=mlstm_out_norm[i][None, :],
        wa=w_out[:MLA_HEADS * V_HEAD], wm=w_out[MLA_HEADS * V_HEAD:])


def _pick(pref, seq):
    return min(pref, seq)


def _forward(x3, even, odd, ffn, final_norm):
    batch, seq, _ = x3.shape
    x = x3.reshape(batch * seq, D_MODEL)
    tm = _pick(512, seq)
    cos_t, sin_t = _rope_tables(seq)
    e = even
    q, k, v, mqk, mv, mo, g = _ab_in(x, cos_t, sin_t, e["nrm"], e["w_big"], e["qn"], e["wqa"], e["wqb"], e["kvn"],
                                     e["wka"], e["wv"], e["gb"], seq=seq, tm=tm)
    ya = _attn(q, k, v, batch=batch, seq=seq, tq=_pick(256, seq), tk=_pick(512, seq))
    h2 = _mlstm(mqk, mv, g, e["conv_w"], e["conv_b"], batch=batch, seq=seq, ts=_pick(512, seq))
    x = _ab_out(x, ya, h2, mo, e["on"], e["wa"], e["wm"], tm=tm)
    f = ffn[0]
    x = _ffn(x, f["nrm"], f["wup"], f["cw"], f["cb"], f["wdn"], final_norm, seq=seq, tm=tm, final=False)
    o = odd
    x = _mix_c(x, o["nrm"], o["win"], o["vn"], o["ws"], o["bs"], o["wout"], tm=tm)
    f = ffn[1]
    x = _ffn(x, f["nrm"], f["wup"], f["cw"], f["cb"], f["wdn"], final_norm, seq=seq, tm=tm, final=True)
    return x.reshape(batch, seq, D_MODEL)


def kernel(x_prompt, x_sample, ab_norm, ab_w_in, mla_q_norm, mla_w_uq, mla_kv_norm, mla_w_ukv, mlstm_conv_w,
           mlstm_conv_b, mlstm_gate_bias, mlstm_out_norm, ab_w_out, c_norm, c_w_in, c_v_norm, c_w_spatial,
           c_b_spatial, c_w_out, ffn_norm, ffn_w_up, ffn_conv_w, ffn_conv_b, ffn_w_down, final_norm):
    even = _prep_even(0, ab_norm, ab_w_in, mla_q_norm, mla_w_uq, mla_kv_norm, mla_w_ukv, mlstm_conv_w, mlstm_conv_b,
                      mlstm_gate_bias, mlstm_out_norm, ab_w_out)
    odd = dict(nrm=c_norm[0][None, :], win=c_w_in[0].astype(BF16), vn=c_v_norm[0][None, :],
               ws=c_w_spatial[0].astype(BF16), bs=c_b_spatial[0].T, wout=c_w_out[0].astype(BF16))
    ffn = [dict(nrm=ffn_norm[l][None, :], wup=ffn_w_up[l].astype(BF16), cw=ffn_conv_w[l], cb=ffn_conv_b[l][None, :],
                wdn=ffn_w_down[l].astype(BF16)) for l in range(2)]
    fin = final_norm[None, :]
    return (_forward(x_prompt, even, odd, ffn, fin), _forward(x_sample, even, odd, ffn, fin))
```

```python
import functools

import jax
import jax.numpy as jnp
import numpy as np
from jax import lax
from jax.experimental import pallas as pl
from jax.experimental.pallas import tpu as pltpu

F32 = jnp.float32
BF16 = jnp.bfloat16

EPS = 1e-6
D_MODEL = 1024
MLA_HEADS = 8
Q_LORA = 256
KV_LORA = 128
QK_NOPE = 64
QK_ROPE = 32
V_HEAD = 64
ROPE_THETA = 10000.0
M_HEADS = 4
M_HEAD_DIM = 128
M_WIDTH = M_HEADS * M_HEAD_DIM
CHUNK = 128
G_GROUPS = 8
D_FF = 2816
FF_COLS = 256
LANES = 128
KEY_TILE = 256
LOG2_E = 1.4426950408889634
HALO = 8
VMEM_LIMIT = 56 * 1024 * 1024

_C_Q, _C_KV, _C_KRA, _C_KRB, _C_MQK, _C_MV, _C_MO, _C_G, _C_END = (
    0, 256, 384, 512, 640, 1664, 2176, 2688, 2816)


def _rms(x, g):
    return x * lax.rsqrt(jnp.mean(x * x, axis=-1, keepdims=True) + EPS) * g


def _const_spec(shape):
    nd = len(shape)
    return pl.BlockSpec(shape, lambda *_: (0,) * nd, pipeline_mode=pl.Buffered(1))


def _params(*sem):
    return pltpu.CompilerParams(dimension_semantics=sem, vmem_limit_bytes=VMEM_LIMIT)


def _dot(a, b):
    return jnp.dot(a, b, preferred_element_type=F32)


def _dot_nt(a, b):
    return lax.dot_general(a, b, (((1,), (1,)), ((), ())), preferred_element_type=F32)


def _dot_tn(a, b):
    return lax.dot_general(a, b, (((0,), (0,)), ((), ())), preferred_element_type=F32)


def _ab_in_kernel(x_ref, cos_ref, sin_ref, nrm_ref, w_ref, qn_ref, wqa_ref, wqb_ref, kvn_ref, wka_ref, wv_ref,
                  gb_ref, vone_ref, q_ref, k_ref, v_ref, mqk_ref, mv_ref, mo_ref, g_ref, *, tm):
    xn = _rms(x_ref[...], nrm_ref[...]).astype(BF16)

    def proj(a, b):
        return _dot(xn, w_ref[:, a:b])

    mqk_ref[...] = proj(_C_MQK, _C_MV)
    mv_ref[...] = proj(_C_MV, _C_MO).astype(BF16)
    mo_ref[...] = proj(_C_MO, _C_G)
    gt = proj(_C_G, _C_END).T
    for c in range(tm // CHUNK):
        g_ref[c] = gt[0:16, c * CHUNK:(c + 1) * CHUNK] + gb_ref[:, 0:1]

    cos = cos_ref[...]
    sin = sin_ref[...]
    scale = (QK_NOPE + QK_ROPE) ** -0.5 * LOG2_E
    cqn = _rms(proj(_C_Q, _C_KV), qn_ref[...]).astype(BF16)
    qa = _dot(cqn, wqa_ref[...])
    qb = _dot(cqn, wqb_ref[...])
    cos_q = cos * scale
    sin_q = sin * scale
    for h in range(MLA_HEADS):
        blk = slice(h * LANES, (h + 1) * LANES)
        q_ref[:, blk] = (qa[:, blk] * cos_q + qb[:, blk] * sin_q).astype(BF16)

    ckvn = _rms(proj(_C_KV, _C_KRA), kvn_ref[...]).astype(BF16)
    ka = _dot(ckvn, wka_ref[...])
    kr = proj(_C_KRA, _C_KRB) * cos + proj(_C_KRB, _C_MQK) * sin
    for h in range(MLA_HEADS):
        blk = slice(h * LANES, (h + 1) * LANES)
        k_ref[:, blk] = (ka[:, blk] + kr).astype(BF16)
    v_ref[...] = (_dot(ckvn, wv_ref[...]) + vone_ref[...]).astype(BF16)


def _ab_in(x, cos_t, sin_t, nrm, w_big, qn, wqa, wqb, kvn, wka, wv, gb, vone, *, seq, tm):
    n = x.shape[0]
    nt = n // tm
    tps = seq // tm
    row = lambda i: (i, 0)
    tab = lambda i: (i % tps, 0)
    out_shape = (
        jax.ShapeDtypeStruct((n, MLA_HEADS * LANES), BF16),
        jax.ShapeDtypeStruct((n, MLA_HEADS * LANES), BF16),
        jax.ShapeDtypeStruct((n, MLA_HEADS * LANES), BF16),
        jax.ShapeDtypeStruct((n, 2 * M_WIDTH), F32),
        jax.ShapeDtypeStruct((n, M_WIDTH), BF16),
        jax.ShapeDtypeStruct((n, M_WIDTH), F32),
        jax.ShapeDtypeStruct((n // CHUNK, 16, CHUNK), F32),
    )
    return pl.pallas_call(
        functools.partial(_ab_in_kernel, tm=tm),
        grid=(nt,),
        in_specs=[
            pl.BlockSpec((tm, D_MODEL), row),
            pl.BlockSpec((tm, LANES), tab),
            pl.BlockSpec((tm, LANES), tab),
            _const_spec(nrm.shape), _const_spec(w_big.shape), _const_spec(qn.shape), _const_spec(wqa.shape),
            _const_spec(wqb.shape), _const_spec(kvn.shape), _const_spec(wka.shape), _const_spec(wv.shape),
            _const_spec(gb.shape), _const_spec(vone.shape),
        ],
        out_specs=(
            pl.BlockSpec((tm, MLA_HEADS * LANES), row),
            pl.BlockSpec((tm, MLA_HEADS * LANES), row),
            pl.BlockSpec((tm, MLA_HEADS * LANES), row),
            pl.BlockSpec((tm, 2 * M_WIDTH), row),
            pl.BlockSpec((tm, M_WIDTH), row),
            pl.BlockSpec((tm, M_WIDTH), row),
            pl.BlockSpec((tm // CHUNK, 16, CHUNK), lambda i: (i, 0, 0)),
        ),
        out_shape=out_shape,
        compiler_params=_params("parallel"),
        name="ab_in",
    )(x, cos_t, sin_t, nrm, w_big, qn, wqa, wqb, kvn, wka, wv, gb, vone)


def _attn_kernel(q_ref, k_ref, v_ref, o_ref, s_scr, *, tq):
    nt = k_ref.shape[0] // KEY_TILE
    mbs = []
    for h in range(2):
        hc = slice(h * LANES, (h + 1) * LANES)
        q = q_ref[:, hc]
        mx = jnp.full((tq, LANES), -jnp.inf, F32)
        for t in range(nt):
            keys = slice(t * KEY_TILE, (t + 1) * KEY_TILE)
            s = _dot_nt(q, k_ref[keys, hc])
            s_scr[h, :, keys] = s
            for j in range(KEY_TILE // LANES):
                mx = jnp.maximum(mx, s[:, j * LANES:(j + 1) * LANES])
        mbs.append(jnp.broadcast_to(jnp.max(mx, axis=-1, keepdims=True), (tq, LANES)))
    outs = []
    for h in range(2):
        hc = slice(h * LANES, (h + 1) * LANES)
        mb = jnp.concatenate([mbs[h]] * (KEY_TILE // LANES), axis=1)
        acc = jnp.zeros((tq, LANES), F32)
        for t in range(nt):
            keys = slice(t * KEY_TILE, (t + 1) * KEY_TILE)
            p = jnp.exp2(s_scr[h, :, keys] - mb).astype(BF16)
            acc = acc + _dot(p, v_ref[keys, hc])
        ones_lane = V_HEAD if h == 0 else 0
        outs.append(acc / acc[:, ones_lane:ones_lane + 1])
    lane = lax.broadcasted_iota(jnp.int32, (tq, LANES), 1)
    o_ref[...] = jnp.where(lane < V_HEAD, outs[0], outs[1]).astype(BF16)


def _attn(q, k, v, *, batch, seq, tq):
    n = q.shape[0]
    nq = seq // tq
    pairs = MLA_HEADS // 2
    return pl.pallas_call(
        functools.partial(_attn_kernel, tq=tq),
        grid=(batch, pairs, nq),
        in_specs=[
            pl.BlockSpec((tq, 2 * LANES), lambda b, p, i: (b * nq + i, p)),
            pl.BlockSpec((seq, 2 * LANES), lambda b, p, i: (b, p)),
            pl.BlockSpec((seq, 2 * LANES), lambda b, p, i: (b, p)),
        ],
        out_specs=pl.BlockSpec((tq, LANES), lambda b, p, i: (b * nq + i, p)),
        out_shape=jax.ShapeDtypeStruct((n, MLA_HEADS * V_HEAD), BF16),
        scratch_shapes=[pltpu.VMEM((2, tq, seq), F32)],
        compiler_params=_params("parallel", "parallel", "arbitrary"),
        name="attn",
    )(q, k, v)


def _log_sigmoid(x):
    return jnp.minimum(x, 0.0) - jnp.log1p(jnp.exp(-jnp.abs(x)))


def _mlstm_kernel(x_ref, xp_ref, xn_ref, v_ref, g_ref, cw_ref, cb_ref, h_ref, xs, qs, ks, c_st, n_st, m_st,
                  *, ts, nblk):
    d = pl.program_id(1)
    i = pl.program_id(2)
    j = jnp.where(d == 0, i, nblk - 1 - i)
    nch = ts // CHUNK

    @pl.when(i == 0)
    def _():
        c_st[...] = jnp.zeros_like(c_st)
        n_st[...] = jnp.zeros_like(n_st)
        m_st[...] = jnp.zeros_like(m_st)

    xs[HALO:HALO + ts, :] = x_ref[...]
    xs[0:HALO, :] = jnp.where(j == 0, 0.0, xp_ref[...])
    xs[HALO + ts:2 * HALO + ts, :] = jnp.where(j == nblk - 1, 0.0, xn_ref[...])
    for cb in range(2 * M_HEADS):
        cols = slice(cb * LANES, (cb + 1) * LANES)
        y = (xs[HALO - 1:HALO - 1 + ts, cols] * cw_ref[0:1, cols] + xs[HALO:HALO + ts, cols] * cw_ref[1:2, cols]
             + xs[HALO + 1:HALO + 1 + ts, cols] * cw_ref[2:3, cols] + cb_ref[:, cols])
        y = y * jax.nn.sigmoid(y)
        if cb < M_HEADS:
            qs[:, cols] = y.astype(BF16)
        else:
            ks[:, (cb - M_HEADS) * LANES:(cb - M_HEADS + 1) * LANES] = (y * (M_HEAD_DIM ** -0.5)).astype(BF16)

    row = lax.broadcasted_iota(jnp.int32, (CHUNK, CHUNK), 0)
    col = lax.broadcasted_iota(jnp.int32, (CHUNK, CHUNK), 1)
    mask = (row - col) * (1 - 2 * d) >= 0
    lane8 = lax.broadcasted_iota(jnp.int32, (8, CHUNK), 1)

    def chunk(cc, carry):
        ci = jnp.where(d == 0, cc, nch - 1 - cc)
        r0 = pl.multiple_of(ci * CHUNK, CHUNK)
        gg = g_ref[ci, pl.ds(pl.multiple_of(8 * d, 8), 8), :]
        lf = _log_sigmoid(gg)
        pre = lf
        for sh in (1, 2, 4, 8, 16, 32, 64):
            pre = pre + jnp.where(lane8 >= sh, pltpu.roll(pre, sh, axis=1), 0.0)
        tot = pre[:, CHUNK - 1:CHUNK]
        bcum = jnp.where(d == 0, pre, tot - pre + lf)[4:8]
        b_end = tot[4:8]
        li = gg[0:4]
        grow = b_end - bcum + li
        m_loc = jnp.max(grow, axis=1, keepdims=True)
        cols_t = jnp.concatenate([bcum, grow, jnp.zeros((CHUNK - 8, CHUNK), F32)], axis=0).T

        for hh in range(M_HEADS):
            hc = slice(hh * LANES, (hh + 1) * LANES)
            q = qs[pl.ds(r0, CHUNK), hc]
            k = ks[pl.ds(r0, CHUNK), hc]
            v = v_ref[pl.ds(r0, CHUNK), hc]
            b_t = cols_t[:, hh:hh + 1]
            g_t = cols_t[:, 4 + hh:5 + hh]
            b_s = bcum[hh:hh + 1, :]
            li_s = li[hh:hh + 1, :]
            m_prev = m_st[hh:hh + 1, 0:1]
            n_prev = n_st[hh:hh + 1, :]
            c_prev = c_st[hh]

            dlog = jnp.where(mask, b_t - b_s + li_s, -jnp.inf)
            a_t = b_t + m_prev
            m_t = jnp.maximum(a_t, jnp.max(dlog, axis=-1, keepdims=True))
            inter = jnp.exp(a_t - m_t)
            qk = _dot_nt(q, k) * jnp.exp(dlog - m_t)
            num = inter * _dot(q, c_prev.astype(BF16)) + _dot(qk.astype(BF16), v)
            den = (inter * jnp.sum(q.astype(F32) * n_prev, axis=-1, keepdims=True)
                   + jnp.sum(qk, axis=-1, keepdims=True))
            h_ref[pl.ds(r0, CHUNK), hc] = num / jnp.maximum(jnp.abs(den), jnp.exp(-m_t))

            ml = m_loc[hh:hh + 1, :]
            kw = k.astype(F32) * jnp.exp(g_t - ml)
            c_loc = _dot_tn(kw.astype(BF16), v)
            n_loc = jnp.sum(kw, axis=0, keepdims=True)
            be = b_end[hh:hh + 1, :]
            m_new = jnp.maximum(be + m_prev, ml)
            fa = jnp.exp(be + m_prev - m_new)
            fc = jnp.exp(ml - m_new)
            c_st[hh] = fa * c_prev + fc * c_loc
            n_st[hh:hh + 1, :] = fa * n_prev + fc * n_loc
            m_st[hh:hh + 1, :] = jnp.broadcast_to(m_new, (1, LANES))
        return carry

    lax.fori_loop(0, nch, chunk, 0)


def _mlstm(mqk, mv, g, conv_w, conv_b, *, batch, seq, ts):
    n = mqk.shape[0]
    nblk = seq // ts
    tsh = ts // HALO
    nh = n // HALO

    def blk(b, d, i):
        return b * nblk + jnp.where(d == 0, i, nblk - 1 - i)

    return pl.pallas_call(
        functools.partial(_mlstm_kernel, ts=ts, nblk=nblk),
        grid=(batch, 2, nblk),
        in_specs=[
            pl.BlockSpec((ts, 2 * M_WIDTH), lambda b, d, i: (blk(b, d, i), 0)),
            pl.BlockSpec((HALO, 2 * M_WIDTH), lambda b, d, i: (jnp.maximum(blk(b, d, i) * tsh - 1, 0), 0)),
            pl.BlockSpec((HALO, 2 * M_WIDTH), lambda b, d, i: (jnp.minimum((blk(b, d, i) + 1) * tsh, nh - 1), 0)),
            pl.BlockSpec((ts, M_WIDTH), lambda b, d, i: (blk(b, d, i), 0)),
            pl.BlockSpec((ts // CHUNK, 16, CHUNK), lambda b, d, i: (blk(b, d, i), 0, 0)),
            _const_spec(conv_w.shape), _const_spec(conv_b.shape),
        ],
        out_specs=pl.BlockSpec((None, ts, M_WIDTH), lambda b, d, i: (d, blk(b, d, i), 0)),
        out_shape=jax.ShapeDtypeStruct((2, n, M_WIDTH), F32),
        scratch_shapes=[
            pltpu.VMEM((ts + 2 * HALO, 2 * M_WIDTH), F32),
            pltpu.VMEM((ts, M_WIDTH), BF16),
            pltpu.VMEM((ts, M_WIDTH), BF16),
            pltpu.VMEM((M_HEADS, M_HEAD_DIM, M_HEAD_DIM), F32),
            pltpu.VMEM((8, LANES), F32),
            pltpu.VMEM((8, LANES), F32),
        ],
        compiler_params=_params("parallel", "parallel", "arbitrary"),
        name="mlstm",
    )(mqk, mqk, mqk, mv, g, conv_w, conv_b)


def _ab_out_kernel(x_ref, ya_ref, h_ref, mo_ref, on_ref, wa_ref, wm_ref, o_ref):
    hs = h_ref[0] + h_ref[1]
    parts = []
    for hh in range(M_HEADS):
        hc = slice(hh * LANES, (hh + 1) * LANES)
        t = hs[:, hc]
        t = t * lax.rsqrt(jnp.mean(t * t, axis=-1, keepdims=True) + EPS) * on_ref[:, hc]
        parts.append((jax.nn.sigmoid(mo_ref[:, hc]) * t).astype(BF16))
    ym = jnp.concatenate(parts, axis=1)
    o_ref[...] = x_ref[...] + _dot(ya_ref[...], wa_ref[...]) + _dot(ym, wm_ref[...])


def _ab_out(x, ya, h2, mo, on, wa, wm, *, tm):
    n = x.shape[0]
    row = lambda i: (i, 0)
    return pl.pallas_call(
        _ab_out_kernel,
        grid=(n // tm,),
        in_specs=[
            pl.BlockSpec((tm, D_MODEL), row),
            pl.BlockSpec((tm, MLA_HEADS * V_HEAD), row),
            pl.BlockSpec((2, tm, M_WIDTH), lambda i: (0, i, 0)),
            pl.BlockSpec((tm, M_WIDTH), row),
            _const_spec(on.shape), _const_spec(wa.shape), _const_spec(wm.shape),
        ],
        out_specs=pl.BlockSpec((tm, D_MODEL), row),
        out_shape=jax.ShapeDtypeStruct((n, D_MODEL), F32),
        compiler_params=_params("parallel"),
        name="ab_out",
    )(x, ya, h2, mo, on, wa, wm)


def _ffn_kernel(x_ref, xp_ref, xn_ref, nrm_ref, wup_ref, cw_ref, cb_ref, wdn_ref, fin_ref, o_ref, us, act,
                *, tm, tps, final):
    jj = pl.program_id(0) % tps
    xa = jnp.concatenate([
        jnp.where(jj == 0, 0.0, xp_ref[...]),
        x_ref[...],
        jnp.where(jj == tps - 1, 0.0, xn_ref[...]),
    ], axis=0)
    xn = _rms(xa, nrm_ref[...]).astype(BF16)

    def conv(slot, cols):
        u = us.at[slot]
        return (u[HALO - 1:HALO - 1 + tm, :] * cw_ref[0:1, cols] + u[HALO:HALO + tm, :] * cw_ref[1:2, cols]
                + u[HALO + 1:HALO + 1 + tm, :] * cw_ref[2:3, cols] + cb_ref[:, cols])

    for c in range(D_FF // FF_COLS):
        gcols = slice(c * FF_COLS, (c + 1) * FF_COLS)
        vcols = slice(D_FF + c * FF_COLS, D_FF + (c + 1) * FF_COLS)
        us[0] = _dot(xn, wup_ref[:, gcols])
        us[1] = _dot(xn, wup_ref[:, vcols])
        gate = conv(0, gcols)
        val = conv(1, vcols)
        act[:, gcols] = (gate * jax.nn.sigmoid(gate) * val).astype(BF16)
    y = x_ref[...] + _dot(act[...], wdn_ref[...])
    if final:
        y = _rms(y, fin_ref[...])
    o_ref[...] = y


def _ffn(x, nrm, wup, cw, cb, wdn, fin, *, seq, tm, final):
    n = x.shape[0]
    nt = n // tm
    tps = seq // tm
    tmh = tm // HALO
    nh = n // HALO
    row = lambda i: (i, 0)
    return pl.pallas_call(
        functools.partial(_ffn_kernel, tm=tm, tps=tps, final=final),
        grid=(nt,),
        in_specs=[
            pl.BlockSpec((tm, D_MODEL), row),
            pl.BlockSpec((HALO, D_MODEL), lambda i: (jnp.maximum(i * tmh - 1, 0), 0)),
            pl.BlockSpec((HALO, D_MODEL), lambda i: (jnp.minimum((i + 1) * tmh, nh - 1), 0)),
            _const_spec(nrm.shape), _const_spec(wup.shape), _const_spec(cw.shape), _const_spec(cb.shape),
            _const_spec(wdn.shape), _const_spec(fin.shape),
        ],
        out_specs=pl.BlockSpec((tm, D_MODEL), row),
        out_shape=jax.ShapeDtypeStruct((n, D_MODEL), F32),
        scratch_shapes=[
            pltpu.VMEM((2, tm + 2 * HALO, FF_COLS), F32),
            pltpu.VMEM((tm, D_FF), BF16),
        ],
        compiler_params=_params("parallel"),
        name="ffn_final" if final else "ffn",
    )(x, x, x, nrm, wup, cw, cb, wdn, fin)


def _gelu(x):
    return 0.5 * x * (1.0 + lax.erf(x * (2.0 ** -0.5)))


def _mix_c_kernel(x_ref, nrm_ref, win_ref, vn_ref, ws_ref, bs_ref, wout_ref, o_ref, zs, *, tm):
    x = x_ref[...]
    xn = _rms(x, nrm_ref[...]).astype(BF16)
    u = _gelu(_dot(xn, win_ref[:, 0:D_MODEL]))
    v = _gelu(_dot(xn, win_ref[:, D_MODEL:2 * D_MODEL]))
    vn = _rms(v, vn_ref[...]).astype(BF16)
    nch = tm // CHUNK
    for g in range(G_GROUPS):
        cols = slice(g * LANES, (g + 1) * LANES)
        rhs = jnp.concatenate([vn[c * CHUNK:(c + 1) * CHUNK, cols] for c in range(nch)], axis=1)
        sv = _dot(ws_ref[g], rhs) + bs_ref[:, g:g + 1]
        for c in range(nch):
            zs[c * CHUNK:(c + 1) * CHUNK, cols] = (
                u[c * CHUNK:(c + 1) * CHUNK, cols] * sv[:, c * LANES:(c + 1) * LANES]).astype(BF16)
    o_ref[...] = x + _dot(zs[...], wout_ref[...])


def _mix_c(x, nrm, win, vn, ws, bs, wout, *, tm):
    n = x.shape[0]
    row = lambda i: (i, 0)
    return pl.pallas_call(
        functools.partial(_mix_c_kernel, tm=tm),
        grid=(n // tm,),
        in_specs=[
            pl.BlockSpec((tm, D_MODEL), row),
            _const_spec(nrm.shape), _const_spec(win.shape), _const_spec(vn.shape), _const_spec(ws.shape),
            _const_spec(bs.shape), _const_spec(wout.shape),
        ],
        out_specs=pl.BlockSpec((tm, D_MODEL), row),
        out_shape=jax.ShapeDtypeStruct((n, D_MODEL), F32),
        scratch_shapes=[pltpu.VMEM((tm, D_MODEL), BF16)],
        compiler_params=_params("parallel"),
        name="mix_c",
    )(x, nrm, win, vn, ws, bs, wout)


def _rope_tables(seq):
    pos = jnp.arange(seq, dtype=F32)
    inv = 1.0 / (ROPE_THETA ** (jnp.arange(0, QK_ROPE, 2, dtype=F32) / QK_ROPE))
    ang = pos[:, None] * inv[None, :]
    cos, sin = jnp.cos(ang), jnp.sin(ang)
    one = jnp.ones((seq, QK_NOPE), F32)
    zero_n = jnp.zeros((seq, QK_NOPE), F32)
    pad = jnp.zeros((seq, LANES - QK_NOPE - QK_ROPE), F32)
    return (jnp.concatenate([one, cos, cos, pad], axis=1), jnp.concatenate([zero_n, sin, sin, pad], axis=1))


def _head_block(nope, x1, x2):
    pad = jnp.zeros((nope.shape[0], LANES - QK_NOPE - QK_ROPE), nope.dtype)
    return jnp.concatenate([nope, x1, x2, pad], axis=1)


def _prep_even(i, ab_norm, ab_w_in, mla_q_norm, mla_w_uq, mla_kv_norm, mla_w_ukv, mlstm_conv_w, mlstm_conv_b,
               mlstm_gate_bias, mlstm_out_norm, ab_w_out):
    w_in = ab_w_in[i]
    half = QK_ROPE // 2
    zq = jnp.zeros((D_MODEL, QK_NOPE), F32)
    kr1 = w_in[:, 384:384 + half]
    kr2 = w_in[:, 384 + half:416]
    gates = jnp.pad(w_in[:, 2464:2480], ((0, 0), (0, LANES - 16)))
    w_big = jnp.concatenate([
        w_in[:, 0:384],
        _head_block(zq, kr1, kr2),
        _head_block(zq, -kr2, kr1),
        w_in[:, 416:2464],
        gates,
    ], axis=1).astype(BF16)
    w_uq = mla_w_uq[i].reshape(Q_LORA, MLA_HEADS, QK_NOPE + QK_ROPE)
    zn = jnp.zeros((Q_LORA, QK_NOPE), F32)
    wqa = jnp.concatenate([_head_block(w_uq[:, h, :QK_NOPE], w_uq[:, h, QK_NOPE:QK_NOPE + half],
                                       w_uq[:, h, QK_NOPE + half:]) for h in range(MLA_HEADS)], axis=1)
    wqb = jnp.concatenate([_head_block(zn, -w_uq[:, h, QK_NOPE + half:], w_uq[:, h, QK_NOPE:QK_NOPE + half])
                           for h in range(MLA_HEADS)], axis=1)
    w_ukv = mla_w_ukv[i].reshape(KV_LORA, MLA_HEADS, QK_NOPE + V_HEAD)
    wka = jnp.pad(w_ukv[:, :, :QK_NOPE], ((0, 0), (0, 0), (0, LANES - QK_NOPE))).reshape(KV_LORA, MLA_HEADS * LANES)
    zv = jnp.zeros((KV_LORA, V_HEAD), F32)
    wv = jnp.concatenate([jnp.concatenate([w_ukv[:, h, QK_NOPE:], zv] if h % 2 == 0 else [zv, w_ukv[:, h, QK_NOPE:]],
                                          axis=1) for h in range(MLA_HEADS)], axis=1)
    one_even = jnp.zeros((LANES,), F32).at[V_HEAD].set(1.0)
    one_odd = jnp.zeros((LANES,), F32).at[0].set(1.0)
    vone = jnp.concatenate([one_even if h % 2 == 0 else one_odd for h in range(MLA_HEADS)])[None, :]
    w_out = ab_w_out[i].astype(BF16)
    return dict(
        nrm=ab_norm[i][None, :], w_big=w_big, qn=mla_q_norm[i][None, :], wqa=wqa.astype(BF16),
        wqb=wqb.astype(BF16), kvn=mla_kv_norm[i][None, :], wka=wka.astype(BF16), wv=wv.astype(BF16),
        gb=jnp.broadcast_to(mlstm_gate_bias[i][:, None], (16, LANES)), vone=vone,
        conv_w=mlstm_conv_w[i], conv_b=mlstm_conv_b[i][None, :], on=mlstm_out_norm[i][None, :],
        wa=w_out[:MLA_HEADS * V_HEAD], wm=w_out[MLA_HEADS * V_HEAD:])


def _pick(pref, seq):
    return min(pref, seq)


def _forward(x3, even, odd, ffn, final_norm):
    batch, seq, _ = x3.shape
    x = x3.reshape(batch * seq, D_MODEL)
    tm = _pick(512, seq)
    cos_t, sin_t = _rope_tables(seq)
    e = even
    q, k, v, mqk, mv, mo, g = _ab_in(x, cos_t, sin_t, e["nrm"], e["w_big"], e["qn"], e["wqa"], e["wqb"], e["kvn"],
                                     e["wka"], e["wv"], e["gb"], e["vone"], seq=seq, tm=tm)
    ya = _attn(q, k, v, batch=batch, seq=seq, tq=_pick(256, seq))
    h2 = _mlstm(mqk, mv, g, e["conv_w"], e["conv_b"], batch=batch, seq=seq, ts=_pick(512, seq))
    x = _ab_out(x, ya, h2, mo, e["on"], e["wa"], e["wm"], tm=tm)
    f = ffn[0]
    x = _ffn(x, f["nrm"], f["wup"], f["cw"], f["cb"], f["wdn"], final_norm, seq=seq, tm=tm, final=False)
    o = odd
    x = _mix_c(x, o["nrm"], o["win"], o["vn"], o["ws"], o["bs"], o["wout"], tm=tm)
    f = ffn[1]
    x = _ffn(x, f["nrm"], f["wup"], f["cw"], f["cb"], f["wdn"], final_norm, seq=seq, tm=tm, final=True)
    return x.reshape(batch, seq, D_MODEL)


def kernel(x_prompt, x_sample, ab_norm, ab_w_in, mla_q_norm, mla_w_uq, mla_kv_norm, mla_w_ukv, mlstm_conv_w,
           mlstm_conv_b, mlstm_gate_bias, mlstm_out_norm, ab_w_out, c_norm, c_w_in, c_v_norm, c_w_spatial,
           c_b_spatial, c_w_out, ffn_norm, ffn_w_up, ffn_conv_w, ffn_conv_b, ffn_w_down, final_norm):
    even = _prep_even(0, ab_norm, ab_w_in, mla_q_norm, mla_w_uq, mla_kv_norm, mla_w_ukv, mlstm_conv_w, mlstm_conv_b,
                      mlstm_gate_bias, mlstm_out_norm, ab_w_out)
    odd = dict(nrm=c_norm[0][None, :], win=c_w_in[0].astype(BF16), vn=c_v_norm[0][None, :],
               ws=c_w_spatial[0].astype(BF16), bs=c_b_spatial[0].T, wout=c_w_out[0].astype(BF16))
    ffn = [dict(nrm=ffn_norm[l][None, :], wup=ffn_w_up[l].astype(BF16), cw=ffn_conv_w[l], cb=ffn_conv_b[l][None, :],
                wdn=ffn_w_down[l].astype(BF16)) for l in range(2)]
    fin = final_norm[None, :]
    return (_forward(x_prompt, even, odd, ffn, fin), _forward(x_sample, even, odd, ffn, fin))
```

```python
import functools

import jax
import jax.numpy as jnp
from jax import lax
from jax.experimental import pallas as pl
from jax.experimental.pallas import tpu as pltpu

F32 = jnp.float32
BF16 = jnp.bfloat16

EPS = 1e-6
D_MODEL = 1024
MLA_HEADS = 8
Q_LORA = 256
KV_LORA = 128
QK_NOPE = 64
QK_ROPE = 32
V_HEAD = 64
ROPE_THETA = 10000.0
M_HEADS = 4
M_HEAD_DIM = 128
M_WIDTH = M_HEADS * M_HEAD_DIM
N_STREAMS = 2 * M_HEADS
CHUNK = 128
BLOCK_CHUNKS = 4
G_GROUPS = 8
D_FF = 2816
FF_COLS = 256
LANES = 128
SUBLANES = 8
HALO = SUBLANES
KEY_TILE = 256
LOG2_E = 1.4426950408889634
VMEM_LIMIT = 56 * 1024 * 1024

_C_Q, _C_KV, _C_KRA, _C_KRB, _C_MQK, _C_MV, _C_MO, _C_G, _C_END = (
    0, 256, 384, 512, 640, 1664, 2176, 2688, 2816)


def _rms(x, g):
    return x * lax.rsqrt(jnp.mean(x * x, axis=-1, keepdims=True) + EPS) * g


def _const_spec(shape):
    nd = len(shape)
    return pl.BlockSpec(shape, lambda *_: (0,) * nd, pipeline_mode=pl.Buffered(1))


def _params(*sem):
    return pltpu.CompilerParams(dimension_semantics=sem, vmem_limit_bytes=VMEM_LIMIT)


def _dot(a, b):
    return jnp.dot(a, b, preferred_element_type=F32)


def _dot_nt(a, b):
    return lax.dot_general(a, b, (((1,), (1,)), ((), ())), preferred_element_type=F32)


def _dot_tn(a, b):
    return lax.dot_general(a, b, (((0,), (0,)), ((), ())), preferred_element_type=F32)


def _halo_specs(tile_rows, n_rows, width):
    tb = tile_rows // HALO
    last = n_rows // HALO - 1
    return (pl.BlockSpec((HALO, width), lambda i: (jnp.maximum(i * tb - 1, 0), 0)),
            pl.BlockSpec((HALO, width), lambda i: (jnp.minimum((i + 1) * tb, last), 0)))


def _conv3(u, rows, w_ref, b_ref, cols):
    return (u[HALO - 1:HALO - 1 + rows, :] * w_ref[0:1, cols] + u[HALO:HALO + rows, :] * w_ref[1:2, cols]
            + u[HALO + 1:HALO + 1 + rows, :] * w_ref[2:3, cols] + b_ref[:, cols])


def _ab_in_kernel(x_ref, xp_ref, xn_ref, cos_ref, sin_ref, nrm_ref, w_ref, qn_ref, wqa_ref, wqb_ref, kvn_ref,
                  wka_ref, wv_ref, gb_ref, vone_ref, cw_ref, cb_ref,
                  q_ref, k_ref, v_ref, mq_ref, mk_ref, mv_ref, mo_ref, g_ref, us, *, tm, tps):
    jj = pl.program_id(0) % tps
    xa = jnp.concatenate([
        jnp.where(jj == 0, 0.0, xp_ref[...]),
        x_ref[...],
        jnp.where(jj == tps - 1, 0.0, xn_ref[...]),
    ], axis=0)
    xe = _rms(xa, nrm_ref[...]).astype(BF16)
    xn = xe[HALO:HALO + tm]

    def proj(a, b):
        return _dot(xn, w_ref[:, a:b])

    for cb in range(2 * M_HEADS):
        cols = slice(cb * LANES, (cb + 1) * LANES)
        us[cb % 2] = _dot(xe, w_ref[:, _C_MQK + cb * LANES:_C_MQK + (cb + 1) * LANES])
        y = _conv3(us.at[cb % 2], tm, cw_ref, cb_ref, cols)
        y = y * jax.nn.sigmoid(y)
        if cb < M_HEADS:
            mq_ref[:, cols] = y.astype(BF16)
        else:
            mk_ref[:, (cb - M_HEADS) * LANES:(cb - M_HEADS + 1) * LANES] = (y * (M_HEAD_DIM ** -0.5)).astype(BF16)
    mv_ref[...] = proj(_C_MV, _C_MO).astype(BF16)
    mo_ref[...] = proj(_C_MO, _C_G)
    gt = proj(_C_G, _C_END).T
    for c in range(tm // CHUNK):
        g_ref[c] = gt[0:2 * N_STREAMS, c * CHUNK:(c + 1) * CHUNK] + gb_ref[:, 0:1]

    cos = cos_ref[...]
    sin = sin_ref[...]
    scale = (QK_NOPE + QK_ROPE) ** -0.5 * LOG2_E
    cqn = _rms(proj(_C_Q, _C_KV), qn_ref[...]).astype(BF16)
    qa = _dot(cqn, wqa_ref[...])
    qb = _dot(cqn, wqb_ref[...])
    cos_q = cos * scale
    sin_q = sin * scale
    for h in range(MLA_HEADS):
        blk = slice(h * LANES, (h + 1) * LANES)
        q_ref[:, blk] = (qa[:, blk] * cos_q + qb[:, blk] * sin_q).astype(BF16)

    ckvn = _rms(proj(_C_KV, _C_KRA), kvn_ref[...]).astype(BF16)
    ka = _dot(ckvn, wka_ref[...])
    kr = proj(_C_KRA, _C_KRB) * cos + proj(_C_KRB, _C_MQK) * sin
    for h in range(MLA_HEADS):
        blk = slice(h * LANES, (h + 1) * LANES)
        k_ref[:, blk] = (ka[:, blk] + kr).astype(BF16)
    v_ref[...] = (_dot(ckvn, wv_ref[...]) + vone_ref[...]).astype(BF16)


def _ab_in(x, cos_t, sin_t, e, *, seq, tm):
    n = x.shape[0]
    nt = n // tm
    tps = seq // tm
    row = lambda i: (i, 0)
    tab = lambda i: (i % tps, 0)
    consts = [e[k] for k in ("nrm", "w_big", "qn", "wqa", "wqb", "kvn", "wka", "wv", "gb", "vone", "conv_w",
                             "conv_b")]
    out_shape = (
        jax.ShapeDtypeStruct((n, MLA_HEADS * LANES), BF16),
        jax.ShapeDtypeStruct((n, MLA_HEADS * LANES), BF16),
        jax.ShapeDtypeStruct((n, MLA_HEADS * LANES), BF16),
        jax.ShapeDtypeStruct((n, M_WIDTH), BF16),
        jax.ShapeDtypeStruct((n, M_WIDTH), BF16),
        jax.ShapeDtypeStruct((n, M_WIDTH), BF16),
        jax.ShapeDtypeStruct((n, M_WIDTH), F32),
        jax.ShapeDtypeStruct((n // CHUNK, 2 * N_STREAMS, CHUNK), F32),
    )
    wide = pl.BlockSpec((tm, MLA_HEADS * LANES), row)
    narrow = pl.BlockSpec((tm, M_WIDTH), row)
    return pl.pallas_call(
        functools.partial(_ab_in_kernel, tm=tm, tps=tps),
        grid=(nt,),
        in_specs=[pl.BlockSpec((tm, D_MODEL), row), *_halo_specs(tm, n, D_MODEL),
                  pl.BlockSpec((tm, LANES), tab), pl.BlockSpec((tm, LANES), tab)]
                 + [_const_spec(c.shape) for c in consts],
        out_specs=(wide, wide, wide, narrow, narrow, narrow, narrow,
                   pl.BlockSpec((tm // CHUNK, 2 * N_STREAMS, CHUNK), lambda i: (i, 0, 0))),
        out_shape=out_shape,
        scratch_shapes=[pltpu.VMEM((2, tm + 2 * HALO, LANES), F32)],
        compiler_params=_params("parallel"),
        name="ab_in",
    )(x, x, x, cos_t, sin_t, *consts)


def _attn_kernel(q_ref, k_ref, v_ref, o_ref, s_scr, *, tq):
    nt = k_ref.shape[0] // KEY_TILE
    mbs = []
    for h in range(2):
        hc = slice(h * LANES, (h + 1) * LANES)
        q = q_ref[:, hc]
        mx = jnp.full((tq, LANES), -jnp.inf, F32)
        for t in range(nt):
            keys = slice(t * KEY_TILE, (t + 1) * KEY_TILE)
            s = _dot_nt(q, k_ref[keys, hc])
            s_scr[h, :, keys] = s
            for j in range(KEY_TILE // LANES):
                mx = jnp.maximum(mx, s[:, j * LANES:(j + 1) * LANES])
        mbs.append(jnp.broadcast_to(jnp.max(mx, axis=-1, keepdims=True), (tq, LANES)))
    outs = []
    for h in range(2):
        hc = slice(h * LANES, (h + 1) * LANES)
        mb = jnp.concatenate([mbs[h]] * (KEY_TILE // LANES), axis=1)
        acc = jnp.zeros((tq, LANES), F32)
        for t in range(nt):
            keys = slice(t * KEY_TILE, (t + 1) * KEY_TILE)
            p = jnp.exp2(s_scr[h, :, keys] - mb).astype(BF16)
            acc = acc + _dot(p, v_ref[keys, hc])
        ones_lane = V_HEAD if h == 0 else 0
        outs.append(acc / acc[:, ones_lane:ones_lane + 1])
    lane = lax.broadcasted_iota(jnp.int32, (tq, LANES), 1)
    o_ref[...] = jnp.where(lane < V_HEAD, outs[0], outs[1]).astype(BF16)


def _attn(q, k, v, *, batch, seq, tq):
    n = q.shape[0]
    nq = seq // tq
    pairs = MLA_HEADS // 2
    return pl.pallas_call(
        functools.partial(_attn_kernel, tq=tq),
        grid=(batch, pairs, nq),
        in_specs=[
            pl.BlockSpec((tq, 2 * LANES), lambda b, p, i: (b * nq + i, p)),
            pl.BlockSpec((seq, 2 * LANES), lambda b, p, i: (b, p)),
            pl.BlockSpec((seq, 2 * LANES), lambda b, p, i: (b, p)),
        ],
        out_specs=pl.BlockSpec((tq, LANES), lambda b, p, i: (b * nq + i, p)),
        out_shape=jax.ShapeDtypeStruct((n, MLA_HEADS * V_HEAD), BF16),
        scratch_shapes=[pltpu.VMEM((2, tq, seq), F32)],
        compiler_params=_params("parallel", "parallel", "arbitrary"),
        name="attn",
    )(q, k, v)


def _log_sigmoid(x):
    return jnp.minimum(x, 0.0) - jnp.log1p(jnp.exp(-jnp.abs(x)))


def _gates_kernel(g_ref, c_ref, dec_ref, cols_ref, tot_s, mloc_s, mpf_s, mpb_s, *, nc):
    rows = nc * N_STREAMS
    shape = (rows, CHUNK)
    three = (nc, N_STREAMS, CHUNK)
    li = g_ref[:, 0:N_STREAMS, :].reshape(shape)
    lf = _log_sigmoid(g_ref[:, N_STREAMS:2 * N_STREAMS, :].reshape(shape))
    fwd = (lax.broadcasted_iota(jnp.int32, shape, 0) & M_HEADS) == 0
    lane = lax.broadcasted_iota(jnp.int32, shape, 1)
    shifts = (1, 2, 4, 8, 16, 32, 64)

    pre = lf
    for sh in shifts:
        pre = pre + jnp.where(lane >= sh, pltpu.roll(pre, sh, axis=1), 0.0)
    tot = jnp.broadcast_to(pre[:, CHUNK - 1:CHUNK], shape)
    b = jnp.where(fwd, pre, tot - pre + lf)
    c = li - b
    run_f = c
    run_b = c
    for sh in shifts:
        run_f = jnp.maximum(run_f, jnp.where(lane >= sh, pltpu.roll(run_f, sh, axis=1), -jnp.inf))
        run_b = jnp.maximum(run_b, jnp.where(lane < CHUNK - sh, pltpu.roll(run_b, CHUNK - sh, axis=1), -jnp.inf))
    run = jnp.where(fwd, run_f, run_b)
    m_loc = tot + jnp.broadcast_to(jnp.max(c, axis=1, keepdims=True), shape)

    tot_s[...] = tot.reshape(three)
    mloc_s[...] = m_loc.reshape(three)
    fwd8 = lax.broadcasted_iota(jnp.int32, (N_STREAMS, CHUNK), 0) < M_HEADS

    def step(i, m):
        jb = nc - 1 - i
        mpf_s[i] = m
        mpb_s[jb] = m
        return jnp.maximum(jnp.where(fwd8, tot_s[i], tot_s[jb]) + m, jnp.where(fwd8, mloc_s[i], mloc_s[jb]))

    lax.fori_loop(0, nc, step, jnp.zeros((N_STREAMS, CHUNK), F32))
    m_prev = jnp.where(fwd, mpf_s[...].reshape(shape), mpb_s[...].reshape(shape))

    m_run = jnp.maximum(m_prev, run)
    inter = jnp.exp(m_prev - m_run)
    floor = jnp.exp(-(b + m_run))
    w = jnp.exp(tot + c - m_loc)
    m_new = jnp.maximum(tot + m_prev, m_loc)
    c_ref[...] = c.reshape(three)
    dec_ref[:, 0:N_STREAMS, :] = jnp.exp(tot + m_prev - m_new).reshape(three)
    dec_ref[:, N_STREAMS:2 * N_STREAMS, :] = jnp.exp(m_loc - m_new).reshape(three)
    quantities = [a.reshape(three) for a in (m_run, inter, floor, w)]
    for blk in range(nc // BLOCK_CHUNKS):
        tile = jnp.concatenate([a[blk * BLOCK_CHUNKS + cq] for cq in range(BLOCK_CHUNKS) for a in quantities], axis=0)
        cols_ref[blk] = tile.T


def _gates(g, *, batch, seq):
    nc = seq // CHUNK
    nchunks = g.shape[0]
    three = (nc, N_STREAMS, CHUNK)
    return pl.pallas_call(
        functools.partial(_gates_kernel, nc=nc),
        grid=(batch,),
        in_specs=[pl.BlockSpec((nc, 2 * N_STREAMS, CHUNK), lambda b: (b, 0, 0))],
        out_specs=(
            pl.BlockSpec(three, lambda b: (b, 0, 0)),
            pl.BlockSpec((nc, 2 * N_STREAMS, CHUNK), lambda b: (b, 0, 0)),
            pl.BlockSpec((nc // BLOCK_CHUNKS, CHUNK, LANES), lambda b: (b, 0, 0)),
        ),
        out_shape=(
            jax.ShapeDtypeStruct((nchunks, N_STREAMS, CHUNK), F32),
            jax.ShapeDtypeStruct((nchunks, 2 * N_STREAMS, CHUNK), F32),
            jax.ShapeDtypeStruct((nchunks // BLOCK_CHUNKS, CHUNK, LANES), F32),
        ),
        scratch_shapes=[pltpu.VMEM(three, F32)] * 4,
        compiler_params=_params("parallel"),
        name="gates",
    )(g)


def _mlstm_kernel(q_ref, k_ref, v_ref, c_ref, dec_ref, cols_ref, h_ref, st, *, direction):
    @pl.when(pl.program_id(1) == 0)
    def _():
        st[...] = jnp.zeros_like(st)

    row = lax.broadcasted_iota(jnp.int32, (CHUNK, CHUNK), 0)
    col = lax.broadcasted_iota(jnp.int32, (CHUNK, CHUNK), 1)
    mask = (col <= row) if direction == 0 else (col >= row)
    ones = (col == 0).astype(BF16)
    cols_t = cols_ref[0]
    order = range(BLOCK_CHUNKS) if direction == 0 else range(BLOCK_CHUNKS - 1, -1, -1)
    for hh in range(M_HEADS):
        hc = slice(hh * LANES, (hh + 1) * LANES)
        s = direction * M_HEADS + hh
        state = st[hh]
        for cq in order:
            rows = slice(cq * CHUNK, (cq + 1) * CHUNK)
            q = q_ref[rows, hc]
            k = k_ref[rows, hc]
            v1 = jnp.concatenate([v_ref[rows, hc], ones], axis=1)
            base = cq * 4 * N_STREAMS + s
            m_run, inter, floor, w = (cols_t[:, base + qi * N_STREAMS:base + qi * N_STREAMS + 1] for qi in range(4))
            d = jnp.where(mask, jnp.exp(c_ref[cq, s:s + 1, :] - m_run), 0.0)
            qk = (_dot_nt(q, k) * d).astype(BF16)
            nd = inter * _dot(q, state.astype(BF16)) + _dot(qk, v1)
            den = nd[:, LANES:LANES + 1]
            h_ref[rows, hc] = nd[:, 0:LANES] / jnp.maximum(jnp.abs(den), floor)
            kw = (k.astype(F32) * w).astype(BF16)
            state = (dec_ref[cq, s:s + 1, 0:1] * state
                     + dec_ref[cq, N_STREAMS + s:N_STREAMS + s + 1, 0:1] * _dot_tn(kw, v1))
        st[hh] = state


def _mlstm(mq, mk, mv, c_rows, dec, cols, *, batch, seq, direction):
    n = mq.shape[0]
    ts = BLOCK_CHUNKS * CHUNK
    nblk = seq // ts

    def blk(b, i):
        return b * nblk + (i if direction == 0 else nblk - 1 - i)

    tok = pl.BlockSpec((ts, M_WIDTH), lambda b, i: (blk(b, i), 0))
    return pl.pallas_call(
        functools.partial(_mlstm_kernel, direction=direction),
        grid=(batch, nblk),
        in_specs=[
            tok, tok, tok,
            pl.BlockSpec((BLOCK_CHUNKS, N_STREAMS, CHUNK), lambda b, i: (blk(b, i), 0, 0)),
            pl.BlockSpec((BLOCK_CHUNKS, 2 * N_STREAMS, CHUNK), lambda b, i: (blk(b, i), 0, 0)),
            pl.BlockSpec((1, CHUNK, LANES), lambda b, i: (blk(b, i), 0, 0)),
        ],
        out_specs=tok,
        out_shape=jax.ShapeDtypeStruct((n, M_WIDTH), F32),
        scratch_shapes=[pltpu.VMEM((M_HEADS, M_HEAD_DIM, 2 * LANES), F32)],
        compiler_params=_params("parallel", "arbitrary"),
        name="mlstm_fwd" if direction == 0 else "mlstm_bwd",
    )(mq, mk, mv, c_rows, dec, cols)


def _ab_out_kernel(x_ref, ya_ref, hf_ref, hb_ref, mo_ref, on_ref, wa_ref, wm_ref, o_ref):
    parts = []
    for hh in range(M_HEADS):
        hc = slice(hh * LANES, (hh + 1) * LANES)
        t = hf_ref[:, hc] + hb_ref[:, hc]
        t = t * lax.rsqrt(jnp.mean(t * t, axis=-1, keepdims=True) + EPS) * on_ref[:, hc]
        parts.append((jax.nn.sigmoid(mo_ref[:, hc]) * t).astype(BF16))
    ym = jnp.concatenate(parts, axis=1)
    o_ref[...] = x_ref[...] + _dot(ya_ref[...], wa_ref[...]) + _dot(ym, wm_ref[...])


def _ab_out(x, ya, hf, hb, mo, on, wa, wm, *, tm):
    n = x.shape[0]
    row = lambda i: (i, 0)
    narrow = pl.BlockSpec((tm, M_WIDTH), row)
    return pl.pallas_call(
        _ab_out_kernel,
        grid=(n // tm,),
        in_specs=[
            pl.BlockSpec((tm, D_MODEL), row), narrow, narrow, narrow, narrow,
            _const_spec(on.shape), _const_spec(wa.shape), _const_spec(wm.shape),
        ],
        out_specs=pl.BlockSpec((tm, D_MODEL), row),
        out_shape=jax.ShapeDtypeStruct((n, D_MODEL), F32),
        compiler_params=_params("parallel"),
        name="ab_out",
    )(x, ya, hf, hb, mo, on, wa, wm)


def _ffn_kernel(x_ref, xp_ref, xn_ref, nrm_ref, wup_ref, cw_ref, cb_ref, wdn_ref, fin_ref, o_ref, us, act,
                *, tm, tps, final):
    jj = pl.program_id(0) % tps
    xa = jnp.concatenate([
        jnp.where(jj == 0, 0.0, xp_ref[...]),
        x_ref[...],
        jnp.where(jj == tps - 1, 0.0, xn_ref[...]),
    ], axis=0)
    xn = _rms(xa, nrm_ref[...]).astype(BF16)

    for c in range(D_FF // FF_COLS):
        gcols = slice(c * FF_COLS, (c + 1) * FF_COLS)
        vcols = slice(D_FF + c * FF_COLS, D_FF + (c + 1) * FF_COLS)
        us[0] = _dot(xn, wup_ref[:, gcols])
        us[1] = _dot(xn, wup_ref[:, vcols])
        gate = _conv3(us.at[0], tm, cw_ref, cb_ref, gcols)
        val = _conv3(us.at[1], tm, cw_ref, cb_ref, vcols)
        act[:, gcols] = (gate * jax.nn.sigmoid(gate) * val).astype(BF16)
    y = x_ref[...] + _dot(act[...], wdn_ref[...])
    if final:
        y = _rms(y, fin_ref[...])
    o_ref[...] = y


def _ffn(x, nrm, wup, cw, cb, wdn, fin, *, seq, tm, final):
    n = x.shape[0]
    row = lambda i: (i, 0)
    return pl.pallas_call(
        functools.partial(_ffn_kernel, tm=tm, tps=seq // tm, final=final),
        grid=(n // tm,),
        in_specs=[
            pl.BlockSpec((tm, D_MODEL), row), *_halo_specs(tm, n, D_MODEL),
            _const_spec(nrm.shape), _const_spec(wup.shape), _const_spec(cw.shape), _const_spec(cb.shape),
            _const_spec(wdn.shape), _const_spec(fin.shape),
        ],
        out_specs=pl.BlockSpec((tm, D_MODEL), row),
        out_shape=jax.ShapeDtypeStruct((n, D_MODEL), F32),
        scratch_shapes=[
            pltpu.VMEM((2, tm + 2 * HALO, FF_COLS), F32),
            pltpu.VMEM((tm, D_FF), BF16),
        ],
        compiler_params=_params("parallel"),
        name="ffn_final" if final else "ffn",
    )(x, x, x, nrm, wup, cw, cb, wdn, fin)


def _gelu(x):
    return 0.5 * x * (1.0 + lax.erf(x * (2.0 ** -0.5)))


def _mix_c_kernel(x_ref, nrm_ref, win_ref, vn_ref, ws_ref, bs_ref, wout_ref, o_ref, zs, *, tm):
    x = x_ref[...]
    xn = _rms(x, nrm_ref[...]).astype(BF16)
    u = _gelu(_dot(xn, win_ref[:, 0:D_MODEL]))
    v = _gelu(_dot(xn, win_ref[:, D_MODEL:2 * D_MODEL]))
    vn = _rms(v, vn_ref[...]).astype(BF16)
    nch = tm // CHUNK
    for g in range(G_GROUPS):
        cols = slice(g * LANES, (g + 1) * LANES)
        rhs = jnp.concatenate([vn[c * CHUNK:(c + 1) * CHUNK, cols] for c in range(nch)], axis=1)
        sv = _dot(ws_ref[g], rhs) + bs_ref[:, g:g + 1]
        for c in range(nch):
            zs[c * CHUNK:(c + 1) * CHUNK, cols] = (
                u[c * CHUNK:(c + 1) * CHUNK, cols] * sv[:, c * LANES:(c + 1) * LANES]).astype(BF16)
    o_ref[...] = x + _dot(zs[...], wout_ref[...])


def _mix_c(x, nrm, win, vn, ws, bs, wout, *, tm):
    n = x.shape[0]
    row = lambda i: (i, 0)
    return pl.pallas_call(
        functools.partial(_mix_c_kernel, tm=tm),
        grid=(n // tm,),
        in_specs=[
            pl.BlockSpec((tm, D_MODEL), row),
            _const_spec(nrm.shape), _const_spec(win.shape), _const_spec(vn.shape), _const_spec(ws.shape),
            _const_spec(bs.shape), _const_spec(wout.shape),
        ],
        out_specs=pl.BlockSpec((tm, D_MODEL), row),
        out_shape=jax.ShapeDtypeStruct((n, D_MODEL), F32),
        scratch_shapes=[pltpu.VMEM((tm, D_MODEL), BF16)],
        compiler_params=_params("parallel"),
        name="mix_c",
    )(x, nrm, win, vn, ws, bs, wout)


def _rope_tables(seq):
    pos = jnp.arange(seq, dtype=F32)
    inv = 1.0 / (ROPE_THETA ** (jnp.arange(0, QK_ROPE, 2, dtype=F32) / QK_ROPE))
    ang = pos[:, None] * inv[None, :]
    cos, sin = jnp.cos(ang), jnp.sin(ang)
    one = jnp.ones((seq, QK_NOPE), F32)
    zero_n = jnp.zeros((seq, QK_NOPE), F32)
    pad = jnp.zeros((seq, LANES - QK_NOPE - QK_ROPE), F32)
    return (jnp.concatenate([one, cos, cos, pad], axis=1), jnp.concatenate([zero_n, sin, sin, pad], axis=1))


def _head_block(nope, x1, x2):
    pad = jnp.zeros((nope.shape[0], LANES - QK_NOPE - QK_ROPE), nope.dtype)
    return jnp.concatenate([nope, x1, x2, pad], axis=1)


def _prep_even(i, ab_norm, ab_w_in, mla_q_norm, mla_w_uq, mla_kv_norm, mla_w_ukv, mlstm_conv_w, mlstm_conv_b,
               mlstm_gate_bias, mlstm_out_norm, ab_w_out):
    w_in = ab_w_in[i]
    half = QK_ROPE // 2
    zq = jnp.zeros((D_MODEL, QK_NOPE), F32)
    kr1 = w_in[:, 384:384 + half]
    kr2 = w_in[:, 384 + half:416]
    gate_order = jnp.array([0, 1, 2, 3, 8, 9, 10, 11, 4, 5, 6, 7, 12, 13, 14, 15])
    gates = jnp.pad(w_in[:, 2464:2480][:, gate_order], ((0, 0), (0, LANES - 2 * N_STREAMS)))
    w_big = jnp.concatenate([
        w_in[:, 0:384],
        _head_block(zq, kr1, kr2),
        _head_block(zq, -kr2, kr1),
        w_in[:, 416:2464],
        gates,
    ], axis=1).astype(BF16)
    w_uq = mla_w_uq[i].reshape(Q_LORA, MLA_HEADS, QK_NOPE + QK_ROPE)
    zn = jnp.zeros((Q_LORA, QK_NOPE), F32)
    wqa = jnp.concatenate([_head_block(w_uq[:, h, :QK_NOPE], w_uq[:, h, QK_NOPE:QK_NOPE + half],
                                       w_uq[:, h, QK_NOPE + half:]) for h in range(MLA_HEADS)], axis=1)
    wqb = jnp.concatenate([_head_block(zn, -w_uq[:, h, QK_NOPE + half:], w_uq[:, h, QK_NOPE:QK_NOPE + half])
                           for h in range(MLA_HEADS)], axis=1)
    w_ukv = mla_w_ukv[i].reshape(KV_LORA, MLA_HEADS, QK_NOPE + V_HEAD)
    wka = jnp.pad(w_ukv[:, :, :QK_NOPE], ((0, 0), (0, 0), (0, LANES - QK_NOPE))).reshape(KV_LORA, MLA_HEADS * LANES)
    zv = jnp.zeros((KV_LORA, V_HEAD), F32)
    wv = jnp.concatenate([jnp.concatenate([w_ukv[:, h, QK_NOPE:], zv] if h % 2 == 0 else [zv, w_ukv[:, h, QK_NOPE:]],
                                          axis=1) for h in range(MLA_HEADS)], axis=1)
    one_even = jnp.zeros((LANES,), F32).at[V_HEAD].set(1.0)
    one_odd = jnp.zeros((LANES,), F32).at[0].set(1.0)
    vone = jnp.concatenate([one_even if h % 2 == 0 else one_odd for h in range(MLA_HEADS)])[None, :]
    w_out = ab_w_out[i].astype(BF16)
    return dict(
        nrm=ab_norm[i][None, :], w_big=w_big, qn=mla_q_norm[i][None, :], wqa=wqa.astype(BF16),
        wqb=wqb.astype(BF16), kvn=mla_kv_norm[i][None, :], wka=wka.astype(BF16), wv=wv.astype(BF16),
        gb=jnp.broadcast_to(mlstm_gate_bias[i][gate_order][:, None], (2 * N_STREAMS, LANES)), vone=vone,
        conv_w=mlstm_conv_w[i], conv_b=mlstm_conv_b[i][None, :], on=mlstm_out_norm[i][None, :],
        wa=w_out[:MLA_HEADS * V_HEAD], wm=w_out[MLA_HEADS * V_HEAD:])


def _pick(pref, seq):
    return min(pref, seq)


def _forward(x3, even, odd, ffn, final_norm):
    batch, seq, _ = x3.shape
    x = x3.reshape(batch * seq, D_MODEL)
    tm = _pick(512, seq)
    cos_t, sin_t = _rope_tables(seq)
    e = even
    q, k, v, mq, mk, mv, mo, g = _ab_in(x, cos_t, sin_t, e, seq=seq, tm=tm)
    ya = _attn(q, k, v, batch=batch, seq=seq, tq=_pick(256, seq))
    c_rows, dec, cols = _gates(g, batch=batch, seq=seq)
    hf = _mlstm(mq, mk, mv, c_rows, dec, cols, batch=batch, seq=seq, direction=0)
    hb = _mlstm(mq, mk, mv, c_rows, dec, cols, batch=batch, seq=seq, direction=1)
    x = _ab_out(x, ya, hf, hb, mo, e["on"], e["wa"], e["wm"], tm=tm)
    f = ffn[0]
    x = _ffn(x, f["nrm"], f["wup"], f["cw"], f["cb"], f["wdn"], final_norm, seq=seq, tm=tm, final=False)
    o = odd
    x = _mix_c(x, o["nrm"], o["win"], o["vn"], o["ws"], o["bs"], o["wout"], tm=tm)
    f = ffn[1]
    x = _ffn(x, f["nrm"], f["wup"], f["cw"], f["cb"], f["wdn"], final_norm, seq=seq, tm=tm, final=True)
    return x.reshape(batch, seq, D_MODEL)


def kernel(x_prompt, x_sample, ab_norm, ab_w_in, mla_q_norm, mla_w_uq, mla_kv_norm, mla_w_ukv, mlstm_conv_w,
           mlstm_conv_b, mlstm_gate_bias, mlstm_out_norm, ab_w_out, c_norm, c_w_in, c_v_norm, c_w_spatial,
           c_b_spatial, c_w_out, ffn_norm, ffn_w_up, ffn_conv_w, ffn_conv_b, ffn_w_down, final_norm):
    even = _prep_even(0, ab_norm, ab_w_in, mla_q_norm, mla_w_uq, mla_kv_norm, mla_w_ukv, mlstm_conv_w, mlstm_conv_b,
                      mlstm_gate_bias, mlstm_out_norm, ab_w_out)
    odd = dict(nrm=c_norm[0][None, :], win=c_w_in[0].astype(BF16), vn=c_v_norm[0][None, :],
               ws=c_w_spatial[0].astype(BF16), bs=c_b_spatial[0].T, wout=c_w_out[0].astype(BF16))
    ffn = [dict(nrm=ffn_norm[l][None, :], wup=ffn_w_up[l].astype(BF16), cw=ffn_conv_w[l], cb=ffn_conv_b[l][None, :],
                wdn=ffn_w_down[l].astype(BF16)) for l in range(2)]
    fin = final_norm[None, :]
    return (_forward(x_prompt, even, odd, ffn, fin), _forward(x_sample, even, odd, ffn, fin))
```

```python
import functools

import jax
import jax.numpy as jnp
from jax import lax
from jax.experimental import pallas as pl
from jax.experimental.pallas import tpu as pltpu

F32 = jnp.float32
BF16 = jnp.bfloat16

EPS = 1e-6
D_MODEL = 1024
MLA_HEADS = 8
Q_LORA = 256
KV_LORA = 128
QK_NOPE = 64
QK_ROPE = 32
V_HEAD = 64
ROPE_THETA = 10000.0
M_HEADS = 4
M_HEAD_DIM = 128
M_WIDTH = M_HEADS * M_HEAD_DIM
N_STREAMS = 2 * M_HEADS
CHUNK = 128
BLOCK_CHUNKS = 4
G_GROUPS = 8
D_FF = 2816
LANES = 128
SUBLANES = 8
HALO = SUBLANES
MXU_COLS = 256
FF_COLS = MXU_COLS
KEY_TILE = MXU_COLS
SCORE_ELEMS = 2 * 1024 * 1024
LOG2_E = 1.4426950408889634
VMEM_LIMIT = 56 * 1024 * 1024

_C_Q, _C_KV, _C_KRA, _C_KRB, _C_G, _C_MQK, _C_MV, _C_MO, _C_END = (
    0, 256, 384, 512, 640, 768, 1792, 2304, 2816)


def _rms(x, g):
    return x * lax.rsqrt(jnp.mean(x * x, axis=-1, keepdims=True) + EPS) * g


def _const_spec(shape):
    nd = len(shape)
    return pl.BlockSpec(shape, lambda *_: (0,) * nd, pipeline_mode=pl.Buffered(1))


def _params(*sem):
    return pltpu.CompilerParams(dimension_semantics=sem, vmem_limit_bytes=VMEM_LIMIT)


def _dot(a, b):
    return jnp.dot(a, b, preferred_element_type=F32)


def _dot_nt(a, b):
    return lax.dot_general(a, b, (((1,), (1,)), ((), ())), preferred_element_type=F32)


def _dot_tn(a, b):
    return lax.dot_general(a, b, (((0,), (0,)), ((), ())), preferred_element_type=F32)


def _halo_specs(tile_rows, n_rows, width):
    tb = tile_rows // HALO
    last = n_rows // HALO - 1
    return (pl.BlockSpec((HALO, width), lambda i: (jnp.maximum(i * tb - 1, 0), 0)),
            pl.BlockSpec((HALO, width), lambda i: (jnp.minimum((i + 1) * tb, last), 0)))


def _conv3(u, rows, w_ref, b_ref, cols):
    return (u[HALO - 1:HALO - 1 + rows, :] * w_ref[0:1, cols] + u[HALO:HALO + rows, :] * w_ref[1:2, cols]
            + u[HALO + 1:HALO + 1 + rows, :] * w_ref[2:3, cols] + b_ref[:, cols])


def _ab_in_kernel(x_ref, xp_ref, xn_ref, cos_ref, sin_ref, nrm_ref, w_ref, qn_ref, wqa_ref, wqb_ref, kvn_ref,
                  wka_ref, wv_ref, gb_ref, vone_ref, cw_ref, cb_ref,
                  q_ref, k_ref, v_ref, mq_ref, mk_ref, mv_ref, mo_ref, g_ref, us, *, tm, tps):
    jj = pl.program_id(0) % tps
    xa = jnp.concatenate([
        jnp.where(jj == 0, 0.0, xp_ref[...]),
        x_ref[...],
        jnp.where(jj == tps - 1, 0.0, xn_ref[...]),
    ], axis=0)
    xe = _rms(xa, nrm_ref[...]).astype(BF16)
    xn = xe[HALO:HALO + tm]

    def proj(a, b):
        return _dot(xn, w_ref[:, a:b])

    nq_blocks = M_WIDTH // MXU_COLS
    for cb in range(2 * nq_blocks):
        cols = slice(cb * MXU_COLS, (cb + 1) * MXU_COLS)
        us[cb % 2] = _dot(xe, w_ref[:, _C_MQK + cb * MXU_COLS:_C_MQK + (cb + 1) * MXU_COLS])
        y = _conv3(us.at[cb % 2], tm, cw_ref, cb_ref, cols)
        y = y * jax.nn.sigmoid(y)
        if cb < nq_blocks:
            mq_ref[:, cols] = y.astype(BF16)
        else:
            mk_ref[:, (cb - nq_blocks) * MXU_COLS:(cb - nq_blocks + 1) * MXU_COLS] = (
                y * (M_HEAD_DIM ** -0.5)).astype(BF16)
    mv_ref[...] = proj(_C_MV, _C_MO).astype(BF16)
    mo_ref[...] = proj(_C_MO, _C_END)
    ckv_kra = proj(_C_KV, _C_KRB)
    krb_g = proj(_C_KRB, _C_MQK)
    gt = krb_g[:, LANES:2 * LANES].T
    for c in range(tm // CHUNK):
        g_ref[c] = gt[0:2 * N_STREAMS, c * CHUNK:(c + 1) * CHUNK] + gb_ref[:, 0:1]

    cos = cos_ref[...]
    sin = sin_ref[...]
    scale = (QK_NOPE + QK_ROPE) ** -0.5 * LOG2_E
    cqn = _rms(proj(_C_Q, _C_KV), qn_ref[...]).astype(BF16)
    qa = _dot(cqn, wqa_ref[...])
    qb = _dot(cqn, wqb_ref[...])
    cos_q = cos * scale
    sin_q = sin * scale
    for h in range(MLA_HEADS):
        blk = slice(h * LANES, (h + 1) * LANES)
        q_ref[:, blk] = (qa[:, blk] * cos_q + qb[:, blk] * sin_q).astype(BF16)

    ckvn = _rms(ckv_kra[:, 0:LANES], kvn_ref[...]).astype(BF16)
    ka = _dot(ckvn, wka_ref[...])
    kr = ckv_kra[:, LANES:2 * LANES] * cos + krb_g[:, 0:LANES] * sin
    for h in range(MLA_HEADS):
        blk = slice(h * LANES, (h + 1) * LANES)
        k_ref[:, blk] = (ka[:, blk] + kr).astype(BF16)
    v_ref[...] = (_dot(ckvn, wv_ref[...]) + vone_ref[...]).astype(BF16)


def _ab_in(x, cos_t, sin_t, e, *, seq, tm):
    n = x.shape[0]
    nt = n // tm
    tps = seq // tm
    row = lambda i: (i, 0)
    tab = lambda i: (i % tps, 0)
    consts = [e[k] for k in ("nrm", "w_big", "qn", "wqa", "wqb", "kvn", "wka", "wv", "gb", "vone", "conv_w",
                             "conv_b")]
    out_shape = (
        jax.ShapeDtypeStruct((n, MLA_HEADS * LANES), BF16),
        jax.ShapeDtypeStruct((n, MLA_HEADS * LANES), BF16),
        jax.ShapeDtypeStruct((n, MLA_HEADS * LANES), BF16),
        jax.ShapeDtypeStruct((n, M_WIDTH), BF16),
        jax.ShapeDtypeStruct((n, M_WIDTH), BF16),
        jax.ShapeDtypeStruct((n, M_WIDTH), BF16),
        jax.ShapeDtypeStruct((n, M_WIDTH), F32),
        jax.ShapeDtypeStruct((n // CHUNK, 2 * N_STREAMS, CHUNK), F32),
    )
    wide = pl.BlockSpec((tm, MLA_HEADS * LANES), row)
    narrow = pl.BlockSpec((tm, M_WIDTH), row)
    return pl.pallas_call(
        functools.partial(_ab_in_kernel, tm=tm, tps=tps),
        grid=(nt,),
        in_specs=[pl.BlockSpec((tm, D_MODEL), row), *_halo_specs(tm, n, D_MODEL),
                  pl.BlockSpec((tm, LANES), tab), pl.BlockSpec((tm, LANES), tab)]
                 + [_const_spec(c.shape) for c in consts],
        out_specs=(wide, wide, wide, narrow, narrow, narrow, narrow,
                   pl.BlockSpec((tm // CHUNK, 2 * N_STREAMS, CHUNK), lambda i: (i, 0, 0))),
        out_shape=out_shape,
        scratch_shapes=[pltpu.VMEM((2, tm + 2 * HALO, MXU_COLS), F32)],
        compiler_params=_params("parallel"),
        name="ab_in",
    )(x, x, x, cos_t, sin_t, *consts)


def _attn_kernel(q_ref, k_ref, v_ref, o_ref, s_scr, *, tq):
    nt = k_ref.shape[0] // KEY_TILE
    mbs = []
    for h in range(2):
        hc = slice(h * LANES, (h + 1) * LANES)
        q = q_ref[:, hc]
        mx = jnp.full((tq, LANES), -jnp.inf, F32)
        for t in range(nt):
            keys = slice(t * KEY_TILE, (t + 1) * KEY_TILE)
            s = _dot_nt(q, k_ref[keys, hc])
            s_scr[h, :, keys] = s
            for j in range(KEY_TILE // LANES):
                mx = jnp.maximum(mx, s[:, j * LANES:(j + 1) * LANES])
        mbs.append(jnp.broadcast_to(jnp.max(mx, axis=-1, keepdims=True), (tq, LANES)))
    outs = []
    for h in range(2):
        hc = slice(h * LANES, (h + 1) * LANES)
        mb = jnp.concatenate([mbs[h]] * (KEY_TILE // LANES), axis=1)
        acc = jnp.zeros((tq, LANES), F32)
        for t in range(nt):
            keys = slice(t * KEY_TILE, (t + 1) * KEY_TILE)
            p = jnp.exp2(s_scr[h, :, keys] - mb).astype(BF16)
            acc = acc + _dot(p, v_ref[keys, hc])
        ones_lane = V_HEAD if h == 0 else 0
        outs.append(acc / acc[:, ones_lane:ones_lane + 1])
    lane = lax.broadcasted_iota(jnp.int32, (tq, LANES), 1)
    o_ref[...] = jnp.where(lane < V_HEAD, outs[0], outs[1]).astype(BF16)


def _attn(q, k, v, *, batch, seq, tq):
    n = q.shape[0]
    nq = seq // tq
    pairs = MLA_HEADS // 2
    return pl.pallas_call(
        functools.partial(_attn_kernel, tq=tq),
        grid=(batch, pairs, nq),
        in_specs=[
            pl.BlockSpec((tq, 2 * LANES), lambda b, p, i: (b * nq + i, p)),
            pl.BlockSpec((seq, 2 * LANES), lambda b, p, i: (b, p)),
            pl.BlockSpec((seq, 2 * LANES), lambda b, p, i: (b, p)),
        ],
        out_specs=pl.BlockSpec((tq, LANES), lambda b, p, i: (b * nq + i, p)),
        out_shape=jax.ShapeDtypeStruct((n, MLA_HEADS * V_HEAD), BF16),
        scratch_shapes=[pltpu.VMEM((2, tq, seq), F32)],
        compiler_params=_params("parallel", "parallel", "arbitrary"),
        name="attn",
    )(q, k, v)


def _log_sigmoid(x):
    return jnp.minimum(x, 0.0) - jnp.log1p(jnp.exp(-jnp.abs(x)))


def _gates_kernel(g_ref, c_ref, dec_ref, cols_ref, tot_s, mloc_s, mpf_s, mpb_s, *, nc):
    rows = nc * N_STREAMS
    shape = (rows, CHUNK)
    three = (nc, N_STREAMS, CHUNK)
    li = g_ref[:, 0:N_STREAMS, :].reshape(shape)
    lf = _log_sigmoid(g_ref[:, N_STREAMS:2 * N_STREAMS, :].reshape(shape))
    fwd = (lax.broadcasted_iota(jnp.int32, shape, 0) & M_HEADS) == 0
    lane = lax.broadcasted_iota(jnp.int32, shape, 1)
    shifts = (1, 2, 4, 8, 16, 32, 64)

    pre = lf
    for sh in shifts:
        pre = pre + jnp.where(lane >= sh, pltpu.roll(pre, sh, axis=1), 0.0)
    tot = jnp.broadcast_to(pre[:, CHUNK - 1:CHUNK], shape)
    b = jnp.where(fwd, pre, tot - pre + lf)
    c = li - b
    run_f = c
    run_b = c
    for sh in shifts:
        run_f = jnp.maximum(run_f, jnp.where(lane >= sh, pltpu.roll(run_f, sh, axis=1), -jnp.inf))
        run_b = jnp.maximum(run_b, jnp.where(lane < CHUNK - sh, pltpu.roll(run_b, CHUNK - sh, axis=1), -jnp.inf))
    run = jnp.where(fwd, run_f, run_b)
    m_loc = tot + jnp.broadcast_to(jnp.max(c, axis=1, keepdims=True), shape)

    tot_s[...] = tot.reshape(three)
    mloc_s[...] = m_loc.reshape(three)
    fwd8 = lax.broadcasted_iota(jnp.int32, (N_STREAMS, CHUNK), 0) < M_HEADS

    def step(i, m):
        jb = nc - 1 - i
        mpf_s[i] = m
        mpb_s[jb] = m
        return jnp.maximum(jnp.where(fwd8, tot_s[i], tot_s[jb]) + m, jnp.where(fwd8, mloc_s[i], mloc_s[jb]))

    lax.fori_loop(0, nc, step, jnp.zeros((N_STREAMS, CHUNK), F32))
    m_prev = jnp.where(fwd, mpf_s[...].reshape(shape), mpb_s[...].reshape(shape))

    m_run = jnp.maximum(m_prev, run)
    inter = jnp.exp(m_prev - m_run)
    floor = jnp.exp(-(b + m_run))
    w = jnp.exp(tot + c - m_loc)
    m_new = jnp.maximum(tot + m_prev, m_loc)
    c_ref[...] = c.reshape(three)
    dec_ref[:, 0:N_STREAMS, :] = jnp.exp(tot + m_prev - m_new).reshape(three)
    dec_ref[:, N_STREAMS:2 * N_STREAMS, :] = jnp.exp(m_loc - m_new).reshape(three)
    quantities = [a.reshape(three) for a in (m_run, inter, floor, w)]
    for blk in range(nc // BLOCK_CHUNKS):
        tile = jnp.concatenate([a[blk * BLOCK_CHUNKS + cq] for cq in range(BLOCK_CHUNKS) for a in quantities], axis=0)
        cols_ref[blk] = tile.T


def _gates(g, *, batch, seq):
    nc = seq // CHUNK
    nchunks = g.shape[0]
    three = (nc, N_STREAMS, CHUNK)
    return pl.pallas_call(
        functools.partial(_gates_kernel, nc=nc),
        grid=(batch,),
        in_specs=[pl.BlockSpec((nc, 2 * N_STREAMS, CHUNK), lambda b: (b, 0, 0))],
        out_specs=(
            pl.BlockSpec(three, lambda b: (b, 0, 0)),
            pl.BlockSpec((nc, 2 * N_STREAMS, CHUNK), lambda b: (b, 0, 0)),
            pl.BlockSpec((nc // BLOCK_CHUNKS, CHUNK, LANES), lambda b: (b, 0, 0)),
        ),
        out_shape=(
            jax.ShapeDtypeStruct((nchunks, N_STREAMS, CHUNK), F32),
            jax.ShapeDtypeStruct((nchunks, 2 * N_STREAMS, CHUNK), F32),
            jax.ShapeDtypeStruct((nchunks // BLOCK_CHUNKS, CHUNK, LANES), F32),
        ),
        scratch_shapes=[pltpu.VMEM(three, F32)] * 4,
        compiler_params=_params("parallel"),
        name="gates",
    )(g)


def _mlstm_kernel(q_ref, k_ref, v_ref, c_ref, dec_ref, cols_ref, h_ref, st, *, direction):
    @pl.when(pl.program_id(1) == 0)
    def _():
        st[...] = jnp.zeros_like(st)

    row = lax.broadcasted_iota(jnp.int32, (CHUNK, CHUNK), 0)
    col = lax.broadcasted_iota(jnp.int32, (CHUNK, CHUNK), 1)
    mask = (col <= row) if direction == 0 else (col >= row)
    ones = (col == 0).astype(BF16)
    cols_t = cols_ref[0]
    order = range(BLOCK_CHUNKS) if direction == 0 else range(BLOCK_CHUNKS - 1, -1, -1)
    for hh in range(M_HEADS):
        hc = slice(hh * LANES, (hh + 1) * LANES)
        s = direction * M_HEADS + hh
        state = st[hh]
        for cq in order:
            rows = slice(cq * CHUNK, (cq + 1) * CHUNK)
            q = q_ref[rows, hc]
            k = k_ref[rows, hc]
            v1 = jnp.concatenate([v_ref[rows, hc], ones], axis=1)
            base = cq * 4 * N_STREAMS + s
            m_run, inter, floor, w = (cols_t[:, base + qi * N_STREAMS:base + qi * N_STREAMS + 1] for qi in range(4))
            d = jnp.where(mask, jnp.exp(c_ref[cq, s:s + 1, :] - m_run), 0.0)
            qk = (_dot_nt(q, k) * d).astype(BF16)
            nd = inter * _dot(q, state.astype(BF16)) + _dot(qk, v1)
            den = nd[:, LANES:LANES + 1]
            h_ref[rows, hc] = nd[:, 0:LANES] / jnp.maximum(jnp.abs(den), floor)
            kw = (k.astype(F32) * w).astype(BF16)
            state = (dec_ref[cq, s:s + 1, 0:1] * state
                     + dec_ref[cq, N_STREAMS + s:N_STREAMS + s + 1, 0:1] * _dot_tn(kw, v1))
        st[hh] = state


def _mlstm(mq, mk, mv, c_rows, dec, cols, *, batch, seq, direction):
    n = mq.shape[0]
    ts = BLOCK_CHUNKS * CHUNK
    nblk = seq // ts

    def blk(b, i):
        return b * nblk + (i if direction == 0 else nblk - 1 - i)

    tok = pl.BlockSpec((ts, M_WIDTH), lambda b, i: (blk(b, i), 0))
    return pl.pallas_call(
        functools.partial(_mlstm_kernel, direction=direction),
        grid=(batch, nblk),
        in_specs=[
            tok, tok, tok,
            pl.BlockSpec((BLOCK_CHUNKS, N_STREAMS, CHUNK), lambda b, i: (blk(b, i), 0, 0)),
            pl.BlockSpec((BLOCK_CHUNKS, 2 * N_STREAMS, CHUNK), lambda b, i: (blk(b, i), 0, 0)),
            pl.BlockSpec((1, CHUNK, LANES), lambda b, i: (blk(b, i), 0, 0)),
        ],
        out_specs=tok,
        out_shape=jax.ShapeDtypeStruct((n, M_WIDTH), F32),
        scratch_shapes=[pltpu.VMEM((M_HEADS, M_HEAD_DIM, 2 * LANES), F32)],
        compiler_params=_params("parallel", "arbitrary"),
        name="mlstm_fwd" if direction == 0 else "mlstm_bwd",
    )(mq, mk, mv, c_rows, dec, cols)


def _ab_out_kernel(x_ref, ya_ref, hf_ref, hb_ref, mo_ref, on_ref, wa_ref, wm_ref, o_ref):
    parts = []
    for hh in range(M_HEADS):
        hc = slice(hh * LANES, (hh + 1) * LANES)
        t = hf_ref[:, hc] + hb_ref[:, hc]
        t = t * lax.rsqrt(jnp.mean(t * t, axis=-1, keepdims=True) + EPS) * on_ref[:, hc]
        parts.append((jax.nn.sigmoid(mo_ref[:, hc]) * t).astype(BF16))
    ym = jnp.concatenate(parts, axis=1)
    o_ref[...] = x_ref[...] + _dot(ya_ref[...], wa_ref[...]) + _dot(ym, wm_ref[...])


def _ab_out(x, ya, hf, hb, mo, on, wa, wm, *, tm):
    n = x.shape[0]
    row = lambda i: (i, 0)
    narrow = pl.BlockSpec((tm, M_WIDTH), row)
    return pl.pallas_call(
        _ab_out_kernel,
        grid=(n // tm,),
        in_specs=[
            pl.BlockSpec((tm, D_MODEL), row), narrow, narrow, narrow, narrow,
            _const_spec(on.shape), _const_spec(wa.shape), _const_spec(wm.shape),
        ],
        out_specs=pl.BlockSpec((tm, D_MODEL), row),
        out_shape=jax.ShapeDtypeStruct((n, D_MODEL), F32),
        compiler_params=_params("parallel"),
        name="ab_out",
    )(x, ya, hf, hb, mo, on, wa, wm)


def _conv3_by_residue(u, rows, w_ref, b_ref, cols):
    t = rows // SUBLANES
    w0, w1, w2, b = w_ref[0:1, cols], w_ref[1:2, cols], w_ref[2:3, cols], b_ref[:, cols]
    ri = lax.broadcasted_iota(jnp.int32, (t, u.shape[1]), 0)
    outs = []
    for j in range(SUBLANES):
        ctr = u[j * t:(j + 1) * t, :]
        if j > 0:
            prv = u[(j - 1) * t:j * t, :]
        else:
            prv = jnp.where(ri == 0, u[rows + HALO - 1:rows + HALO, :], u[7 * t - 1:8 * t - 1, :])
        if j < SUBLANES - 1:
            nxt = u[(j + 1) * t:(j + 2) * t, :]
        else:
            nxt = jnp.where(ri == t - 1, u[rows + HALO:rows + HALO + 1, :], u[1:t + 1, :])
        outs.append(prv * w0 + ctr * w1 + nxt * w2 + b)
    return jnp.concatenate(outs, axis=0)


def _ffn_kernel(x_ref, xp_ref, xn_ref, nrm_ref, wup_ref, cw_ref, cb_ref, wdn_ref, fin_ref, o_ref, us, act,
                *, tm, tps, final):
    jj = pl.program_id(0) % tps
    t = tm // SUBLANES
    x = jnp.concatenate([x_ref[:, j * D_MODEL:(j + 1) * D_MODEL] for j in range(SUBLANES)], axis=0)
    xa = jnp.concatenate([
        x,
        jnp.where(jj == 0, 0.0, xp_ref[...]),
        jnp.where(jj == tps - 1, 0.0, xn_ref[...]),
    ], axis=0)
    xn = _rms(xa, nrm_ref[...]).astype(BF16)

    for c in range(D_FF // FF_COLS):
        gcols = slice(c * FF_COLS, (c + 1) * FF_COLS)
        vcols = slice(D_FF + c * FF_COLS, D_FF + (c + 1) * FF_COLS)
        us[0] = _dot(xn, wup_ref[:, gcols])
        us[1] = _dot(xn, wup_ref[:, vcols])
        gate = _conv3_by_residue(us.at[0], tm, cw_ref, cb_ref, gcols)
        val = _conv3_by_residue(us.at[1], tm, cw_ref, cb_ref, vcols)
        act[:, gcols] = (gate * jax.nn.sigmoid(gate) * val).astype(BF16)
    y = x + _dot(act[...], wdn_ref[...])
    if final:
        y = _rms(y, fin_ref[...])
    for j in range(SUBLANES):
        o_ref[:, j * D_MODEL:(j + 1) * D_MODEL] = y[j * t:(j + 1) * t]


def _ffn(x, nrm, wup, cw, cb, wdn, fin, *, seq, tm, final):
    n = x.shape[0]
    row = lambda i: (i, 0)
    grouped = pl.BlockSpec((tm // SUBLANES, SUBLANES * D_MODEL), row)
    y = pl.pallas_call(
        functools.partial(_ffn_kernel, tm=tm, tps=seq // tm, final=final),
        grid=(n // tm,),
        in_specs=[
            grouped, *_halo_specs(tm, n, D_MODEL),
            _const_spec(nrm.shape), _const_spec(wup.shape), _const_spec(cw.shape), _const_spec(cb.shape),
            _const_spec(wdn.shape), _const_spec(fin.shape),
        ],
        out_specs=grouped,
        out_shape=jax.ShapeDtypeStruct((n // SUBLANES, SUBLANES * D_MODEL), F32),
        scratch_shapes=[
            pltpu.VMEM((2, tm + 2 * HALO, FF_COLS), F32),
            pltpu.VMEM((tm, D_FF), BF16),
        ],
        compiler_params=_params("parallel"),
        name="ffn_final" if final else "ffn",
    )(x.reshape(n // SUBLANES, SUBLANES * D_MODEL), x, x, nrm, wup, cw, cb, wdn, fin)
    return y.reshape(n, D_MODEL)


def _gelu(x):
    return 0.5 * x * (1.0 + lax.erf(x * (2.0 ** -0.5)))


def _mix_c_kernel(x_ref, nrm_ref, win_ref, vn_ref, ws_ref, bs_ref, wout_ref, o_ref, zs, *, tm):
    x = x_ref[...]
    xn = _rms(x, nrm_ref[...]).astype(BF16)
    u = _gelu(_dot(xn, win_ref[:, 0:D_MODEL]))
    v = _gelu(_dot(xn, win_ref[:, D_MODEL:2 * D_MODEL]))
    vn = _rms(v, vn_ref[...]).astype(BF16)
    nch = tm // CHUNK
    for g in range(G_GROUPS):
        cols = slice(g * LANES, (g + 1) * LANES)
        rhs = jnp.concatenate([vn[c * CHUNK:(c + 1) * CHUNK, cols] for c in range(nch)], axis=1)
        sv = _dot(ws_ref[g], rhs) + bs_ref[:, g:g + 1]
        for c in range(nch):
            zs[c * CHUNK:(c + 1) * CHUNK, cols] = (
                u[c * CHUNK:(c + 1) * CHUNK, cols] * sv[:, c * LANES:(c + 1) * LANES]).astype(BF16)
    o_ref[...] = x + _dot(zs[...], wout_ref[...])


def _mix_c(x, nrm, win, vn, ws, bs, wout, *, tm):
    n = x.shape[0]
    row = lambda i: (i, 0)
    return pl.pallas_call(
        functools.partial(_mix_c_kernel, tm=tm),
        grid=(n // tm,),
        in_specs=[
            pl.BlockSpec((tm, D_MODEL), row),
            _const_spec(nrm.shape), _const_spec(win.shape), _const_spec(vn.shape), _const_spec(ws.shape),
            _const_spec(bs.shape), _const_spec(wout.shape),
        ],
        out_specs=pl.BlockSpec((tm, D_MODEL), row),
        out_shape=jax.ShapeDtypeStruct((n, D_MODEL), F32),
        scratch_shapes=[pltpu.VMEM((tm, D_MODEL), BF16)],
        compiler_params=_params("parallel"),
        name="mix_c",
    )(x, nrm, win, vn, ws, bs, wout)


def _rope_tables(seq):
    pos = jnp.arange(seq, dtype=F32)
    inv = 1.0 / (ROPE_THETA ** (jnp.arange(0, QK_ROPE, 2, dtype=F32) / QK_ROPE))
    ang = pos[:, None] * inv[None, :]
    cos, sin = jnp.cos(ang), jnp.sin(ang)
    one = jnp.ones((seq, QK_NOPE), F32)
    zero_n = jnp.zeros((seq, QK_NOPE), F32)
    pad = jnp.zeros((seq, LANES - QK_NOPE - QK_ROPE), F32)
    return (jnp.concatenate([one, cos, cos, pad], axis=1), jnp.concatenate([zero_n, sin, sin, pad], axis=1))


def _head_block(nope, x1, x2):
    pad = jnp.zeros((nope.shape[0], LANES - QK_NOPE - QK_ROPE), nope.dtype)
    return jnp.concatenate([nope, x1, x2, pad], axis=1)


def _prep_even(i, ab_norm, ab_w_in, mla_q_norm, mla_w_uq, mla_kv_norm, mla_w_ukv, mlstm_conv_w, mlstm_conv_b,
               mlstm_gate_bias, mlstm_out_norm, ab_w_out):
    w_in = ab_w_in[i]
    half = QK_ROPE // 2
    zq = jnp.zeros((D_MODEL, QK_NOPE), F32)
    kr1 = w_in[:, 384:384 + half]
    kr2 = w_in[:, 384 + half:416]
    gate_order = jnp.array([0, 1, 2, 3, 8, 9, 10, 11, 4, 5, 6, 7, 12, 13, 14, 15])
    gates = jnp.pad(w_in[:, 2464:2480][:, gate_order], ((0, 0), (0, LANES - 2 * N_STREAMS)))
    w_big = jnp.concatenate([
        w_in[:, 0:384],
        _head_block(zq, kr1, kr2),
        _head_block(zq, -kr2, kr1),
        gates,
        w_in[:, 416:2464],
    ], axis=1).astype(BF16)
    w_uq = mla_w_uq[i].reshape(Q_LORA, MLA_HEADS, QK_NOPE + QK_ROPE)
    zn = jnp.zeros((Q_LORA, QK_NOPE), F32)
    wqa = jnp.concatenate([_head_block(w_uq[:, h, :QK_NOPE], w_uq[:, h, QK_NOPE:QK_NOPE + half],
                                       w_uq[:, h, QK_NOPE + half:]) for h in range(MLA_HEADS)], axis=1)
    wqb = jnp.concatenate([_head_block(zn, -w_uq[:, h, QK_NOPE + half:], w_uq[:, h, QK_NOPE:QK_NOPE + half])
                           for h in range(MLA_HEADS)], axis=1)
    w_ukv = mla_w_ukv[i].reshape(KV_LORA, MLA_HEADS, QK_NOPE + V_HEAD)
    wka = jnp.pad(w_ukv[:, :, :QK_NOPE], ((0, 0), (0, 0), (0, LANES - QK_NOPE))).reshape(KV_LORA, MLA_HEADS * LANES)
    zv = jnp.zeros((KV_LORA, V_HEAD), F32)
    wv = jnp.concatenate([jnp.concatenate([w_ukv[:, h, QK_NOPE:], zv] if h % 2 == 0 else [zv, w_ukv[:, h, QK_NOPE:]],
                                          axis=1) for h in range(MLA_HEADS)], axis=1)
    one_even = jnp.zeros((LANES,), F32).at[V_HEAD].set(1.0)
    one_odd = jnp.zeros((LANES,), F32).at[0].set(1.0)
    vone = jnp.concatenate([one_even if h % 2 == 0 else one_odd for h in range(MLA_HEADS)])[None, :]
    w_out = ab_w_out[i].astype(BF16)
    return dict(
        nrm=ab_norm[i][None, :], w_big=w_big, qn=mla_q_norm[i][None, :], wqa=wqa.astype(BF16),
        wqb=wqb.astype(BF16), kvn=mla_kv_norm[i][None, :], wka=wka.astype(BF16), wv=wv.astype(BF16),
        gb=jnp.broadcast_to(mlstm_gate_bias[i][gate_order][:, None], (2 * N_STREAMS, LANES)), vone=vone,
        conv_w=mlstm_conv_w[i], conv_b=mlstm_conv_b[i][None, :], on=mlstm_out_norm[i][None, :],
        wa=w_out[:MLA_HEADS * V_HEAD], wm=w_out[MLA_HEADS * V_HEAD:])


def _pick(pref, seq):
    return min(pref, seq)


def _forward(x3, even, odd, ffn, final_norm):
    batch, seq, _ = x3.shape
    x = x3.reshape(batch * seq, D_MODEL)
    tm = _pick(512, seq)
    cos_t, sin_t = _rope_tables(seq)
    e = even
    q, k, v, mq, mk, mv, mo, g = _ab_in(x, cos_t, sin_t, e, seq=seq, tm=tm)
    ya = _attn(q, k, v, batch=batch, seq=seq, tq=_pick(SCORE_ELEMS // seq, seq))
    c_rows, dec, cols = _gates(g, batch=batch, seq=seq)
    hf = _mlstm(mq, mk, mv, c_rows, dec, cols, batch=batch, seq=seq, direction=0)
    hb = _mlstm(mq, mk, mv, c_rows, dec, cols, batch=batch, seq=seq, direction=1)
    x = _ab_out(x, ya, hf, hb, mo, e["on"], e["wa"], e["wm"], tm=tm)
    f = ffn[0]
    x = _ffn(x, f["nrm"], f["wup"], f["cw"], f["cb"], f["wdn"], final_norm, seq=seq, tm=tm, final=False)
    o = odd
    x = _mix_c(x, o["nrm"], o["win"], o["vn"], o["ws"], o["bs"], o["wout"], tm=tm)
    f = ffn[1]
    x = _ffn(x, f["nrm"], f["wup"], f["cw"], f["cb"], f["wdn"], final_norm, seq=seq, tm=tm, final=True)
    return x.reshape(batch, seq, D_MODEL)


def kernel(x_prompt, x_sample, ab_norm, ab_w_in, mla_q_norm, mla_w_uq, mla_kv_norm, mla_w_ukv, mlstm_conv_w,
           mlstm_conv_b, mlstm_gate_bias, mlstm_out_norm, ab_w_out, c_norm, c_w_in, c_v_norm, c_w_spatial,
           c_b_spatial, c_w_out, ffn_norm, ffn_w_up, ffn_conv_w, ffn_conv_b, ffn_w_down, final_norm):
    even = _prep_even(0, ab_norm, ab_w_in, mla_q_norm, mla_w_uq, mla_kv_norm, mla_w_ukv, mlstm_conv_w, mlstm_conv_b,
                      mlstm_gate_bias, mlstm_out_norm, ab_w_out)
    odd = dict(nrm=c_norm[0][None, :], win=c_w_in[0].astype(BF16), vn=c_v_norm[0][None, :],
               ws=c_w_spatial[0].astype(BF16), bs=c_b_spatial[0].T, wout=c_w_out[0].astype(BF16))
    ffn = [dict(nrm=ffn_norm[l][None, :], wup=ffn_w_up[l].astype(BF16), cw=ffn_conv_w[l], cb=ffn_conv_b[l][None, :],
                wdn=ffn_w_down[l].astype(BF16)) for l in range(2)]
    fin = final_norm[None, :]
    return (_forward(x_prompt, even, odd, ffn, fin), _forward(x_sample, even, odd, ffn, fin))
```

```python
import functools

import jax
import jax.numpy as jnp
from jax import lax
from jax.experimental import pallas as pl
from jax.experimental.pallas import tpu as pltpu

F32 = jnp.float32
BF16 = jnp.bfloat16

EPS = 1e-6
D_MODEL = 1024
MLA_HEADS = 8
Q_LORA = 256
KV_LORA = 128
QK_NOPE = 64
QK_ROPE = 32
V_HEAD = 64
ROPE_THETA = 10000.0
M_HEADS = 4
M_HEAD_DIM = 128
M_WIDTH = M_HEADS * M_HEAD_DIM
N_STREAMS = 2 * M_HEADS
CHUNK = 128
BLOCK_CHUNKS = 4
G_GROUPS = 8
D_FF = 2816
LANES = 128
SUBLANES = 8
HALO = SUBLANES
MXU_COLS = 256
ROW_PITCH = 2
FF_COLS = MXU_COLS
KEY_TILE = MXU_COLS
SCORE_ELEMS = 2 * 1024 * 1024
LOG2_E = 1.4426950408889634
VMEM_LIMIT = 56 * 1024 * 1024

_C_Q, _C_KV, _C_KRA, _C_KRB, _C_G, _C_MQK, _C_MV, _C_MO, _C_END = (
    0, 256, 384, 512, 640, 768, 1792, 2304, 2816)


def _rms(x, g):
    return x * lax.rsqrt(jnp.mean(x * x, axis=-1, keepdims=True) + EPS) * g


def _const_spec(shape):
    nd = len(shape)
    return pl.BlockSpec(shape, lambda *_: (0,) * nd, pipeline_mode=pl.Buffered(1))


def _params(*sem):
    return pltpu.CompilerParams(dimension_semantics=sem, vmem_limit_bytes=VMEM_LIMIT)


def _dot(a, b):
    return jnp.dot(a, b, preferred_element_type=F32)


def _dot_nt(a, b):
    return lax.dot_general(a, b, (((1,), (1,)), ((), ())), preferred_element_type=F32)


def _dot_tn(a, b):
    return lax.dot_general(a, b, (((0,), (0,)), ((), ())), preferred_element_type=F32)


def _halo_specs(tile_rows, n_rows, width):
    tb = tile_rows // HALO
    last = n_rows // HALO - 1
    return (pl.BlockSpec((HALO, width), lambda i: (jnp.maximum(i * tb - 1, 0), 0)),
            pl.BlockSpec((HALO, width), lambda i: (jnp.minimum((i + 1) * tb, last), 0)))


def _conv3(y, u, rows, w_ref, b_ref, c0):
    tall = rows + 2 * HALO
    outs = []
    for s in range(y.shape[1] // LANES):
        lanes = slice(s * LANES, (s + 1) * LANES)
        cols = slice(c0 + s * LANES, c0 + (s + 1) * LANES)
        u[s, pl.ds(0, tall, stride=ROW_PITCH), :] = y[:, lanes]
        prv = u[s, pl.ds(ROW_PITCH * (HALO - 1), rows, stride=ROW_PITCH), :]
        nxt = u[s, pl.ds(ROW_PITCH * (HALO + 1), rows, stride=ROW_PITCH), :]
        outs.append(prv * w_ref[0:1, cols] + y[HALO:HALO + rows, lanes] * w_ref[1:2, cols]
                    + nxt * w_ref[2:3, cols] + b_ref[:, cols])
    return jnp.concatenate(outs, axis=1)


def _ab_in_kernel(x_ref, xp_ref, xn_ref, cos_ref, sin_ref, nrm_ref, w_ref, qn_ref, wqa_ref, wqb_ref, kvn_ref,
                  wka_ref, wv_ref, gb_ref, vone_ref, cw_ref, cb_ref,
                  q_ref, k_ref, v_ref, mq_ref, mk_ref, mv_ref, mo_ref, g_ref, us, *, tm, tps):
    jj = pl.program_id(0) % tps
    xa = jnp.concatenate([
        jnp.where(jj == 0, 0.0, xp_ref[...]),
        x_ref[...],
        jnp.where(jj == tps - 1, 0.0, xn_ref[...]),
    ], axis=0)
    xe = _rms(xa, nrm_ref[...]).astype(BF16)
    xn = xe[HALO:HALO + tm]

    def proj(a, b):
        return _dot(xn, w_ref[:, a:b])

    nq_blocks = M_WIDTH // MXU_COLS
    for cb in range(2 * nq_blocks):
        cols = slice(cb * MXU_COLS, (cb + 1) * MXU_COLS)
        y = _conv3(_dot(xe, w_ref[:, _C_MQK + cb * MXU_COLS:_C_MQK + (cb + 1) * MXU_COLS]), us.at[cb % 2], tm,
                   cw_ref, cb_ref, cb * MXU_COLS)
        y = y * jax.nn.sigmoid(y)
        if cb < nq_blocks:
            mq_ref[:, cols] = y.astype(BF16)
        else:
            mk_ref[:, (cb - nq_blocks) * MXU_COLS:(cb - nq_blocks + 1) * MXU_COLS] = (
                y * (M_HEAD_DIM ** -0.5)).astype(BF16)
    mv_ref[...] = proj(_C_MV, _C_MO).astype(BF16)
    mo_ref[...] = proj(_C_MO, _C_END)
    ckv_kra = proj(_C_KV, _C_KRB)
    krb_g = proj(_C_KRB, _C_MQK)
    gt = krb_g[:, LANES:2 * LANES].T
    for c in range(tm // CHUNK):
        g_ref[c] = gt[0:2 * N_STREAMS, c * CHUNK:(c + 1) * CHUNK] + gb_ref[:, 0:1]

    cos = cos_ref[...]
    sin = sin_ref[...]
    scale = (QK_NOPE + QK_ROPE) ** -0.5 * LOG2_E
    cqn = _rms(proj(_C_Q, _C_KV), qn_ref[...]).astype(BF16)
    qa = _dot(cqn, wqa_ref[...])
    qb = _dot(cqn, wqb_ref[...])
    cos_q = cos * scale
    sin_q = sin * scale
    for h in range(MLA_HEADS):
        blk = slice(h * LANES, (h + 1) * LANES)
        q_ref[:, blk] = (qa[:, blk] * cos_q + qb[:, blk] * sin_q).astype(BF16)

    ckvn = _rms(ckv_kra[:, 0:LANES], kvn_ref[...]).astype(BF16)
    ka = _dot(ckvn, wka_ref[...])
    kr = ckv_kra[:, LANES:2 * LANES] * cos + krb_g[:, 0:LANES] * sin
    for h in range(MLA_HEADS):
        blk = slice(h * LANES, (h + 1) * LANES)
        k_ref[:, blk] = (ka[:, blk] + kr).astype(BF16)
    v_ref[...] = (_dot(ckvn, wv_ref[...]) + vone_ref[...]).astype(BF16)


def _ab_in(x, cos_t, sin_t, e, *, seq, tm):
    n = x.shape[0]
    nt = n // tm
    tps = seq // tm
    row = lambda i: (i, 0)
    tab = lambda i: (i % tps, 0)
    consts = [e[k] for k in ("nrm", "w_big", "qn", "wqa", "wqb", "kvn", "wka", "wv", "gb", "vone", "conv_w",
                             "conv_b")]
    out_shape = (
        jax.ShapeDtypeStruct((n, MLA_HEADS * LANES), BF16),
        jax.ShapeDtypeStruct((n, MLA_HEADS * LANES), BF16),
        jax.ShapeDtypeStruct((n, MLA_HEADS * LANES), BF16),
        jax.ShapeDtypeStruct((n, M_WIDTH), BF16),
        jax.ShapeDtypeStruct((n, M_WIDTH), BF16),
        jax.ShapeDtypeStruct((n, M_WIDTH), BF16),
        jax.ShapeDtypeStruct((n, M_WIDTH), F32),
        jax.ShapeDtypeStruct((n // CHUNK, 2 * N_STREAMS, CHUNK), F32),
    )
    wide = pl.BlockSpec((tm, MLA_HEADS * LANES), row)
    narrow = pl.BlockSpec((tm, M_WIDTH), row)
    return pl.pallas_call(
        functools.partial(_ab_in_kernel, tm=tm, tps=tps),
        grid=(nt,),
        in_specs=[pl.BlockSpec((tm, D_MODEL), row), *_halo_specs(tm, n, D_MODEL),
                  pl.BlockSpec((tm, LANES), tab), pl.BlockSpec((tm, LANES), tab)]
                 + [_const_spec(c.shape) for c in consts],
        out_specs=(wide, wide, wide, narrow, narrow, narrow, narrow,
                   pl.BlockSpec((tm // CHUNK, 2 * N_STREAMS, CHUNK), lambda i: (i, 0, 0))),
        out_shape=out_shape,
        scratch_shapes=[pltpu.VMEM((2, MXU_COLS // LANES, ROW_PITCH * (tm + 2 * HALO), LANES), F32)],
        compiler_params=_params("parallel"),
        name="ab_in",
    )(x, x, x, cos_t, sin_t, *consts)


def _attn_kernel(q_ref, k_ref, v_ref, o_ref, s_scr, *, tq):
    nt = k_ref.shape[0] // KEY_TILE
    mbs = []
    for h in range(2):
        hc = slice(h * LANES, (h + 1) * LANES)
        q = q_ref[:, hc]
        mx = jnp.full((tq, LANES), -jnp.inf, F32)
        for t in range(nt):
            keys = slice(t * KEY_TILE, (t + 1) * KEY_TILE)
            s = _dot_nt(q, k_ref[keys, hc])
            s_scr[h, :, keys] = s
            for j in range(KEY_TILE // LANES):
                mx = jnp.maximum(mx, s[:, j * LANES:(j + 1) * LANES])
        mbs.append(jnp.broadcast_to(jnp.max(mx, axis=-1, keepdims=True), (tq, LANES)))
    outs = []
    for h in range(2):
        hc = slice(h * LANES, (h + 1) * LANES)
        mb = jnp.concatenate([mbs[h]] * (KEY_TILE // LANES), axis=1)
        acc = jnp.zeros((tq, LANES), F32)
        for t in range(nt):
            keys = slice(t * KEY_TILE, (t + 1) * KEY_TILE)
            p = jnp.exp2(s_scr[h, :, keys] - mb).astype(BF16)
            acc = acc + _dot(p, v_ref[keys, hc])
        ones_lane = V_HEAD if h == 0 else 0
        outs.append(acc / acc[:, ones_lane:ones_lane + 1])
    lane = lax.broadcasted_iota(jnp.int32, (tq, LANES), 1)
    o_ref[...] = jnp.where(lane < V_HEAD, outs[0], outs[1]).astype(BF16)


def _attn(q, k, v, *, batch, seq, tq):
    n = q.shape[0]
    nq = seq // tq
    pairs = MLA_HEADS // 2
    return pl.pallas_call(
        functools.partial(_attn_kernel, tq=tq),
        grid=(batch, pairs, nq),
        in_specs=[
            pl.BlockSpec((tq, 2 * LANES), lambda b, p, i: (b * nq + i, p)),
            pl.BlockSpec((seq, 2 * LANES), lambda b, p, i: (b, p)),
            pl.BlockSpec((seq, 2 * LANES), lambda b, p, i: (b, p)),
        ],
        out_specs=pl.BlockSpec((tq, LANES), lambda b, p, i: (b * nq + i, p)),
        out_shape=jax.ShapeDtypeStruct((n, MLA_HEADS * V_HEAD), BF16),
        scratch_shapes=[pltpu.VMEM((2, tq, seq), F32)],
        compiler_params=_params("parallel", "parallel", "arbitrary"),
        name="attn",
    )(q, k, v)


def _log_sigmoid(x):
    return jnp.minimum(x, 0.0) - jnp.log1p(jnp.exp(-jnp.abs(x)))


def _gates_kernel(g_ref, c_ref, dec_ref, cols_ref, tot_s, mloc_s, mpf_s, mpb_s, *, nc):
    rows = nc * N_STREAMS
    shape = (rows, CHUNK)
    three = (nc, N_STREAMS, CHUNK)
    li = g_ref[:, 0:N_STREAMS, :].reshape(shape)
    lf = _log_sigmoid(g_ref[:, N_STREAMS:2 * N_STREAMS, :].reshape(shape))
    fwd = (lax.broadcasted_iota(jnp.int32, shape, 0) & M_HEADS) == 0
    lane = lax.broadcasted_iota(jnp.int32, shape, 1)
    shifts = (1, 2, 4, 8, 16, 32, 64)

    pre = lf
    for sh in shifts:
        pre = pre + jnp.where(lane >= sh, pltpu.roll(pre, sh, axis=1), 0.0)
    tot = jnp.broadcast_to(pre[:, CHUNK - 1:CHUNK], shape)
    b = jnp.where(fwd, pre, tot - pre + lf)
    c = li - b
    run_f = c
    run_b = c
    for sh in shifts:
        run_f = jnp.maximum(run_f, jnp.where(lane >= sh, pltpu.roll(run_f, sh, axis=1), -jnp.inf))
        run_b = jnp.maximum(run_b, jnp.where(lane < CHUNK - sh, pltpu.roll(run_b, CHUNK - sh, axis=1), -jnp.inf))
    run = jnp.where(fwd, run_f, run_b)
    m_loc = tot + jnp.broadcast_to(jnp.max(c, axis=1, keepdims=True), shape)

    tot_s[...] = tot.reshape(three)
    mloc_s[...] = m_loc.reshape(three)
    fwd8 = lax.broadcasted_iota(jnp.int32, (N_STREAMS, CHUNK), 0) < M_HEADS

    def step(i, m):
        jb = nc - 1 - i
        mpf_s[i] = m
        mpb_s[jb] = m
        return jnp.maximum(jnp.where(fwd8, tot_s[i], tot_s[jb]) + m, jnp.where(fwd8, mloc_s[i], mloc_s[jb]))

    lax.fori_loop(0, nc, step, jnp.zeros((N_STREAMS, CHUNK), F32))
    m_prev = jnp.where(fwd, mpf_s[...].reshape(shape), mpb_s[...].reshape(shape))

    m_run = jnp.maximum(m_prev, run)
    inter = jnp.exp(m_prev - m_run)
    floor = jnp.exp(-(b + m_run))
    w = jnp.exp(tot + c - m_loc)
    m_new = jnp.maximum(tot + m_prev, m_loc)
    c_ref[...] = c.reshape(three)
    dec_ref[:, 0:N_STREAMS, :] = jnp.exp(tot + m_prev - m_new).reshape(three)
    dec_ref[:, N_STREAMS:2 * N_STREAMS, :] = jnp.exp(m_loc - m_new).reshape(three)
    quantities = [a.reshape(three) for a in (m_run, inter, floor, w)]
    for blk in range(nc // BLOCK_CHUNKS):
        tile = jnp.concatenate([a[blk * BLOCK_CHUNKS + cq] for cq in range(BLOCK_CHUNKS) for a in quantities], axis=0)
        cols_ref[blk] = tile.T


def _gates(g, *, batch, seq):
    nc = seq // CHUNK
    nchunks = g.shape[0]
    three = (nc, N_STREAMS, CHUNK)
    return pl.pallas_call(
        functools.partial(_gates_kernel, nc=nc),
        grid=(batch,),
        in_specs=[pl.BlockSpec((nc, 2 * N_STREAMS, CHUNK), lambda b: (b, 0, 0))],
        out_specs=(
            pl.BlockSpec(three, lambda b: (b, 0, 0)),
            pl.BlockSpec((nc, 2 * N_STREAMS, CHUNK), lambda b: (b, 0, 0)),
            pl.BlockSpec((nc // BLOCK_CHUNKS, CHUNK, LANES), lambda b: (b, 0, 0)),
        ),
        out_shape=(
            jax.ShapeDtypeStruct((nchunks, N_STREAMS, CHUNK), F32),
            jax.ShapeDtypeStruct((nchunks, 2 * N_STREAMS, CHUNK), F32),
            jax.ShapeDtypeStruct((nchunks // BLOCK_CHUNKS, CHUNK, LANES), F32),
        ),
        scratch_shapes=[pltpu.VMEM(three, F32)] * 4,
        compiler_params=_params("parallel"),
        name="gates",
    )(g)


def _mlstm_kernel(q_ref, k_ref, v_ref, c_ref, dec_ref, cols_ref, h_ref, st, *, direction):
    @pl.when(pl.program_id(1) == 0)
    def _():
        st[...] = jnp.zeros_like(st)

    row = lax.broadcasted_iota(jnp.int32, (CHUNK, CHUNK), 0)
    col = lax.broadcasted_iota(jnp.int32, (CHUNK, CHUNK), 1)
    mask = (col <= row) if direction == 0 else (col >= row)
    ones = (col == 0).astype(BF16)
    cols_t = cols_ref[0]
    order = range(BLOCK_CHUNKS) if direction == 0 else range(BLOCK_CHUNKS - 1, -1, -1)
    for hh in range(M_HEADS):
        hc = slice(hh * LANES, (hh + 1) * LANES)
        s = direction * M_HEADS + hh
        state = st[hh]
        for cq in order:
            rows = slice(cq * CHUNK, (cq + 1) * CHUNK)
            q = q_ref[rows, hc]
            k = k_ref[rows, hc]
            v1 = jnp.concatenate([v_ref[rows, hc], ones], axis=1)
            base = cq * 4 * N_STREAMS + s
            m_run, inter, floor, w = (cols_t[:, base + qi * N_STREAMS:base + qi * N_STREAMS + 1] for qi in range(4))
            d = jnp.where(mask, jnp.exp(c_ref[cq, s:s + 1, :] - m_run), 0.0)
            qk = (_dot_nt(q, k) * d).astype(BF16)
            nd = inter * _dot(q, state.astype(BF16)) + _dot(qk, v1)
            den = nd[:, LANES:LANES + 1]
            h_ref[rows, hc] = nd[:, 0:LANES] / jnp.maximum(jnp.abs(den), floor)
            kw = (k.astype(F32) * w).astype(BF16)
            state = (dec_ref[cq, s:s + 1, 0:1] * state
                     + dec_ref[cq, N_STREAMS + s:N_STREAMS + s + 1, 0:1] * _dot_tn(kw, v1))
        st[hh] = state


def _mlstm(mq, mk, mv, c_rows, dec, cols, *, batch, seq, direction):
    n = mq.shape[0]
    ts = BLOCK_CHUNKS * CHUNK
    nblk = seq // ts

    def blk(b, i):
        return b * nblk + (i if direction == 0 else nblk - 1 - i)

    tok = pl.BlockSpec((ts, M_WIDTH), lambda b, i: (blk(b, i), 0))
    return pl.pallas_call(
        functools.partial(_mlstm_kernel, direction=direction),
        grid=(batch, nblk),
        in_specs=[
            tok, tok, tok,
            pl.BlockSpec((BLOCK_CHUNKS, N_STREAMS, CHUNK), lambda b, i: (blk(b, i), 0, 0)),
            pl.BlockSpec((BLOCK_CHUNKS, 2 * N_STREAMS, CHUNK), lambda b, i: (blk(b, i), 0, 0)),
            pl.BlockSpec((1, CHUNK, LANES), lambda b, i: (blk(b, i), 0, 0)),
        ],
        out_specs=tok,
        out_shape=jax.ShapeDtypeStruct((n, M_WIDTH), F32),
        scratch_shapes=[pltpu.VMEM((M_HEADS, M_HEAD_DIM, 2 * LANES), F32)],
        compiler_params=_params("parallel", "arbitrary"),
        name="mlstm_fwd" if direction == 0 else "mlstm_bwd",
    )(mq, mk, mv, c_rows, dec, cols)


def _ab_out_kernel(x_ref, ya_ref, hf_ref, hb_ref, mo_ref, on_ref, wa_ref, wm_ref, o_ref):
    parts = []
    for hh in range(M_HEADS):
        hc = slice(hh * LANES, (hh + 1) * LANES)
        t = hf_ref[:, hc] + hb_ref[:, hc]
        t = t * lax.rsqrt(jnp.mean(t * t, axis=-1, keepdims=True) + EPS) * on_ref[:, hc]
        parts.append((jax.nn.sigmoid(mo_ref[:, hc]) * t).astype(BF16))
    ym = jnp.concatenate(parts, axis=1)
    o_ref[...] = x_ref[...] + _dot(ya_ref[...], wa_ref[...]) + _dot(ym, wm_ref[...])


def _ab_out(x, ya, hf, hb, mo, on, wa, wm, *, tm):
    n = x.shape[0]
    row = lambda i: (i, 0)
    narrow = pl.BlockSpec((tm, M_WIDTH), row)
    return pl.pallas_call(
        _ab_out_kernel,
        grid=(n // tm,),
        in_specs=[
            pl.BlockSpec((tm, D_MODEL), row), narrow, narrow, narrow, narrow,
            _const_spec(on.shape), _const_spec(wa.shape), _const_spec(wm.shape),
        ],
        out_specs=pl.BlockSpec((tm, D_MODEL), row),
        out_shape=jax.ShapeDtypeStruct((n, D_MODEL), F32),
        compiler_params=_params("parallel"),
        name="ab_out",
    )(x, ya, hf, hb, mo, on, wa, wm)


def _ffn_kernel(x_ref, xp_ref, xn_ref, nrm_ref, wup_ref, cw_ref, cb_ref, wdn_ref, fin_ref, o_ref, us, act,
                *, tm, tps, final):
    jj = pl.program_id(0) % tps
    xa = jnp.concatenate([
        jnp.where(jj == 0, 0.0, xp_ref[...]),
        x_ref[...],
        jnp.where(jj == tps - 1, 0.0, xn_ref[...]),
    ], axis=0)
    xn = _rms(xa, nrm_ref[...]).astype(BF16)

    for c in range(D_FF // FF_COLS):
        g0 = c * FF_COLS
        v0 = D_FF + c * FF_COLS
        gate = _conv3(_dot(xn, wup_ref[:, g0:g0 + FF_COLS]), us.at[0], tm, cw_ref, cb_ref, g0)
        val = _conv3(_dot(xn, wup_ref[:, v0:v0 + FF_COLS]), us.at[1], tm, cw_ref, cb_ref, v0)
        act[:, g0:g0 + FF_COLS] = (gate * jax.nn.sigmoid(gate) * val).astype(BF16)
    y = x_ref[...] + _dot(act[...], wdn_ref[...])
    if final:
        y = _rms(y, fin_ref[...])
    o_ref[...] = y


def _ffn(x, nrm, wup, cw, cb, wdn, fin, *, seq, tm, final):
    n = x.shape[0]
    row = lambda i: (i, 0)
    return pl.pallas_call(
        functools.partial(_ffn_kernel, tm=tm, tps=seq // tm, final=final),
        grid=(n // tm,),
        in_specs=[
            pl.BlockSpec((tm, D_MODEL), row), *_halo_specs(tm, n, D_MODEL),
            _const_spec(nrm.shape), _const_spec(wup.shape), _const_spec(cw.shape), _const_spec(cb.shape),
            _const_spec(wdn.shape), _const_spec(fin.shape),
        ],
        out_specs=pl.BlockSpec((tm, D_MODEL), row),
        out_shape=jax.ShapeDtypeStruct((n, D_MODEL), F32),
        scratch_shapes=[
            pltpu.VMEM((2, FF_COLS // LANES, ROW_PITCH * (tm + 2 * HALO), LANES), F32),
            pltpu.VMEM((tm, D_FF), BF16),
        ],
        compiler_params=_params("parallel"),
        name="ffn_final" if final else "ffn",
    )(x, x, x, nrm, wup, cw, cb, wdn, fin)


def _gelu(x):
    return 0.5 * x * (1.0 + lax.erf(x * (2.0 ** -0.5)))


def _mix_c_kernel(x_ref, nrm_ref, win_ref, vn_ref, ws_ref, bs_ref, wout_ref, o_ref, zs, *, tm):
    x = x_ref[...]
    xn = _rms(x, nrm_ref[...]).astype(BF16)
    u = _gelu(_dot(xn, win_ref[:, 0:D_MODEL]))
    v = _gelu(_dot(xn, win_ref[:, D_MODEL:2 * D_MODEL]))
    vn = _rms(v, vn_ref[...]).astype(BF16)
    nch = tm // CHUNK
    for g in range(G_GROUPS):
        cols = slice(g * LANES, (g + 1) * LANES)
        rhs = jnp.concatenate([vn[c * CHUNK:(c + 1) * CHUNK, cols] for c in range(nch)], axis=1)
        sv = _dot(ws_ref[g], rhs) + bs_ref[:, g:g + 1]
        for c in range(nch):
            zs[c * CHUNK:(c + 1) * CHUNK, cols] = (
                u[c * CHUNK:(c + 1) * CHUNK, cols] * sv[:, c * LANES:(c + 1) * LANES]).astype(BF16)
    o_ref[...] = x + _dot(zs[...], wout_ref[...])


def _mix_c(x, nrm, win, vn, ws, bs, wout, *, tm):
    n = x.shape[0]
    row = lambda i: (i, 0)
    return pl.pallas_call(
        functools.partial(_mix_c_kernel, tm=tm),
        grid=(n // tm,),
        in_specs=[
            pl.BlockSpec((tm, D_MODEL), row),
            _const_spec(nrm.shape), _const_spec(win.shape), _const_spec(vn.shape), _const_spec(ws.shape),
            _const_spec(bs.shape), _const_spec(wout.shape),
        ],
        out_specs=pl.BlockSpec((tm, D_MODEL), row),
        out_shape=jax.ShapeDtypeStruct((n, D_MODEL), F32),
        scratch_shapes=[pltpu.VMEM((tm, D_MODEL), BF16)],
        compiler_params=_params("parallel"),
        name="mix_c",
    )(x, nrm, win, vn, ws, bs, wout)


def _rope_tables(seq):
    pos = jnp.arange(seq, dtype=F32)
    inv = 1.0 / (ROPE_THETA ** (jnp.arange(0, QK_ROPE, 2, dtype=F32) / QK_ROPE))
    ang = pos[:, None] * inv[None, :]
    cos, sin = jnp.cos(ang), jnp.sin(ang)
    one = jnp.ones((seq, QK_NOPE), F32)
    zero_n = jnp.zeros((seq, QK_NOPE), F32)
    pad = jnp.zeros((seq, LANES - QK_NOPE - QK_ROPE), F32)
    return (jnp.concatenate([one, cos, cos, pad], axis=1), jnp.concatenate([zero_n, sin, sin, pad], axis=1))


def _head_block(nope, x1, x2):
    pad = jnp.zeros((nope.shape[0], LANES - QK_NOPE - QK_ROPE), nope.dtype)
    return jnp.concatenate([nope, x1, x2, pad], axis=1)


def _prep_even(i, ab_norm, ab_w_in, mla_q_norm, mla_w_uq, mla_kv_norm, mla_w_ukv, mlstm_conv_w, mlstm_conv_b,
               mlstm_gate_bias, mlstm_out_norm, ab_w_out):
    w_in = ab_w_in[i]
    half = QK_ROPE // 2
    zq = jnp.zeros((D_MODEL, QK_NOPE), F32)
    kr1 = w_in[:, 384:384 + half]
    kr2 = w_in[:, 384 + half:416]
    gate_order = jnp.array([0, 1, 2, 3, 8, 9, 10, 11, 4, 5, 6, 7, 12, 13, 14, 15])
    gates = jnp.pad(w_in[:, 2464:2480][:, gate_order], ((0, 0), (0, LANES - 2 * N_STREAMS)))
    w_big = jnp.concatenate([
        w_in[:, 0:384],
        _head_block(zq, kr1, kr2),
        _head_block(zq, -kr2, kr1),
        gates,
        w_in[:, 416:2464],
    ], axis=1).astype(BF16)
    w_uq = mla_w_uq[i].reshape(Q_LORA, MLA_HEADS, QK_NOPE + QK_ROPE)
    zn = jnp.zeros((Q_LORA, QK_NOPE), F32)
    wqa = jnp.concatenate([_head_block(w_uq[:, h, :QK_NOPE], w_uq[:, h, QK_NOPE:QK_NOPE + half],
                                       w_uq[:, h, QK_NOPE + half:]) for h in range(MLA_HEADS)], axis=1)
    wqb = jnp.concatenate([_head_block(zn, -w_uq[:, h, QK_NOPE + half:], w_uq[:, h, QK_NOPE:QK_NOPE + half])
                           for h in range(MLA_HEADS)], axis=1)
    w_ukv = mla_w_ukv[i].reshape(KV_LORA, MLA_HEADS, QK_NOPE + V_HEAD)
    wka = jnp.pad(w_ukv[:, :, :QK_NOPE], ((0, 0), (0, 0), (0, LANES - QK_NOPE))).reshape(KV_LORA, MLA_HEADS * LANES)
    zv = jnp.zeros((KV_LORA, V_HEAD), F32)
    wv = jnp.concatenate([jnp.concatenate([w_ukv[:, h, QK_NOPE:], zv] if h % 2 == 0 else [zv, w_ukv[:, h, QK_NOPE:]],
                                          axis=1) for h in range(MLA_HEADS)], axis=1)
    one_even = jnp.zeros((LANES,), F32).at[V_HEAD].set(1.0)
    one_odd = jnp.zeros((LANES,), F32).at[0].set(1.0)
    vone = jnp.concatenate([one_even if h % 2 == 0 else one_odd for h in range(MLA_HEADS)])[None, :]
    w_out = ab_w_out[i].astype(BF16)
    return dict(
        nrm=ab_norm[i][None, :], w_big=w_big, qn=mla_q_norm[i][None, :], wqa=wqa.astype(BF16),
        wqb=wqb.astype(BF16), kvn=mla_kv_norm[i][None, :], wka=wka.astype(BF16), wv=wv.astype(BF16),
        gb=jnp.broadcast_to(mlstm_gate_bias[i][gate_order][:, None], (2 * N_STREAMS, LANES)), vone=vone,
        conv_w=mlstm_conv_w[i], conv_b=mlstm_conv_b[i][None, :], on=mlstm_out_norm[i][None, :],
        wa=w_out[:MLA_HEADS * V_HEAD], wm=w_out[MLA_HEADS * V_HEAD:])


def _pick(pref, seq):
    return min(pref, seq)


def _forward(x3, even, odd, ffn, final_norm):
    batch, seq, _ = x3.shape
    x = x3.reshape(batch * seq, D_MODEL)
    tm = _pick(512, seq)
    cos_t, sin_t = _rope_tables(seq)
    e = even
    q, k, v, mq, mk, mv, mo, g = _ab_in(x, cos_t, sin_t, e, seq=seq, tm=tm)
    ya = _attn(q, k, v, batch=batch, seq=seq, tq=_pick(SCORE_ELEMS // seq, seq))
    c_rows, dec, cols = _gates(g, batch=batch, seq=seq)
    hf = _mlstm(mq, mk, mv, c_rows, dec, cols, batch=batch, seq=seq, direction=0)
    hb = _mlstm(mq, mk, mv, c_rows, dec, cols, batch=batch, seq=seq, direction=1)
    x = _ab_out(x, ya, hf, hb, mo, e["on"], e["wa"], e["wm"], tm=tm)
    f = ffn[0]
    x = _ffn(x, f["nrm"], f["wup"], f["cw"], f["cb"], f["wdn"], final_norm, seq=seq, tm=tm, final=False)
    o = odd
    x = _mix_c(x, o["nrm"], o["win"], o["vn"], o["ws"], o["bs"], o["wout"], tm=tm)
    f = ffn[1]
    x = _ffn(x, f["nrm"], f["wup"], f["cw"], f["cb"], f["wdn"], final_norm, seq=seq, tm=tm, final=True)
    return x.reshape(batch, seq, D_MODEL)


def kernel(x_prompt, x_sample, ab_norm, ab_w_in, mla_q_norm, mla_w_uq, mla_kv_norm, mla_w_ukv, mlstm_conv_w,
           mlstm_conv_b, mlstm_gate_bias, mlstm_out_norm, ab_w_out, c_norm, c_w_in, c_v_norm, c_w_spatial,
           c_b_spatial, c_w_out, ffn_norm, ffn_w_up, ffn_conv_w, ffn_conv_b, ffn_w_down, final_norm):
    even = _prep_even(0, ab_norm, ab_w_in, mla_q_norm, mla_w_uq, mla_kv_norm, mla_w_ukv, mlstm_conv_w, mlstm_conv_b,
                      mlstm_gate_bias, mlstm_out_norm, ab_w_out)
    odd = dict(nrm=c_norm[0][None, :], win=c_w_in[0].astype(BF16), vn=c_v_norm[0][None, :],
               ws=c_w_spatial[0].astype(BF16), bs=c_b_spatial[0].T, wout=c_w_out[0].astype(BF16))
    ffn = [dict(nrm=ffn_norm[l][None, :], wup=ffn_w_up[l].astype(BF16), cw=ffn_conv_w[l], cb=ffn_conv_b[l][None, :],
                wdn=ffn_w_down[l].astype(BF16)) for l in range(2)]
    fin = final_norm[None, :]
    return (_forward(x_prompt, even, odd, ffn, fin), _forward(x_sample, even, odd, ffn, fin))
```

```python
import functools

import jax
import jax.numpy as jnp
from jax import lax
from jax.experimental import pallas as pl
from jax.experimental.pallas import tpu as pltpu

F32 = jnp.float32
BF16 = jnp.bfloat16

EPS = 1e-6
D_MODEL = 1024
MLA_HEADS = 8
Q_LORA = 256
KV_LORA = 128
QK_NOPE = 64
QK_ROPE = 32
V_HEAD = 64
ROPE_THETA = 10000.0
M_HEADS = 4
M_HEAD_DIM = 128
M_WIDTH = M_HEADS * M_HEAD_DIM
N_STREAMS = 2 * M_HEADS
CHUNK = 128
BLOCK_CHUNKS = 4
G_GROUPS = 8
D_FF = 2816
LANES = 128
SUBLANES = 8
HALO = SUBLANES
MXU_COLS = 256
ROW_PITCH = 2
FF_COLS = MXU_COLS
KEY_TILE = MXU_COLS
SCORE_ELEMS = 2 * 1024 * 1024
LOG2_E = 1.4426950408889634
VMEM_LIMIT = 56 * 1024 * 1024

_C_Q, _C_KV, _C_KRA, _C_KRB, _C_G, _C_MQK, _C_MV, _C_MO, _C_END = (
    0, 256, 384, 512, 640, 768, 1792, 2304, 2816)


def _rms(x, g):
    return x * lax.rsqrt(jnp.mean(x * x, axis=-1, keepdims=True) + EPS) * g


def _const_spec(shape):
    nd = len(shape)
    return pl.BlockSpec(shape, lambda *_: (0,) * nd, pipeline_mode=pl.Buffered(1))


def _params(*sem):
    return pltpu.CompilerParams(dimension_semantics=sem, vmem_limit_bytes=VMEM_LIMIT)


def _dot(a, b):
    return jnp.dot(a, b, preferred_element_type=F32)


def _dot_nt(a, b):
    return lax.dot_general(a, b, (((1,), (1,)), ((), ())), preferred_element_type=F32)


def _halo_specs(tile_rows, n_rows, width):
    tb = tile_rows // HALO
    last = n_rows // HALO - 1
    return (pl.BlockSpec((HALO, width), lambda i: (jnp.maximum(i * tb - 1, 0), 0)),
            pl.BlockSpec((HALO, width), lambda i: (jnp.minimum((i + 1) * tb, last), 0)))


def _conv3(y, u, rows, w_ref, b_ref, c0):
    tall = rows + 2 * HALO
    outs = []
    for s in range(y.shape[1] // LANES):
        lanes = slice(s * LANES, (s + 1) * LANES)
        cols = slice(c0 + s * LANES, c0 + (s + 1) * LANES)
        u[s, pl.ds(0, tall, stride=ROW_PITCH), :] = y[:, lanes]
        prv = u[s, pl.ds(ROW_PITCH * (HALO - 1), rows, stride=ROW_PITCH), :]
        nxt = u[s, pl.ds(ROW_PITCH * (HALO + 1), rows, stride=ROW_PITCH), :]
        outs.append(prv * w_ref[0:1, cols] + y[HALO:HALO + rows, lanes] * w_ref[1:2, cols]
                    + nxt * w_ref[2:3, cols] + b_ref[:, cols])
    return jnp.concatenate(outs, axis=1)


def _ab_in_kernel(x_ref, xp_ref, xn_ref, cos_ref, sin_ref, nrm_ref, w_ref, qn_ref, wqa_ref, wqb_ref, kvn_ref,
                  wka_ref, wv_ref, gb_ref, vone_ref, cw_ref, cb_ref,
                  q_ref, k_ref, v_ref, mq_ref, mk_ref, mv_ref, mo_ref, g_ref, us, *, tm, tps):
    jj = pl.program_id(0) % tps
    xa = jnp.concatenate([
        jnp.where(jj == 0, 0.0, xp_ref[...]),
        x_ref[...],
        jnp.where(jj == tps - 1, 0.0, xn_ref[...]),
    ], axis=0)
    xe = _rms(xa, nrm_ref[...]).astype(BF16)
    xn = xe[HALO:HALO + tm]

    def proj(a, b):
        return _dot(xn, w_ref[:, a:b])

    cq = proj(_C_Q, _C_KV)
    ckv_kra = proj(_C_KV, _C_KRB)
    krb_g = proj(_C_KRB, _C_MQK)

    nq_blocks = M_WIDTH // MXU_COLS
    for cb in range(2 * nq_blocks):
        cols = slice(cb * MXU_COLS, (cb + 1) * MXU_COLS)
        y = _conv3(_dot(xe, w_ref[:, _C_MQK + cb * MXU_COLS:_C_MQK + (cb + 1) * MXU_COLS]), us.at[cb % 2], tm,
                   cw_ref, cb_ref, cb * MXU_COLS)
        y = y * jax.nn.sigmoid(y)
        if cb < nq_blocks:
            mq_ref[:, cols] = y.astype(BF16)
        else:
            mk_ref[:, (cb - nq_blocks) * MXU_COLS:(cb - nq_blocks + 1) * MXU_COLS] = (
                y * (M_HEAD_DIM ** -0.5)).astype(BF16)
    mv_ref[...] = proj(_C_MV, _C_MO).astype(BF16)
    mo_ref[...] = proj(_C_MO, _C_END)
    gt = krb_g[:, LANES:2 * LANES].T
    for c in range(tm // CHUNK):
        g_ref[c] = gt[0:2 * N_STREAMS, c * CHUNK:(c + 1) * CHUNK] + gb_ref[:, 0:1]

    cos = cos_ref[...]
    sin = sin_ref[...]
    scale = (QK_NOPE + QK_ROPE) ** -0.5 * LOG2_E
    cqn = _rms(cq, qn_ref[...]).astype(BF16)
    qa = _dot(cqn, wqa_ref[...])
    qb = _dot(cqn, wqb_ref[...])
    cos_q = cos * scale
    sin_q = sin * scale
    for h in range(MLA_HEADS):
        blk = slice(h * LANES, (h + 1) * LANES)
        q_ref[:, blk] = (qa[:, blk] * cos_q + qb[:, blk] * sin_q).astype(BF16)

    ckvn = _rms(ckv_kra[:, 0:LANES], kvn_ref[...]).astype(BF16)
    ka = _dot(ckvn, wka_ref[...])
    kr = ckv_kra[:, LANES:2 * LANES] * cos + krb_g[:, 0:LANES] * sin
    for h in range(MLA_HEADS):
        blk = slice(h * LANES, (h + 1) * LANES)
        k_ref[:, blk] = (ka[:, blk] + kr).astype(BF16)
    v_ref[...] = (_dot(ckvn, wv_ref[...]) + vone_ref[...]).astype(BF16)


def _ab_in(x, cos_t, sin_t, e, *, seq, tm):
    n = x.shape[0]
    nt = n // tm
    tps = seq // tm
    row = lambda i: (i, 0)
    tab = lambda i: (i % tps, 0)
    consts = [e[k] for k in ("nrm", "w_big", "qn", "wqa", "wqb", "kvn", "wka", "wv", "gb", "vone", "conv_w",
                             "conv_b")]
    out_shape = (
        jax.ShapeDtypeStruct((n, MLA_HEADS * LANES), BF16),
        jax.ShapeDtypeStruct((n, MLA_HEADS * LANES), BF16),
        jax.ShapeDtypeStruct((n, MLA_HEADS * LANES), BF16),
        jax.ShapeDtypeStruct((n, M_WIDTH), BF16),
        jax.ShapeDtypeStruct((n, M_WIDTH), BF16),
        jax.ShapeDtypeStruct((n, M_WIDTH), BF16),
        jax.ShapeDtypeStruct((n, M_WIDTH), F32),
        jax.ShapeDtypeStruct((n // CHUNK, 2 * N_STREAMS, CHUNK), F32),
    )
    wide = pl.BlockSpec((tm, MLA_HEADS * LANES), row)
    narrow = pl.BlockSpec((tm, M_WIDTH), row)
    return pl.pallas_call(
        functools.partial(_ab_in_kernel, tm=tm, tps=tps),
        grid=(nt,),
        in_specs=[pl.BlockSpec((tm, D_MODEL), row), *_halo_specs(tm, n, D_MODEL),
                  pl.BlockSpec((tm, LANES), tab), pl.BlockSpec((tm, LANES), tab)]
                 + [_const_spec(c.shape) for c in consts],
        out_specs=(wide, wide, wide, narrow, narrow, narrow, narrow,
                   pl.BlockSpec((tm // CHUNK, 2 * N_STREAMS, CHUNK), lambda i: (i, 0, 0))),
        out_shape=out_shape,
        scratch_shapes=[pltpu.VMEM((2, MXU_COLS // LANES, ROW_PITCH * (tm + 2 * HALO), LANES), F32)],
        compiler_params=_params("parallel"),
        name="ab_in",
    )(x, x, x, cos_t, sin_t, *consts)


def _attn_kernel(q_ref, k_ref, v_ref, o_ref, s_scr, *, tq):
    nt = k_ref.shape[0] // KEY_TILE
    mbs = []
    for h in range(2):
        hc = slice(h * LANES, (h + 1) * LANES)
        q = q_ref[:, hc]
        mx = jnp.full((tq, LANES), -jnp.inf, F32)
        for t in range(nt):
            keys = slice(t * KEY_TILE, (t + 1) * KEY_TILE)
            s = _dot_nt(q, k_ref[keys, hc])
            s_scr[h, :, keys] = s
            for j in range(KEY_TILE // LANES):
                mx = jnp.maximum(mx, s[:, j * LANES:(j + 1) * LANES])
        mbs.append(jnp.broadcast_to(jnp.max(mx, axis=-1, keepdims=True), (tq, LANES)))
    outs = []
    for h in range(2):
        hc = slice(h * LANES, (h + 1) * LANES)
        mb = jnp.concatenate([mbs[h]] * (KEY_TILE // LANES), axis=1)
        acc = jnp.zeros((tq, LANES), F32)
        for t in range(nt):
            keys = slice(t * KEY_TILE, (t + 1) * KEY_TILE)
            p = jnp.exp2(s_scr[h, :, keys] - mb).astype(BF16)
            acc = acc + _dot(p, v_ref[keys, hc])
        ones_lane = V_HEAD if h == 0 else 0
        outs.append(acc / acc[:, ones_lane:ones_lane + 1])
    lane = lax.broadcasted_iota(jnp.int32, (tq, LANES), 1)
    o_ref[...] = jnp.where(lane < V_HEAD, outs[0], outs[1]).astype(BF16)


def _attn(q, k, v, *, batch, seq, tq):
    n = q.shape[0]
    nq = seq // tq
    pairs = MLA_HEADS // 2
    return pl.pallas_call(
        functools.partial(_attn_kernel, tq=tq),
        grid=(batch, pairs, nq),
        in_specs=[
            pl.BlockSpec((tq, 2 * LANES), lambda b, p, i: (b * nq + i, p)),
            pl.BlockSpec((seq, 2 * LANES), lambda b, p, i: (b, p)),
            pl.BlockSpec((seq, 2 * LANES), lambda b, p, i: (b, p)),
        ],
        out_specs=pl.BlockSpec((tq, LANES), lambda b, p, i: (b * nq + i, p)),
        out_shape=jax.ShapeDtypeStruct((n, MLA_HEADS * V_HEAD), BF16),
        scratch_shapes=[pltpu.VMEM((2, tq, seq), F32)],
        compiler_params=_params("parallel", "parallel", "arbitrary"),
        name="attn",
    )(q, k, v)


def _log_sigmoid(x):
    return jnp.minimum(x, 0.0) - jnp.log1p(jnp.exp(-jnp.abs(x)))


def _gates_kernel(g_ref, c_ref, dec_ref, cols_ref, tot_s, mloc_s, mpf_s, mpb_s, *, nc):
    rows = nc * N_STREAMS
    shape = (rows, CHUNK)
    three = (nc, N_STREAMS, CHUNK)
    li = g_ref[:, 0:N_STREAMS, :].reshape(shape)
    lf = _log_sigmoid(g_ref[:, N_STREAMS:2 * N_STREAMS, :].reshape(shape))
    fwd = (lax.broadcasted_iota(jnp.int32, shape, 0) & M_HEADS) == 0
    lane = lax.broadcasted_iota(jnp.int32, shape, 1)
    shifts = (1, 2, 4, 8, 16, 32, 64)

    pre = lf
    for sh in shifts:
        pre = pre + jnp.where(lane >= sh, pltpu.roll(pre, sh, axis=1), 0.0)
    tot = jnp.broadcast_to(pre[:, CHUNK - 1:CHUNK], shape)
    b = jnp.where(fwd, pre, tot - pre + lf)
    c = li - b
    run_f = c
    run_b = c
    for sh in shifts:
        run_f = jnp.maximum(run_f, jnp.where(lane >= sh, pltpu.roll(run_f, sh, axis=1), -jnp.inf))
        run_b = jnp.maximum(run_b, jnp.where(lane < CHUNK - sh, pltpu.roll(run_b, CHUNK - sh, axis=1), -jnp.inf))
    run = jnp.where(fwd, run_f, run_b)
    m_loc = tot + jnp.broadcast_to(jnp.max(c, axis=1, keepdims=True), shape)

    tot_s[...] = tot.reshape(three)
    mloc_s[...] = m_loc.reshape(three)
    fwd8 = lax.broadcasted_iota(jnp.int32, (N_STREAMS, CHUNK), 0) < M_HEADS

    def step(i, m):
        jb = nc - 1 - i
        mpf_s[i] = m
        mpb_s[jb] = m
        return jnp.maximum(jnp.where(fwd8, tot_s[i], tot_s[jb]) + m, jnp.where(fwd8, mloc_s[i], mloc_s[jb]))

    lax.fori_loop(0, nc, step, jnp.zeros((N_STREAMS, CHUNK), F32))
    m_prev = jnp.where(fwd, mpf_s[...].reshape(shape), mpb_s[...].reshape(shape))

    m_run = jnp.maximum(m_prev, run)
    floor = jnp.exp(-(b + m_run))
    w = jnp.exp(tot + c - m_loc)
    m_new = jnp.maximum(tot + m_prev, m_loc)
    c_ref[:, 0:N_STREAMS, :] = c.reshape(three)
    c_ref[:, N_STREAMS:2 * N_STREAMS, :] = w.reshape(three)
    c_ref[:, 2 * N_STREAMS:3 * N_STREAMS, :] = m_prev.reshape(three)
    dec_ref[:, 0:N_STREAMS, :] = jnp.exp(tot + m_prev - m_new).reshape(three)
    dec_ref[:, N_STREAMS:2 * N_STREAMS, :] = jnp.exp(m_loc - m_new).reshape(three)
    quantities = [a.reshape(three) for a in (m_run, floor)]
    pad = jnp.zeros((CHUNK - BLOCK_CHUNKS * 2 * N_STREAMS, CHUNK), F32)
    for blk in range(nc // BLOCK_CHUNKS):
        tile = jnp.concatenate(
            [a[blk * BLOCK_CHUNKS + cq] for cq in range(BLOCK_CHUNKS) for a in quantities] + [pad], axis=0)
        cols_ref[blk] = tile.T


def _gates(g, *, batch, seq):
    nc = seq // CHUNK
    nchunks = g.shape[0]
    three = (nc, N_STREAMS, CHUNK)
    return pl.pallas_call(
        functools.partial(_gates_kernel, nc=nc),
        grid=(batch,),
        in_specs=[pl.BlockSpec((nc, 2 * N_STREAMS, CHUNK), lambda b: (b, 0, 0))],
        out_specs=(
            pl.BlockSpec((nc, 3 * N_STREAMS, CHUNK), lambda b: (b, 0, 0)),
            pl.BlockSpec((nc, 2 * N_STREAMS, CHUNK), lambda b: (b, 0, 0)),
            pl.BlockSpec((nc // BLOCK_CHUNKS, CHUNK, LANES), lambda b: (b, 0, 0)),
        ),
        out_shape=(
            jax.ShapeDtypeStruct((nchunks, 3 * N_STREAMS, CHUNK), F32),
            jax.ShapeDtypeStruct((nchunks, 2 * N_STREAMS, CHUNK), F32),
            jax.ShapeDtypeStruct((nchunks // BLOCK_CHUNKS, CHUNK, LANES), F32),
        ),
        scratch_shapes=[pltpu.VMEM(three, F32)] * 4,
        compiler_params=_params("parallel"),
        name="gates",
    )(g)


def _mlstm_kernel(q_ref, k_ref, v_ref, c_ref, dec_ref, cols_ref, h_ref, st, bc, *, direction):
    @pl.when(pl.program_id(1) == 0)
    def _():
        st[...] = jnp.zeros_like(st)

    row = lax.broadcasted_iota(jnp.int32, (CHUNK, CHUNK), 0)
    col = lax.broadcasted_iota(jnp.int32, (CHUNK, CHUNK), 1)
    mask = (col <= row) if direction == 0 else (col >= row)
    ones = (col == 0).astype(BF16)
    cols_t = cols_ref[0]
    order = range(BLOCK_CHUNKS) if direction == 0 else range(BLOCK_CHUNKS - 1, -1, -1)
    for cq in range(BLOCK_CHUNKS):
        for hh in range(M_HEADS):
            base = cq * 2 * N_STREAMS + direction * M_HEADS + hh
            for qi in range(2):
                lane_i = base + qi * N_STREAMS
                bc[(qi * BLOCK_CHUNKS + cq) * M_HEADS + hh] = jnp.broadcast_to(
                    cols_t[:, lane_i:lane_i + 1], (CHUNK, LANES))
    heads = range(M_HEADS)
    hcs = [slice(hh * LANES, (hh + 1) * LANES) for hh in heads]
    qs, v1s, qk, c_loc = {}, {}, {}, {}
    for cq in order:
        rows = slice(cq * CHUNK, (cq + 1) * CHUNK)
        for hh in heads:
            qs[cq, hh] = q_ref[rows, hcs[hh]]
            qk[cq, hh] = _dot_nt(qs[cq, hh], k_ref[rows, hcs[hh]])
    for cq in order:
        rows = slice(cq * CHUNK, (cq + 1) * CHUNK)
        for hh in heads:
            s = direction * M_HEADS + hh
            v1s[cq, hh] = jnp.concatenate([v_ref[rows, hcs[hh]], ones], axis=1)
            kw_t = (k_ref[rows, hcs[hh]].astype(F32).T
                    * c_ref[cq, N_STREAMS + s:N_STREAMS + s + 1, :]).astype(BF16)
            c_loc[cq, hh] = _dot(kw_t, v1s[cq, hh])
    for cq in order:
        rows = slice(cq * CHUNK, (cq + 1) * CHUNK)
        for hh in heads:
            s = direction * M_HEADS + hh
            state = st[hh]
            m_run = bc[cq * M_HEADS + hh]
            floor = bc[(BLOCK_CHUNKS + cq) * M_HEADS + hh]
            c_ext = jnp.concatenate([c_ref[cq, s:s + 1, :], c_ref[cq, 2 * N_STREAMS + s:2 * N_STREAMS + s + 1, :]],
                                    axis=1)
            d_ext = jnp.exp(c_ext - jnp.concatenate([m_run, m_run], axis=1))
            p_ext = jnp.concatenate([qk[cq, hh] * jnp.where(mask, d_ext[:, 0:LANES], 0.0),
                                     qs[cq, hh].astype(F32) * d_ext[:, LANES:2 * LANES]], axis=1).astype(BF16)
            nd = _dot(p_ext, jnp.concatenate([v1s[cq, hh], state.astype(BF16)], axis=0))
            den = nd[:, LANES:LANES + 1]
            h_ref[rows, hcs[hh]] = nd[:, 0:LANES] / jnp.maximum(jnp.abs(den), floor)
            st[hh] = (dec_ref[cq, s:s + 1, 0:1] * state
                      + dec_ref[cq, N_STREAMS + s:N_STREAMS + s + 1, 0:1] * c_loc[cq, hh])


def _mlstm(mq, mk, mv, c_rows, dec, cols, *, batch, seq, direction):
    n = mq.shape[0]
    ts = BLOCK_CHUNKS * CHUNK
    nblk = seq // ts

    def blk(b, i):
        return b * nblk + (i if direction == 0 else nblk - 1 - i)

    tok = pl.BlockSpec((ts, M_WIDTH), lambda b, i: (blk(b, i), 0))
    return pl.pallas_call(
        functools.partial(_mlstm_kernel, direction=direction),
        grid=(batch, nblk),
        in_specs=[
            tok, tok, tok,
            pl.BlockSpec((BLOCK_CHUNKS, 3 * N_STREAMS, CHUNK), lambda b, i: (blk(b, i), 0, 0)),
            pl.BlockSpec((BLOCK_CHUNKS, 2 * N_STREAMS, CHUNK), lambda b, i: (blk(b, i), 0, 0)),
            pl.BlockSpec((1, CHUNK, LANES), lambda b, i: (blk(b, i), 0, 0)),
        ],
        out_specs=tok,
        out_shape=jax.ShapeDtypeStruct((n, M_WIDTH), F32),
        scratch_shapes=[pltpu.VMEM((M_HEADS, M_HEAD_DIM, 2 * LANES), F32),
                        pltpu.VMEM((2 * BLOCK_CHUNKS * M_HEADS, CHUNK, LANES), F32)],
        compiler_params=_params("parallel", "arbitrary"),
        name="mlstm_fwd" if direction == 0 else "mlstm_bwd",
    )(mq, mk, mv, c_rows, dec, cols)


def _ab_out_kernel(x_ref, ya_ref, hf_ref, hb_ref, mo_ref, on_ref, wa_ref, wm_ref, o_ref):
    parts = []
    for hh in range(M_HEADS):
        hc = slice(hh * LANES, (hh + 1) * LANES)
        t = hf_ref[:, hc] + hb_ref[:, hc]
        t = t * lax.rsqrt(jnp.mean(t * t, axis=-1, keepdims=True) + EPS) * on_ref[:, hc]
        parts.append((jax.nn.sigmoid(mo_ref[:, hc]) * t).astype(BF16))
    ym = jnp.concatenate(parts, axis=1)
    o_ref[...] = x_ref[...] + _dot(ya_ref[...], wa_ref[...]) + _dot(ym, wm_ref[...])


def _ab_out(x, ya, hf, hb, mo, on, wa, wm, *, tm):
    n = x.shape[0]
    row = lambda i: (i, 0)
    narrow = pl.BlockSpec((tm, M_WIDTH), row)
    return pl.pallas_call(
        _ab_out_kernel,
        grid=(n // tm,),
        in_specs=[
            pl.BlockSpec((tm, D_MODEL), row), narrow, narrow, narrow, narrow,
            _const_spec(on.shape), _const_spec(wa.shape), _const_spec(wm.shape),
        ],
        out_specs=pl.BlockSpec((tm, D_MODEL), row),
        out_shape=jax.ShapeDtypeStruct((n, D_MODEL), F32),
        compiler_params=_params("parallel"),
        name="ab_out",
    )(x, ya, hf, hb, mo, on, wa, wm)


def _ffn_kernel(x_ref, xp_ref, xn_ref, nrm_ref, wup_ref, cw_ref, cb_ref, wdn_ref, fin_ref, o_ref, us, act,
                *, tm, tps, final):
    jj = pl.program_id(0) % tps
    xa = jnp.concatenate([
        jnp.where(jj == 0, 0.0, xp_ref[...]),
        x_ref[...],
        jnp.where(jj == tps - 1, 0.0, xn_ref[...]),
    ], axis=0)
    xn = _rms(xa, nrm_ref[...]).astype(BF16)

    for c in range(D_FF // FF_COLS):
        g0 = c * FF_COLS
        v0 = D_FF + c * FF_COLS
        gate = _conv3(_dot(xn, wup_ref[:, g0:g0 + FF_COLS]), us.at[0], tm, cw_ref, cb_ref, g0)
        val = _conv3(_dot(xn, wup_ref[:, v0:v0 + FF_COLS]), us.at[1], tm, cw_ref, cb_ref, v0)
        act[:, g0:g0 + FF_COLS] = (gate * jax.nn.sigmoid(gate) * val).astype(BF16)
    y = x_ref[...] + _dot(act[...], wdn_ref[...])
    if final:
        y = _rms(y, fin_ref[...])
    o_ref[...] = y


def _ffn(x, nrm, wup, cw, cb, wdn, fin, *, seq, tm, final):
    n = x.shape[0]
    row = lambda i: (i, 0)
    return pl.pallas_call(
        functools.partial(_ffn_kernel, tm=tm, tps=seq // tm, final=final),
        grid=(n // tm,),
        in_specs=[
            pl.BlockSpec((tm, D_MODEL), row), *_halo_specs(tm, n, D_MODEL),
            _const_spec(nrm.shape), _const_spec(wup.shape), _const_spec(cw.shape), _const_spec(cb.shape),
            _const_spec(wdn.shape), _const_spec(fin.shape),
        ],
        out_specs=pl.BlockSpec((tm, D_MODEL), row),
        out_shape=jax.ShapeDtypeStruct((n, D_MODEL), F32),
        scratch_shapes=[
            pltpu.VMEM((2, FF_COLS // LANES, ROW_PITCH * (tm + 2 * HALO), LANES), F32),
            pltpu.VMEM((tm, D_FF), BF16),
        ],
        compiler_params=_params("parallel"),
        name="ffn_final" if final else "ffn",
    )(x, x, x, nrm, wup, cw, cb, wdn, fin)


def _gelu(x):
    return 0.5 * x * (1.0 + lax.erf(x * (2.0 ** -0.5)))


def _mix_c_kernel(x_ref, nrm_ref, win_ref, vn_ref, ws_ref, bs_ref, wout_ref, o_ref, zs, *, tm):
    x = x_ref[...]
    xn = _rms(x, nrm_ref[...]).astype(BF16)
    v = _gelu(_dot(xn, win_ref[:, D_MODEL:2 * D_MODEL]))
    u = _gelu(_dot(xn, win_ref[:, 0:D_MODEL]))
    vn = _rms(v, vn_ref[...]).astype(BF16)
    nch = tm // CHUNK
    for g in range(G_GROUPS):
        cols = slice(g * LANES, (g + 1) * LANES)
        rhs = jnp.concatenate([vn[c * CHUNK:(c + 1) * CHUNK, cols] for c in range(nch)], axis=1)
        sv = _dot(ws_ref[g], rhs) + bs_ref[:, g:g + 1]
        for c in range(nch):
            zs[c * CHUNK:(c + 1) * CHUNK, cols] = (
                u[c * CHUNK:(c + 1) * CHUNK, cols] * sv[:, c * LANES:(c + 1) * LANES]).astype(BF16)
    o_ref[...] = x + _dot(zs[...], wout_ref[...])


def _mix_c(x, nrm, win, vn, ws, bs, wout, *, tm):
    n = x.shape[0]
    row = lambda i: (i, 0)
    return pl.pallas_call(
        functools.partial(_mix_c_kernel, tm=tm),
        grid=(n // tm,),
        in_specs=[
            pl.BlockSpec((tm, D_MODEL), row),
            _const_spec(nrm.shape), _const_spec(win.shape), _const_spec(vn.shape), _const_spec(ws.shape),
            _const_spec(bs.shape), _const_spec(wout.shape),
        ],
        out_specs=pl.BlockSpec((tm, D_MODEL), row),
        out_shape=jax.ShapeDtypeStruct((n, D_MODEL), F32),
        scratch_shapes=[pltpu.VMEM((tm, D_MODEL), BF16)],
        compiler_params=_params("parallel"),
        name="mix_c",
    )(x, nrm, win, vn, ws, bs, wout)


def _rope_tables(seq):
    pos = jnp.arange(seq, dtype=F32)
    inv = 1.0 / (ROPE_THETA ** (jnp.arange(0, QK_ROPE, 2, dtype=F32) / QK_ROPE))
    ang = pos[:, None] * inv[None, :]
    cos, sin = jnp.cos(ang), jnp.sin(ang)
    one = jnp.ones((seq, QK_NOPE), F32)
    zero_n = jnp.zeros((seq, QK_NOPE), F32)
    pad = jnp.zeros((seq, LANES - QK_NOPE - QK_ROPE), F32)
    return (jnp.concatenate([one, cos, cos, pad], axis=1), jnp.concatenate([zero_n, sin, sin, pad], axis=1))


def _head_block(nope, x1, x2):
    pad = jnp.zeros((nope.shape[0], LANES - QK_NOPE - QK_ROPE), nope.dtype)
    return jnp.concatenate([nope, x1, x2, pad], axis=1)


def _prep_even(i, ab_norm, ab_w_in, mla_q_norm, mla_w_uq, mla_kv_norm, mla_w_ukv, mlstm_conv_w, mlstm_conv_b,
               mlstm_gate_bias, mlstm_out_norm, ab_w_out):
    w_in = ab_w_in[i]
    half = QK_ROPE // 2
    zq = jnp.zeros((D_MODEL, QK_NOPE), F32)
    kr1 = w_in[:, 384:384 + half]
    kr2 = w_in[:, 384 + half:416]
    gate_order = jnp.array([0, 1, 2, 3, 8, 9, 10, 11, 4, 5, 6, 7, 12, 13, 14, 15])
    gates = jnp.pad(w_in[:, 2464:2480][:, gate_order], ((0, 0), (0, LANES - 2 * N_STREAMS)))
    w_big = jnp.concatenate([
        w_in[:, 0:384],
        _head_block(zq, kr1, kr2),
        _head_block(zq, -kr2, kr1),
        gates,
        w_in[:, 416:2464],
    ], axis=1).astype(BF16)
    w_uq = mla_w_uq[i].reshape(Q_LORA, MLA_HEADS, QK_NOPE + QK_ROPE)
    zn = jnp.zeros((Q_LORA, QK_NOPE), F32)
    wqa = jnp.concatenate([_head_block(w_uq[:, h, :QK_NOPE], w_uq[:, h, QK_NOPE:QK_NOPE + half],
                                       w_uq[:, h, QK_NOPE + half:]) for h in range(MLA_HEADS)], axis=1)
    wqb = jnp.concatenate([_head_block(zn, -w_uq[:, h, QK_NOPE + half:], w_uq[:, h, QK_NOPE:QK_NOPE + half])
                           for h in range(MLA_HEADS)], axis=1)
    w_ukv = mla_w_ukv[i].reshape(KV_LORA, MLA_HEADS, QK_NOPE + V_HEAD)
    wka = jnp.pad(w_ukv[:, :, :QK_NOPE], ((0, 0), (0, 0), (0, LANES - QK_NOPE))).reshape(KV_LORA, MLA_HEADS * LANES)
    zv = jnp.zeros((KV_LORA, V_HEAD), F32)
    wv = jnp.concatenate([jnp.concatenate([w_ukv[:, h, QK_NOPE:], zv] if h % 2 == 0 else [zv, w_ukv[:, h, QK_NOPE:]],
                                          axis=1) for h in range(MLA_HEADS)], axis=1)
    one_even = jnp.zeros((LANES,), F32).at[V_HEAD].set(1.0)
    one_odd = jnp.zeros((LANES,), F32).at[0].set(1.0)
    vone = jnp.concatenate([one_even if h % 2 == 0 else one_odd for h in range(MLA_HEADS)])[None, :]
    w_out = ab_w_out[i].astype(BF16)
    return dict(
        nrm=ab_norm[i][None, :], w_big=w_big, qn=mla_q_norm[i][None, :], wqa=wqa.astype(BF16),
        wqb=wqb.astype(BF16), kvn=mla_kv_norm[i][None, :], wka=wka.astype(BF16), wv=wv.astype(BF16),
        gb=jnp.broadcast_to(mlstm_gate_bias[i][gate_order][:, None], (2 * N_STREAMS, LANES)), vone=vone,
        conv_w=mlstm_conv_w[i], conv_b=mlstm_conv_b[i][None, :], on=mlstm_out_norm[i][None, :],
        wa=w_out[:MLA_HEADS * V_HEAD], wm=w_out[MLA_HEADS * V_HEAD:])


def _pick(pref, seq):
    return min(pref, seq)


def _forward(x3, even, odd, ffn, final_norm):
    batch, seq, _ = x3.shape
    x = x3.reshape(batch * seq, D_MODEL)
    tm = _pick(512, seq)
    cos_t, sin_t = _rope_tables(seq)
    e = even
    q, k, v, mq, mk, mv, mo, g = _ab_in(x, cos_t, sin_t, e, seq=seq, tm=tm)
    ya = _attn(q, k, v, batch=batch, seq=seq, tq=_pick(SCORE_ELEMS // seq, seq))
    c_rows, dec, cols = _gates(g, batch=batch, seq=seq)
    hf = _mlstm(mq, mk, mv, c_rows, dec, cols, batch=batch, seq=seq, direction=0)
    hb = _mlstm(mq, mk, mv, c_rows, dec, cols, batch=batch, seq=seq, direction=1)
    x = _ab_out(x, ya, hf, hb, mo, e["on"], e["wa"], e["wm"], tm=tm)
    f = ffn[0]
    x = _ffn(x, f["nrm"], f["wup"], f["cw"], f["cb"], f["wdn"], final_norm, seq=seq, tm=_pick(1024, seq), final=False)
    o = odd
    x = _mix_c(x, o["nrm"], o["win"], o["vn"], o["ws"], o["bs"], o["wout"], tm=tm)
    f = ffn[1]
    x = _ffn(x, f["nrm"], f["wup"], f["cw"], f["cb"], f["wdn"], final_norm, seq=seq, tm=_pick(1024, seq), final=True)
    return x.reshape(batch, seq, D_MODEL)


def kernel(x_prompt, x_sample, ab_norm, ab_w_in, mla_q_norm, mla_w_uq, mla_kv_norm, mla_w_ukv, mlstm_conv_w,
           mlstm_conv_b, mlstm_gate_bias, mlstm_out_norm, ab_w_out, c_norm, c_w_in, c_v_norm, c_w_spatial,
           c_b_spatial, c_w_out, ffn_norm, ffn_w_up, ffn_conv_w, ffn_conv_b, ffn_w_down, final_norm):
    even = _prep_even(0, ab_norm, ab_w_in, mla_q_norm, mla_w_uq, mla_kv_norm, mla_w_ukv, mlstm_conv_w, mlstm_conv_b,
                      mlstm_gate_bias, mlstm_out_norm, ab_w_out)
    odd = dict(nrm=c_norm[0][None, :], win=c_w_in[0].astype(BF16), vn=c_v_norm[0][None, :],
               ws=c_w_spatial[0].astype(BF16), bs=c_b_spatial[0].T, wout=c_w_out[0].astype(BF16))
    ffn = [dict(nrm=ffn_norm[l][None, :], wup=ffn_w_up[l].astype(BF16), cw=ffn_conv_w[l], cb=ffn_conv_b[l][None, :],
                wdn=ffn_w_down[l].astype(BF16)) for l in range(2)]
    fin = final_norm[None, :]
    return (_forward(x_prompt, even, odd, ffn, fin), _forward(x_sample, even, odd, ffn, fin))
```

```python
import functools

import jax
import jax.numpy as jnp
from jax import lax
from jax.experimental import pallas as pl
from jax.experimental.pallas import tpu as pltpu

F32 = jnp.float32
BF16 = jnp.bfloat16

EPS = 1e-6
D_MODEL = 1024
MLA_HEADS = 8
Q_LORA = 256
KV_LORA = 128
QK_NOPE = 64
QK_ROPE = 32
V_HEAD = 64
ROPE_THETA = 10000.0
M_HEADS = 4
M_HEAD_DIM = 128
M_WIDTH = M_HEADS * M_HEAD_DIM
N_STREAMS = 2 * M_HEADS
CHUNK = 128
BLOCK_CHUNKS = 4
G_GROUPS = 8
D_FF = 2816
LANES = 128
SUBLANES = 8
HALO = SUBLANES
PAIR = 2 * SUBLANES
MXU_COLS = 256
ROW_PITCH = 2
FF_COLS = MXU_COLS
KEY_TILE = MXU_COLS
SCORE_ELEMS = 2 * 1024 * 1024
LOG2_E = 1.4426950408889634
VMEM_LIMIT = 56 * 1024 * 1024

_C_Q, _C_KV, _C_KRA, _C_KRB, _C_G, _C_MQK, _C_MV, _C_MO, _C_END = (
    0, 256, 384, 512, 640, 768, 1792, 2304, 2816)


def _rms(x, g):
    return x * lax.rsqrt(jnp.mean(x * x, axis=-1, keepdims=True) + EPS) * g


def _const_spec(shape):
    nd = len(shape)
    return pl.BlockSpec(shape, lambda *_: (0,) * nd, pipeline_mode=pl.Buffered(1))


def _params(*sem):
    return pltpu.CompilerParams(dimension_semantics=sem, vmem_limit_bytes=VMEM_LIMIT)


def _dot(a, b):
    return jnp.dot(a, b, preferred_element_type=F32)


def _dot_nt(a, b):
    return lax.dot_general(a, b, (((1,), (1,)), ((), ())), preferred_element_type=F32)


def _halo_specs(tile_rows, n_rows, width, halo=HALO):
    tb = tile_rows // halo
    last = n_rows // halo - 1
    return (pl.BlockSpec((halo, width), lambda i: (jnp.maximum(i * tb - 1, 0), 0)),
            pl.BlockSpec((halo, width), lambda i: (jnp.minimum((i + 1) * tb, last), 0)))


def _conv3(y, u, rows, w_ref, b_ref, c0):
    tall = rows + 2 * HALO
    outs = []
    for s in range(y.shape[1] // LANES):
        lanes = slice(s * LANES, (s + 1) * LANES)
        cols = slice(c0 + s * LANES, c0 + (s + 1) * LANES)
        u[s, pl.ds(0, tall, stride=ROW_PITCH), :] = y[:, lanes]
        prv = u[s, pl.ds(ROW_PITCH * (HALO - 1), rows, stride=ROW_PITCH), :]
        nxt = u[s, pl.ds(ROW_PITCH * (HALO + 1), rows, stride=ROW_PITCH), :]
        outs.append(prv * w_ref[0:1, cols] + y[HALO:HALO + rows, lanes] * w_ref[1:2, cols]
                    + nxt * w_ref[2:3, cols] + b_ref[:, cols])
    return jnp.concatenate(outs, axis=1)


def _ab_in_kernel(x_ref, xp_ref, xn_ref, cos_ref, sin_ref, nrm_ref, w_ref, qn_ref, wqa_ref, wqb_ref, kvn_ref,
                  wka_ref, wv_ref, gb_ref, vone_ref, cw_ref, cb_ref,
                  q_ref, k_ref, v_ref, mq_ref, mk_ref, mv_ref, mo_ref, g_ref, us, *, tm, tps):
    jj = pl.program_id(0) % tps
    xa = jnp.concatenate([
        jnp.where(jj == 0, 0.0, xp_ref[...]),
        x_ref[...],
        jnp.where(jj == tps - 1, 0.0, xn_ref[...]),
    ], axis=0)
    xe = _rms(xa, nrm_ref[...]).astype(BF16)
    xn = xe[HALO:HALO + tm]

    def proj(a, b):
        return _dot(xn, w_ref[:, a:b])

    cq = proj(_C_Q, _C_KV)
    ckv_kra = proj(_C_KV, _C_KRB)
    krb_g = proj(_C_KRB, _C_MQK)

    nq_blocks = M_WIDTH // MXU_COLS
    for cb in range(2 * nq_blocks):
        cols = slice(cb * MXU_COLS, (cb + 1) * MXU_COLS)
        y = _conv3(_dot(xe, w_ref[:, _C_MQK + cb * MXU_COLS:_C_MQK + (cb + 1) * MXU_COLS]), us.at[cb % 2], tm,
                   cw_ref, cb_ref, cb * MXU_COLS)
        y = y * jax.nn.sigmoid(y)
        if cb < nq_blocks:
            mq_ref[:, cols] = y.astype(BF16)
        else:
            mk_ref[:, (cb - nq_blocks) * MXU_COLS:(cb - nq_blocks + 1) * MXU_COLS] = (
                y * (M_HEAD_DIM ** -0.5)).astype(BF16)
    mv_ref[...] = proj(_C_MV, _C_MO).astype(BF16)
    mo_ref[...] = proj(_C_MO, _C_END)
    gt = krb_g[:, LANES:2 * LANES].T
    for c in range(tm // CHUNK):
        g_ref[c] = gt[0:2 * N_STREAMS, c * CHUNK:(c + 1) * CHUNK] + gb_ref[:, 0:1]

    cos = cos_ref[...]
    sin = sin_ref[...]
    scale = (QK_NOPE + QK_ROPE) ** -0.5 * LOG2_E
    cqn = _rms(cq, qn_ref[...]).astype(BF16)
    qa = _dot(cqn, wqa_ref[...])
    qb = _dot(cqn, wqb_ref[...])
    cos_q = cos * scale
    sin_q = sin * scale
    for h in range(MLA_HEADS):
        blk = slice(h * LANES, (h + 1) * LANES)
        q_ref[:, blk] = (qa[:, blk] * cos_q + qb[:, blk] * sin_q).astype(BF16)

    ckvn = _rms(ckv_kra[:, 0:LANES], kvn_ref[...]).astype(BF16)
    ka = _dot(ckvn, wka_ref[...])
    kr = ckv_kra[:, LANES:2 * LANES] * cos + krb_g[:, 0:LANES] * sin
    for h in range(MLA_HEADS):
        blk = slice(h * LANES, (h + 1) * LANES)
        k_ref[:, blk] = (ka[:, blk] + kr).astype(BF16)
    v_ref[...] = (_dot(ckvn, wv_ref[...]) + vone_ref[...]).astype(BF16)


def _ab_in(x, cos_t, sin_t, e, *, seq, tm):
    n = x.shape[0]
    nt = n // tm
    tps = seq // tm
    row = lambda i: (i, 0)
    tab = lambda i: (i % tps, 0)
    consts = [e[k] for k in ("nrm", "w_big", "qn", "wqa", "wqb", "kvn", "wka", "wv", "gb", "vone", "conv_w",
                             "conv_b")]
    out_shape = (
        jax.ShapeDtypeStruct((n, MLA_HEADS * LANES), BF16),
        jax.ShapeDtypeStruct((n, MLA_HEADS * LANES), BF16),
        jax.ShapeDtypeStruct((n, MLA_HEADS * LANES), BF16),
        jax.ShapeDtypeStruct((n, M_WIDTH), BF16),
        jax.ShapeDtypeStruct((n, M_WIDTH), BF16),
        jax.ShapeDtypeStruct((n, M_WIDTH), BF16),
        jax.ShapeDtypeStruct((n, M_WIDTH), F32),
        jax.ShapeDtypeStruct((n // CHUNK, 2 * N_STREAMS, CHUNK), F32),
    )
    wide = pl.BlockSpec((tm, MLA_HEADS * LANES), row)
    narrow = pl.BlockSpec((tm, M_WIDTH), row)
    return pl.pallas_call(
        functools.partial(_ab_in_kernel, tm=tm, tps=tps),
        grid=(nt,),
        in_specs=[pl.BlockSpec((tm, D_MODEL), row), *_halo_specs(tm, n, D_MODEL),
                  pl.BlockSpec((tm, LANES), tab), pl.BlockSpec((tm, LANES), tab)]
                 + [_const_spec(c.shape) for c in consts],
        out_specs=(wide, wide, wide, narrow, narrow, narrow, narrow,
                   pl.BlockSpec((tm // CHUNK, 2 * N_STREAMS, CHUNK), lambda i: (i, 0, 0))),
        out_shape=out_shape,
        scratch_shapes=[pltpu.VMEM((2, MXU_COLS // LANES, ROW_PITCH * (tm + 2 * HALO), LANES), F32)],
        compiler_params=_params("parallel"),
        name="ab_in",
    )(x, x, x, cos_t, sin_t, *consts)


def _attn_kernel(q_ref, k_ref, v_ref, o_ref, s_scr, *, tq):
    nt = k_ref.shape[0] // KEY_TILE
    mbs = []
    for h in range(2):
        hc = slice(h * LANES, (h + 1) * LANES)
        q = q_ref[:, hc]
        mx = jnp.full((tq, LANES), -jnp.inf, F32)
        for t in range(nt):
            keys = slice(t * KEY_TILE, (t + 1) * KEY_TILE)
            s = _dot_nt(q, k_ref[keys, hc])
            s_scr[h, :, keys] = s
            for j in range(KEY_TILE // LANES):
                mx = jnp.maximum(mx, s[:, j * LANES:(j + 1) * LANES])
        mbs.append(jnp.broadcast_to(jnp.max(mx, axis=-1, keepdims=True), (tq, LANES)))
    outs = []
    for h in range(2):
        hc = slice(h * LANES, (h + 1) * LANES)
        mb = jnp.concatenate([mbs[h]] * (KEY_TILE // LANES), axis=1)
        acc = jnp.zeros((tq, LANES), F32)
        for t in range(nt):
            keys = slice(t * KEY_TILE, (t + 1) * KEY_TILE)
            p = jnp.exp2(s_scr[h, :, keys] - mb).astype(BF16)
            acc = acc + _dot(p, v_ref[keys, hc])
        ones_lane = V_HEAD if h == 0 else 0
        outs.append(acc / acc[:, ones_lane:ones_lane + 1])
    lane = lax.broadcasted_iota(jnp.int32, (tq, LANES), 1)
    o_ref[...] = jnp.where(lane < V_HEAD, outs[0], outs[1]).astype(BF16)


def _attn(q, k, v, *, batch, seq, tq):
    n = q.shape[0]
    nq = seq // tq
    pairs = MLA_HEADS // 2
    return pl.pallas_call(
        functools.partial(_attn_kernel, tq=tq),
        grid=(batch, pairs, nq),
        in_specs=[
            pl.BlockSpec((tq, 2 * LANES), lambda b, p, i: (b * nq + i, p)),
            pl.BlockSpec((seq, 2 * LANES), lambda b, p, i: (b, p)),
            pl.BlockSpec((seq, 2 * LANES), lambda b, p, i: (b, p)),
        ],
        out_specs=pl.BlockSpec((tq, LANES), lambda b, p, i: (b * nq + i, p)),
        out_shape=jax.ShapeDtypeStruct((n, MLA_HEADS * V_HEAD), BF16),
        scratch_shapes=[pltpu.VMEM((2, tq, seq), F32)],
        compiler_params=_params("parallel", "parallel", "arbitrary"),
        name="attn",
    )(q, k, v)


def _log_sigmoid(x):
    return jnp.minimum(x, 0.0) - jnp.log1p(jnp.exp(-jnp.abs(x)))


def _gates_kernel(g_ref, c_ref, dec_ref, cols_ref, tot_s, mloc_s, mpf_s, mpb_s, *, nc):
    rows = nc * N_STREAMS
    shape = (rows, CHUNK)
    three = (nc, N_STREAMS, CHUNK)
    li = g_ref[:, 0:N_STREAMS, :].reshape(shape)
    lf = _log_sigmoid(g_ref[:, N_STREAMS:2 * N_STREAMS, :].reshape(shape))
    fwd = (lax.broadcasted_iota(jnp.int32, shape, 0) & M_HEADS) == 0
    lane = lax.broadcasted_iota(jnp.int32, shape, 1)
    shifts = (1, 2, 4, 8, 16, 32, 64)

    pre = lf
    for sh in shifts:
        pre = pre + jnp.where(lane >= sh, pltpu.roll(pre, sh, axis=1), 0.0)
    tot = jnp.broadcast_to(pre[:, CHUNK - 1:CHUNK], shape)
    b = jnp.where(fwd, pre, tot - pre + lf)
    c = li - b
    run_f = c
    run_b = c
    for sh in shifts:
        run_f = jnp.maximum(run_f, jnp.where(lane >= sh, pltpu.roll(run_f, sh, axis=1), -jnp.inf))
        run_b = jnp.maximum(run_b, jnp.where(lane < CHUNK - sh, pltpu.roll(run_b, CHUNK - sh, axis=1), -jnp.inf))
    run = jnp.where(fwd, run_f, run_b)
    m_loc = tot + jnp.broadcast_to(jnp.max(c, axis=1, keepdims=True), shape)

    tot_s[...] = tot.reshape(three)
    mloc_s[...] = m_loc.reshape(three)
    fwd8 = lax.broadcasted_iota(jnp.int32, (N_STREAMS, CHUNK), 0) < M_HEADS

    def step(i, m):
        jb = nc - 1 - i
        mpf_s[i] = m
        mpb_s[jb] = m
        return jnp.maximum(jnp.where(fwd8, tot_s[i], tot_s[jb]) + m, jnp.where(fwd8, mloc_s[i], mloc_s[jb]))

    lax.fori_loop(0, nc, step, jnp.zeros((N_STREAMS, CHUNK), F32))
    m_prev = jnp.where(fwd, mpf_s[...].reshape(shape), mpb_s[...].reshape(shape))

    m_run = jnp.maximum(m_prev, run)
    floor = jnp.exp(-(b + m_run))
    w = jnp.exp(tot + c - m_loc)
    m_new = jnp.maximum(tot + m_prev, m_loc)
    c_ref[:, 0:N_STREAMS, :] = c.reshape(three)
    c_ref[:, N_STREAMS:2 * N_STREAMS, :] = w.reshape(three)
    c_ref[:, 2 * N_STREAMS:3 * N_STREAMS, :] = m_prev.reshape(three)
    dec_ref[:, 0:N_STREAMS, :] = jnp.exp(tot + m_prev - m_new).reshape(three)
    dec_ref[:, N_STREAMS:2 * N_STREAMS, :] = jnp.exp(m_loc - m_new).reshape(three)
    quantities = [a.reshape(three) for a in (m_run, floor)]
    pad = jnp.zeros((CHUNK - BLOCK_CHUNKS * 2 * N_STREAMS, CHUNK), F32)
    for blk in range(nc // BLOCK_CHUNKS):
        tile = jnp.concatenate(
            [a[blk * BLOCK_CHUNKS + cq] for cq in range(BLOCK_CHUNKS) for a in quantities] + [pad], axis=0)
        cols_ref[blk] = tile.T


def _gates(g, *, batch, seq):
    nc = seq // CHUNK
    nchunks = g.shape[0]
    three = (nc, N_STREAMS, CHUNK)
    return pl.pallas_call(
        functools.partial(_gates_kernel, nc=nc),
        grid=(batch,),
        in_specs=[pl.BlockSpec((nc, 2 * N_STREAMS, CHUNK), lambda b: (b, 0, 0))],
        out_specs=(
            pl.BlockSpec((nc, 3 * N_STREAMS, CHUNK), lambda b: (b, 0, 0)),
            pl.BlockSpec((nc, 2 * N_STREAMS, CHUNK), lambda b: (b, 0, 0)),
            pl.BlockSpec((nc // BLOCK_CHUNKS, CHUNK, LANES), lambda b: (b, 0, 0)),
        ),
        out_shape=(
            jax.ShapeDtypeStruct((nchunks, 3 * N_STREAMS, CHUNK), F32),
            jax.ShapeDtypeStruct((nchunks, 2 * N_STREAMS, CHUNK), F32),
            jax.ShapeDtypeStruct((nchunks // BLOCK_CHUNKS, CHUNK, LANES), F32),
        ),
        scratch_shapes=[pltpu.VMEM(three, F32)] * 4,
        compiler_params=_params("parallel"),
        name="gates",
    )(g)


def _mlstm_kernel(q_ref, k_ref, v_ref, c_ref, dec_ref, cols_ref, h_ref, st, bc, *, direction):
    @pl.when(pl.program_id(1) == 0)
    def _():
        st[...] = jnp.zeros_like(st)

    row = lax.broadcasted_iota(jnp.int32, (CHUNK, CHUNK), 0)
    col = lax.broadcasted_iota(jnp.int32, (CHUNK, CHUNK), 1)
    mask = (col <= row) if direction == 0 else (col >= row)
    ones = (col == 0).astype(BF16)
    cols_t = cols_ref[0]
    order = range(BLOCK_CHUNKS) if direction == 0 else range(BLOCK_CHUNKS - 1, -1, -1)
    for cq in range(BLOCK_CHUNKS):
        for hh in range(M_HEADS):
            base = cq * 2 * N_STREAMS + direction * M_HEADS + hh
            for qi in range(2):
                lane_i = base + qi * N_STREAMS
                bc[(qi * BLOCK_CHUNKS + cq) * M_HEADS + hh] = jnp.broadcast_to(
                    cols_t[:, lane_i:lane_i + 1], (CHUNK, LANES))
    heads = range(M_HEADS)
    hcs = [slice(hh * LANES, (hh + 1) * LANES) for hh in heads]
    qs, v1s, qk, c_loc = {}, {}, {}, {}
    for cq in order:
        rows = slice(cq * CHUNK, (cq + 1) * CHUNK)
        for hh in heads:
            qs[cq, hh] = q_ref[rows, hcs[hh]]
            qk[cq, hh] = _dot_nt(qs[cq, hh], k_ref[rows, hcs[hh]])
    for cq in order:
        rows = slice(cq * CHUNK, (cq + 1) * CHUNK)
        for hh in heads:
            s = direction * M_HEADS + hh
            v1s[cq, hh] = jnp.concatenate([v_ref[rows, hcs[hh]], ones], axis=1)
            kw_t = (k_ref[rows, hcs[hh]].astype(F32).T
                    * c_ref[cq, N_STREAMS + s:N_STREAMS + s + 1, :]).astype(BF16)
            c_loc[cq, hh] = _dot(kw_t, v1s[cq, hh])
    for cq in order:
        rows = slice(cq * CHUNK, (cq + 1) * CHUNK)
        for hh in heads:
            s = direction * M_HEADS + hh
            state = st[hh]
            m_run = bc[cq * M_HEADS + hh]
            floor = bc[(BLOCK_CHUNKS + cq) * M_HEADS + hh]
            c_ext = jnp.concatenate([c_ref[cq, s:s + 1, :], c_ref[cq, 2 * N_STREAMS + s:2 * N_STREAMS + s + 1, :]],
                                    axis=1)
            d_ext = jnp.exp(c_ext - jnp.concatenate([m_run, m_run], axis=1))
            p_ext = jnp.concatenate([qk[cq, hh] * jnp.where(mask, d_ext[:, 0:LANES], 0.0),
                                     qs[cq, hh].astype(F32) * d_ext[:, LANES:2 * LANES]], axis=1).astype(BF16)
            nd = _dot(p_ext, jnp.concatenate([v1s[cq, hh], state.astype(BF16)], axis=0))
            den = nd[:, LANES:LANES + 1]
            h_ref[rows, hcs[hh]] = nd[:, 0:LANES] / jnp.maximum(jnp.abs(den), floor)
            st[hh] = (dec_ref[cq, s:s + 1, 0:1] * state
                      + dec_ref[cq, N_STREAMS + s:N_STREAMS + s + 1, 0:1] * c_loc[cq, hh])


def _mlstm(mq, mk, mv, c_rows, dec, cols, *, batch, seq, direction):
    n = mq.shape[0]
    ts = BLOCK_CHUNKS * CHUNK
    nblk = seq // ts

    def blk(b, i):
        return b * nblk + (i if direction == 0 else nblk - 1 - i)

    tok = pl.BlockSpec((ts, M_WIDTH), lambda b, i: (blk(b, i), 0))
    return pl.pallas_call(
        functools.partial(_mlstm_kernel, direction=direction),
        grid=(batch, nblk),
        in_specs=[
            tok, tok, tok,
            pl.BlockSpec((BLOCK_CHUNKS, 3 * N_STREAMS, CHUNK), lambda b, i: (blk(b, i), 0, 0)),
            pl.BlockSpec((BLOCK_CHUNKS, 2 * N_STREAMS, CHUNK), lambda b, i: (blk(b, i), 0, 0)),
            pl.BlockSpec((1, CHUNK, LANES), lambda b, i: (blk(b, i), 0, 0)),
        ],
        out_specs=tok,
        out_shape=jax.ShapeDtypeStruct((n, M_WIDTH), F32),
        scratch_shapes=[pltpu.VMEM((M_HEADS, M_HEAD_DIM, 2 * LANES), F32),
                        pltpu.VMEM((2 * BLOCK_CHUNKS * M_HEADS, CHUNK, LANES), F32)],
        compiler_params=_params("parallel", "arbitrary"),
        name="mlstm_fwd" if direction == 0 else "mlstm_bwd",
    )(mq, mk, mv, c_rows, dec, cols)


def _ffn_body(xa, nrm_ref, wup_ref, cw_ref, cb_ref, wdn_ref, fin_ref, us, act, *, tm, final):
    xn = _rms(xa, nrm_ref[...]).astype(BF16)
    for c in range(D_FF // FF_COLS):
        g0 = c * FF_COLS
        v0 = D_FF + c * FF_COLS
        gate = _conv3(_dot(xn, wup_ref[:, g0:g0 + FF_COLS]), us.at[0], tm, cw_ref, cb_ref, g0)
        val = _conv3(_dot(xn, wup_ref[:, v0:v0 + FF_COLS]), us.at[1], tm, cw_ref, cb_ref, v0)
        act[:, g0:g0 + FF_COLS] = (gate * jax.nn.sigmoid(gate) * val).astype(BF16)
    y = xa[HALO:HALO + tm] + _dot(act[...], wdn_ref[...])
    if final:
        y = _rms(y, fin_ref[...])
    return y


def _ffn_kernel(x_ref, xp_ref, xn_ref, nrm_ref, wup_ref, cw_ref, cb_ref, wdn_ref, fin_ref, o_ref, us, act,
                *, tm, tps, final):
    jj = pl.program_id(0) % tps
    xa = jnp.concatenate([
        jnp.where(jj == 0, 0.0, xp_ref[...]),
        x_ref[...],
        jnp.where(jj == tps - 1, 0.0, xn_ref[...]),
    ], axis=0)
    o_ref[...] = _ffn_body(xa, nrm_ref, wup_ref, cw_ref, cb_ref, wdn_ref, fin_ref, us, act, tm=tm, final=final)


def _ffn_scratch(tm):
    return [pltpu.VMEM((2, FF_COLS // LANES, ROW_PITCH * (tm + 2 * HALO), LANES), F32),
            pltpu.VMEM((tm, D_FF), BF16)]


def _ab_out_ffn_kernel(x_ref, xp_ref, xn_ref, ya_ref, yap_ref, yan_ref, hf_ref, hfp_ref, hfn_ref, hb_ref, hbp_ref,
                       hbn_ref, mo_ref, mop_ref, mon_ref, on_ref, wa_ref, wm_ref, nrm_ref, wup_ref, cw_ref, cb_ref,
                       wdn_ref, fin_ref, o_ref, us, act, *, tm, tps):
    jj = pl.program_id(0) % tps

    def rows(main, prev, nxt):
        return jnp.concatenate([main[...], prev[...], nxt[...]], axis=0)

    hf, hb, mo = rows(hf_ref, hfp_ref, hfn_ref), rows(hb_ref, hbp_ref, hbn_ref), rows(mo_ref, mop_ref, mon_ref)
    parts = []
    for hh in range(M_HEADS):
        hc = slice(hh * LANES, (hh + 1) * LANES)
        t = hf[:, hc] + hb[:, hc]
        t = t * lax.rsqrt(jnp.mean(t * t, axis=-1, keepdims=True) + EPS) * on_ref[:, hc]
        parts.append((jax.nn.sigmoid(mo[:, hc]) * t).astype(BF16))
    ym = jnp.concatenate(parts, axis=1)
    x1 = (rows(x_ref, xp_ref, xn_ref) + _dot(rows(ya_ref, yap_ref, yan_ref), wa_ref[...]) + _dot(ym, wm_ref[...]))
    xa = jnp.concatenate([
        jnp.where(jj == 0, 0.0, x1[tm + PAIR - HALO:tm + PAIR]),
        x1[0:tm],
        jnp.where(jj == tps - 1, 0.0, x1[tm + PAIR:tm + PAIR + HALO]),
    ], axis=0)
    o_ref[...] = _ffn_body(xa, nrm_ref, wup_ref, cw_ref, cb_ref, wdn_ref, fin_ref, us, act, tm=tm, final=False)


def _ab_out_ffn(x, ya, hf, hb, mo, on, wa, wm, nrm, wup, cw, cb, wdn, fin, *, seq, tm):
    n = x.shape[0]
    row = lambda i: (i, 0)

    def with_halo(width):
        return [pl.BlockSpec((tm, width), row), *_halo_specs(tm, n, width, PAIR)]

    consts = [on, wa, wm, nrm, wup, cw, cb, wdn, fin]
    return pl.pallas_call(
        functools.partial(_ab_out_ffn_kernel, tm=tm, tps=seq // tm),
        grid=(n // tm,),
        in_specs=with_halo(D_MODEL) + 4 * with_halo(M_WIDTH) + [_const_spec(c.shape) for c in consts],
        out_specs=pl.BlockSpec((tm, D_MODEL), row),
        out_shape=jax.ShapeDtypeStruct((n, D_MODEL), F32),
        scratch_shapes=_ffn_scratch(tm),
        compiler_params=_params("parallel"),
        name="ab_out_ffn",
    )(x, x, x, ya, ya, ya, hf, hf, hf, hb, hb, hb, mo, mo, mo, *consts)


def _ffn(x, nrm, wup, cw, cb, wdn, fin, *, seq, tm, final):
    n = x.shape[0]
    row = lambda i: (i, 0)
    return pl.pallas_call(
        functools.partial(_ffn_kernel, tm=tm, tps=seq // tm, final=final),
        grid=(n // tm,),
        in_specs=[
            pl.BlockSpec((tm, D_MODEL), row), *_halo_specs(tm, n, D_MODEL),
            _const_spec(nrm.shape), _const_spec(wup.shape), _const_spec(cw.shape), _const_spec(cb.shape),
            _const_spec(wdn.shape), _const_spec(fin.shape),
        ],
        out_specs=pl.BlockSpec((tm, D_MODEL), row),
        out_shape=jax.ShapeDtypeStruct((n, D_MODEL), F32),
        scratch_shapes=_ffn_scratch(tm),
        compiler_params=_params("parallel"),
        name="ffn_final" if final else "ffn",
    )(x, x, x, nrm, wup, cw, cb, wdn, fin)


def _gelu(x):
    return 0.5 * x * (1.0 + lax.erf(x * (2.0 ** -0.5)))


def _mix_c_kernel(x_ref, nrm_ref, win_ref, vn_ref, ws_ref, bs_ref, wout_ref, o_ref, zs, *, tm):
    x = x_ref[...]
    xn = _rms(x, nrm_ref[...]).astype(BF16)
    v = _gelu(_dot(xn, win_ref[:, D_MODEL:2 * D_MODEL]))
    u = _gelu(_dot(xn, win_ref[:, 0:D_MODEL]))
    vn = _rms(v, vn_ref[...]).astype(BF16)
    nch = tm // CHUNK
    for g in range(G_GROUPS):
        cols = slice(g * LANES, (g + 1) * LANES)
        rhs = jnp.concatenate([vn[c * CHUNK:(c + 1) * CHUNK, cols] for c in range(nch)], axis=1)
        sv = _dot(ws_ref[g], rhs) + bs_ref[:, g:g + 1]
        for c in range(nch):
            zs[c * CHUNK:(c + 1) * CHUNK, cols] = (
                u[c * CHUNK:(c + 1) * CHUNK, cols] * sv[:, c * LANES:(c + 1) * LANES]).astype(BF16)
    o_ref[...] = x + _dot(zs[...], wout_ref[...])


def _mix_c(x, nrm, win, vn, ws, bs, wout, *, tm):
    n = x.shape[0]
    row = lambda i: (i, 0)
    return pl.pallas_call(
        functools.partial(_mix_c_kernel, tm=tm),
        grid=(n // tm,),
        in_specs=[
            pl.BlockSpec((tm, D_MODEL), row),
            _const_spec(nrm.shape), _const_spec(win.shape), _const_spec(vn.shape), _const_spec(ws.shape),
            _const_spec(bs.shape), _const_spec(wout.shape),
        ],
        out_specs=pl.BlockSpec((tm, D_MODEL), row),
        out_shape=jax.ShapeDtypeStruct((n, D_MODEL), F32),
        scratch_shapes=[pltpu.VMEM((tm, D_MODEL), BF16)],
        compiler_params=_params("parallel"),
        name="mix_c",
    )(x, nrm, win, vn, ws, bs, wout)


def _rope_tables(seq):
    pos = jnp.arange(seq, dtype=F32)
    inv = 1.0 / (ROPE_THETA ** (jnp.arange(0, QK_ROPE, 2, dtype=F32) / QK_ROPE))
    ang = pos[:, None] * inv[None, :]
    cos, sin = jnp.cos(ang), jnp.sin(ang)
    one = jnp.ones((seq, QK_NOPE), F32)
    zero_n = jnp.zeros((seq, QK_NOPE), F32)
    pad = jnp.zeros((seq, LANES - QK_NOPE - QK_ROPE), F32)
    return (jnp.concatenate([one, cos, cos, pad], axis=1), jnp.concatenate([zero_n, sin, sin, pad], axis=1))


def _head_block(nope, x1, x2):
    pad = jnp.zeros((nope.shape[0], LANES - QK_NOPE - QK_ROPE), nope.dtype)
    return jnp.concatenate([nope, x1, x2, pad], axis=1)


def _prep_even(i, ab_norm, ab_w_in, mla_q_norm, mla_w_uq, mla_kv_norm, mla_w_ukv, mlstm_conv_w, mlstm_conv_b,
               mlstm_gate_bias, mlstm_out_norm, ab_w_out):
    w_in = ab_w_in[i]
    half = QK_ROPE // 2
    zq = jnp.zeros((D_MODEL, QK_NOPE), F32)
    kr1 = w_in[:, 384:384 + half]
    kr2 = w_in[:, 384 + half:416]
    gate_order = jnp.array([0, 1, 2, 3, 8, 9, 10, 11, 4, 5, 6, 7, 12, 13, 14, 15])
    gates = jnp.pad(w_in[:, 2464:2480][:, gate_order], ((0, 0), (0, LANES - 2 * N_STREAMS)))
    w_big = jnp.concatenate([
        w_in[:, 0:384],
        _head_block(zq, kr1, kr2),
        _head_block(zq, -kr2, kr1),
        gates,
        w_in[:, 416:2464],
    ], axis=1).astype(BF16)
    w_uq = mla_w_uq[i].reshape(Q_LORA, MLA_HEADS, QK_NOPE + QK_ROPE)
    zn = jnp.zeros((Q_LORA, QK_NOPE), F32)
    wqa = jnp.concatenate([_head_block(w_uq[:, h, :QK_NOPE], w_uq[:, h, QK_NOPE:QK_NOPE + half],
                                       w_uq[:, h, QK_NOPE + half:]) for h in range(MLA_HEADS)], axis=1)
    wqb = jnp.concatenate([_head_block(zn, -w_uq[:, h, QK_NOPE + half:], w_uq[:, h, QK_NOPE:QK_NOPE + half])
                           for h in range(MLA_HEADS)], axis=1)
    w_ukv = mla_w_ukv[i].reshape(KV_LORA, MLA_HEADS, QK_NOPE + V_HEAD)
    wka = jnp.pad(w_ukv[:, :, :QK_NOPE], ((0, 0), (0, 0), (0, LANES - QK_NOPE))).reshape(KV_LORA, MLA_HEADS * LANES)
    zv = jnp.zeros((KV_LORA, V_HEAD), F32)
    wv = jnp.concatenate([jnp.concatenate([w_ukv[:, h, QK_NOPE:], zv] if h % 2 == 0 else [zv, w_ukv[:, h, QK_NOPE:]],
                                          axis=1) for h in range(MLA_HEADS)], axis=1)
    one_even = jnp.zeros((LANES,), F32).at[V_HEAD].set(1.0)
    one_odd = jnp.zeros((LANES,), F32).at[0].set(1.0)
    vone = jnp.concatenate([one_even if h % 2 == 0 else one_odd for h in range(MLA_HEADS)])[None, :]
    w_out = ab_w_out[i].astype(BF16)
    return dict(
        nrm=ab_norm[i][None, :], w_big=w_big, qn=mla_q_norm[i][None, :], wqa=wqa.astype(BF16),
        wqb=wqb.astype(BF16), kvn=mla_kv_norm[i][None, :], wka=wka.astype(BF16), wv=wv.astype(BF16),
        gb=jnp.broadcast_to(mlstm_gate_bias[i][gate_order][:, None], (2 * N_STREAMS, LANES)), vone=vone,
        conv_w=mlstm_conv_w[i], conv_b=mlstm_conv_b[i][None, :], on=mlstm_out_norm[i][None, :],
        wa=w_out[:MLA_HEADS * V_HEAD], wm=w_out[MLA_HEADS * V_HEAD:])


def _pick(pref, seq):
    return min(pref, seq)


def _forward(x3, even, odd, ffn, final_norm):
    batch, seq, _ = x3.shape
    x = x3.reshape(batch * seq, D_MODEL)
    tm = _pick(512, seq)
    cos_t, sin_t = _rope_tables(seq)
    e = even
    q, k, v, mq, mk, mv, mo, g = _ab_in(x, cos_t, sin_t, e, seq=seq, tm=tm)
    ya = _attn(q, k, v, batch=batch, seq=seq, tq=_pick(SCORE_ELEMS // seq, seq))
    c_rows, dec, cols = _gates(g, batch=batch, seq=seq)
    hf = _mlstm(mq, mk, mv, c_rows, dec, cols, batch=batch, seq=seq, direction=0)
    hb = _mlstm(mq, mk, mv, c_rows, dec, cols, batch=batch, seq=seq, direction=1)
    f = ffn[0]
    x = _ab_out_ffn(x, ya, hf, hb, mo, e["on"], e["wa"], e["wm"], f["nrm"], f["wup"], f["cw"], f["cb"], f["wdn"],
                    final_norm, seq=seq, tm=tm)
    o = odd
    x = _mix_c(x, o["nrm"], o["win"], o["vn"], o["ws"], o["bs"], o["wout"], tm=tm)
    f = ffn[1]
    x = _ffn(x, f["nrm"], f["wup"], f["cw"], f["cb"], f["wdn"], final_norm, seq=seq, tm=tm, final=True)
    return x.reshape(batch, seq, D_MODEL)


def kernel(x_prompt, x_sample, ab_norm, ab_w_in, mla_q_norm, mla_w_uq, mla_kv_norm, mla_w_ukv, mlstm_conv_w,
           mlstm_conv_b, mlstm_gate_bias, mlstm_out_norm, ab_w_out, c_norm, c_w_in, c_v_norm, c_w_spatial,
           c_b_spatial, c_w_out, ffn_norm, ffn_w_up, ffn_conv_w, ffn_conv_b, ffn_w_down, final_norm):
    even = _prep_even(0, ab_norm, ab_w_in, mla_q_norm, mla_w_uq, mla_kv_norm, mla_w_ukv, mlstm_conv_w, mlstm_conv_b,
                      mlstm_gate_bias, mlstm_out_norm, ab_w_out)
    odd = dict(nrm=c_norm[0][None, :], win=c_w_in[0].astype(BF16), vn=c_v_norm[0][None, :],
               ws=c_w_spatial[0].astype(BF16), bs=c_b_spatial[0].T, wout=c_w_out[0].astype(BF16))
    ffn = [dict(nrm=ffn_norm[l][None, :], wup=ffn_w_up[l].astype(BF16), cw=ffn_conv_w[l], cb=ffn_conv_b[l][None, :],
                wdn=ffn_w_down[l].astype(BF16)) for l in range(2)]
    fin = final_norm[None, :]
    return (_forward(x_prompt, even, odd, ffn, fin), _forward(x_sample, even, odd, ffn, fin))
```

```python
import functools

import jax
import jax.numpy as jnp
from jax import lax
from jax.experimental import pallas as pl
from jax.experimental.pallas import tpu as pltpu

F32 = jnp.float32
BF16 = jnp.bfloat16

EPS = 1e-6
D_MODEL = 1024
MLA_HEADS = 8
Q_LORA = 256
KV_LORA = 128
QK_NOPE = 64
QK_ROPE = 32
V_HEAD = 64
ROPE_THETA = 10000.0
M_HEADS = 4
M_HEAD_DIM = 128
M_WIDTH = M_HEADS * M_HEAD_DIM
N_STREAMS = 2 * M_HEADS
CHUNK = 128
G_GROUPS = 8
D_FF = 2816
LANES = 128
SUBLANES = 8
HALO = SUBLANES
BLOCK_CHUNKS = 4
PAIR = 2 * SUBLANES
MXU_COLS = 256
ROW_PITCH = 2
FF_COLS = MXU_COLS
KEY_TILE = MXU_COLS
SCORE_ELEMS = 2 * 1024 * 1024
KV_BLOCK_ELEMS = 4 * 1024 * 1024
LOG2_E = 1.4426950408889634
VMEM_LIMIT = 56 * 1024 * 1024

_C_Q, _C_KV, _C_KRA, _C_KRB, _C_G, _C_MQK, _C_MV, _C_MO, _C_END = (
    0, 256, 384, 512, 640, 768, 1792, 2304, 2816)


def _rms(x, g):
    return x * lax.rsqrt(jnp.mean(x * x, axis=-1, keepdims=True) + EPS) * g


def _const_spec(shape):
    nd = len(shape)
    return pl.BlockSpec(shape, lambda *_: (0,) * nd, pipeline_mode=pl.Buffered(1))


def _params(*sem):
    return pltpu.CompilerParams(dimension_semantics=sem, vmem_limit_bytes=VMEM_LIMIT)


def _dot(a, b):
    return jnp.dot(a, b, preferred_element_type=F32)


def _dot_nt(a, b):
    return lax.dot_general(a, b, (((1,), (1,)), ((), ())), preferred_element_type=F32)


def _halo_specs(tile_rows, n_rows, width, halo=HALO):
    tb = tile_rows // halo
    last = n_rows // halo - 1
    return (pl.BlockSpec((halo, width), lambda i: (jnp.maximum(i * tb - 1, 0), 0)),
            pl.BlockSpec((halo, width), lambda i: (jnp.minimum((i + 1) * tb, last), 0)))


def _conv3(y, u, rows, w_ref, b_ref, c0):
    tall = rows + 2 * HALO
    outs = []
    for s in range(y.shape[1] // LANES):
        lanes = slice(s * LANES, (s + 1) * LANES)
        cols = slice(c0 + s * LANES, c0 + (s + 1) * LANES)
        u[s, pl.ds(0, tall, stride=ROW_PITCH), :] = y[:, lanes]
        prv = u[s, pl.ds(ROW_PITCH * (HALO - 1), rows, stride=ROW_PITCH), :]
        nxt = u[s, pl.ds(ROW_PITCH * (HALO + 1), rows, stride=ROW_PITCH), :]
        outs.append(prv * w_ref[0:1, cols] + y[HALO:HALO + rows, lanes] * w_ref[1:2, cols]
                    + nxt * w_ref[2:3, cols] + b_ref[:, cols])
    return jnp.concatenate(outs, axis=1)


def _ab_in_kernel(x_ref, xp_ref, xn_ref, cos_ref, sin_ref, nrm_ref, w_ref, qn_ref, wqa_ref, wqb_ref, kvn_ref,
                  wka_ref, wv_ref, gb_ref, vone_ref, cw_ref, cb_ref,
                  q_ref, k_ref, v_ref, mq_ref, mk_ref, mv_ref, mo_ref, g_ref, us, *, tm, tps):
    jj = pl.program_id(0) % tps
    xa = jnp.concatenate([
        jnp.where(jj == 0, 0.0, xp_ref[...]),
        x_ref[...],
        jnp.where(jj == tps - 1, 0.0, xn_ref[...]),
    ], axis=0)
    xe = _rms(xa, nrm_ref[...]).astype(BF16)
    xn = xe[HALO:HALO + tm]

    def proj(a, b):
        return _dot(xn, w_ref[:, a:b])

    cq = proj(_C_Q, _C_KV)
    ckv_kra = proj(_C_KV, _C_KRB)
    krb_g = proj(_C_KRB, _C_MQK)

    nq_blocks = M_WIDTH // MXU_COLS
    for cb in range(2 * nq_blocks):
        cols = slice(cb * MXU_COLS, (cb + 1) * MXU_COLS)
        y = _conv3(_dot(xe, w_ref[:, _C_MQK + cb * MXU_COLS:_C_MQK + (cb + 1) * MXU_COLS]), us.at[cb % 2], tm,
                   cw_ref, cb_ref, cb * MXU_COLS)
        y = y * jax.nn.sigmoid(y)
        if cb < nq_blocks:
            mq_ref[:, cols] = y.astype(BF16)
        else:
            mk_ref[:, (cb - nq_blocks) * MXU_COLS:(cb - nq_blocks + 1) * MXU_COLS] = (
                y * (M_HEAD_DIM ** -0.5)).astype(BF16)
    mv_ref[...] = proj(_C_MV, _C_MO).astype(BF16)
    mo_ref[...] = proj(_C_MO, _C_END)
    gt = krb_g[:, LANES:2 * LANES].T
    for c in range(tm // CHUNK):
        g_ref[c] = gt[0:2 * N_STREAMS, c * CHUNK:(c + 1) * CHUNK] + gb_ref[:, 0:1]

    cos = cos_ref[...]
    sin = sin_ref[...]
    scale = (QK_NOPE + QK_ROPE) ** -0.5 * LOG2_E
    cqn = _rms(cq, qn_ref[...]).astype(BF16)
    qa = _dot(cqn, wqa_ref[...])
    qb = _dot(cqn, wqb_ref[...])
    cos_q = cos * scale
    sin_q = sin * scale
    for h in range(MLA_HEADS):
        blk = slice(h * LANES, (h + 1) * LANES)
        q_ref[:, blk] = (qa[:, blk] * cos_q + qb[:, blk] * sin_q).astype(BF16)

    ckvn = _rms(ckv_kra[:, 0:LANES], kvn_ref[...]).astype(BF16)
    ka = _dot(ckvn, wka_ref[...])
    kr = ckv_kra[:, LANES:2 * LANES] * cos + krb_g[:, 0:LANES] * sin
    for h in range(MLA_HEADS):
        blk = slice(h * LANES, (h + 1) * LANES)
        k_ref[:, blk] = (ka[:, blk] + kr).astype(BF16)
    v_ref[...] = (_dot(ckvn, wv_ref[...]) + vone_ref[...]).astype(BF16)


def _ab_in(x, cos_t, sin_t, e, *, seq, tm):
    n = x.shape[0]
    nt = n // tm
    tps = seq // tm
    row = lambda i: (i, 0)
    tab = lambda i: (i % tps, 0)
    consts = [e[k] for k in ("nrm", "w_big", "qn", "wqa", "wqb", "kvn", "wka", "wv", "gb", "vone", "conv_w",
                             "conv_b")]
    out_shape = (
        jax.ShapeDtypeStruct((n, MLA_HEADS * LANES), BF16),
        jax.ShapeDtypeStruct((n, MLA_HEADS * LANES), BF16),
        jax.ShapeDtypeStruct((n, MLA_HEADS * LANES), BF16),
        jax.ShapeDtypeStruct((n, M_WIDTH), BF16),
        jax.ShapeDtypeStruct((n, M_WIDTH), BF16),
        jax.ShapeDtypeStruct((n, M_WIDTH), BF16),
        jax.ShapeDtypeStruct((n, M_WIDTH), F32),
        jax.ShapeDtypeStruct((n // CHUNK, 2 * N_STREAMS, CHUNK), F32),
    )
    wide = pl.BlockSpec((tm, MLA_HEADS * LANES), row)
    narrow = pl.BlockSpec((tm, M_WIDTH), row)
    return pl.pallas_call(
        functools.partial(_ab_in_kernel, tm=tm, tps=tps),
        grid=(nt,),
        in_specs=[pl.BlockSpec((tm, D_MODEL), row), *_halo_specs(tm, n, D_MODEL),
                  pl.BlockSpec((tm, LANES), tab), pl.BlockSpec((tm, LANES), tab)]
                 + [_const_spec(c.shape) for c in consts],
        out_specs=(wide, wide, wide, narrow, narrow, narrow, narrow,
                   pl.BlockSpec((tm // CHUNK, 2 * N_STREAMS, CHUNK), lambda i: (i, 0, 0))),
        out_shape=out_shape,
        scratch_shapes=[pltpu.VMEM((2, MXU_COLS // LANES, ROW_PITCH * (tm + 2 * HALO), LANES), F32)],
        compiler_params=_params("parallel"),
        name="ab_in",
    )(x, x, x, cos_t, sin_t, *consts)


def _attn_kernel(q_ref, k_ref, v_ref, o_ref, s_scr, *, tq, pairs):
    nt = k_ref.shape[0] // KEY_TILE
    lane = lax.broadcasted_iota(jnp.int32, (tq, LANES), 1)
    for pr in range(pairs):
        mbs = []
        for h in range(2):
            hc = slice((2 * pr + h) * LANES, (2 * pr + h + 1) * LANES)
            q = q_ref[:, hc]
            mx = jnp.full((tq, LANES), -jnp.inf, F32)
            for t in range(nt):
                keys = slice(t * KEY_TILE, (t + 1) * KEY_TILE)
                s = _dot_nt(q, k_ref[keys, hc])
                s_scr[h, :, keys] = s
                for j in range(KEY_TILE // LANES):
                    mx = jnp.maximum(mx, s[:, j * LANES:(j + 1) * LANES])
            mbs.append(jnp.broadcast_to(jnp.max(mx, axis=-1, keepdims=True), (tq, LANES)))
        outs = []
        for h in range(2):
            hc = slice((2 * pr + h) * LANES, (2 * pr + h + 1) * LANES)
            mb = jnp.concatenate([mbs[h]] * (KEY_TILE // LANES), axis=1)
            acc = jnp.zeros((tq, LANES), F32)
            for t in range(nt):
                keys = slice(t * KEY_TILE, (t + 1) * KEY_TILE)
                p = jnp.exp2(s_scr[h, :, keys] - mb).astype(BF16)
                acc = acc + _dot(p, v_ref[keys, hc])
            ones_lane = V_HEAD if h == 0 else 0
            outs.append(acc / acc[:, ones_lane:ones_lane + 1])
        o_ref[:, pr * LANES:(pr + 1) * LANES] = jnp.where(lane < V_HEAD, outs[0], outs[1]).astype(BF16)


def _attn(q, k, v, *, batch, seq, tq, pairs):
    n = q.shape[0]
    nq = seq // tq
    steps = MLA_HEADS // 2 // pairs
    return pl.pallas_call(
        functools.partial(_attn_kernel, tq=tq, pairs=pairs),
        grid=(batch, steps, nq),
        in_specs=[
            pl.BlockSpec((tq, 2 * pairs * LANES), lambda b, p, i: (b * nq + i, p)),
            pl.BlockSpec((seq, 2 * pairs * LANES), lambda b, p, i: (b, p)),
            pl.BlockSpec((seq, 2 * pairs * LANES), lambda b, p, i: (b, p)),
        ],
        out_specs=pl.BlockSpec((tq, pairs * LANES), lambda b, p, i: (b * nq + i, p)),
        out_shape=jax.ShapeDtypeStruct((n, MLA_HEADS * V_HEAD), BF16),
        scratch_shapes=[pltpu.VMEM((2, tq, seq), F32)],
        compiler_params=_params("parallel", "parallel", "arbitrary"),
        name="attn",
    )(q, k, v)


def _log_sigmoid(x):
    return jnp.minimum(x, 0.0) - jnp.log1p(jnp.exp(-jnp.abs(x)))


def _gates_kernel(g_ref, c_ref, dec_ref, cols_ref, tot_s, mloc_s, mpf_s, mpb_s, *, nc):
    rows = nc * N_STREAMS
    shape = (rows, CHUNK)
    three = (nc, N_STREAMS, CHUNK)
    li = g_ref[:, 0:N_STREAMS, :].reshape(shape)
    lf = _log_sigmoid(g_ref[:, N_STREAMS:2 * N_STREAMS, :].reshape(shape))
    fwd = (lax.broadcasted_iota(jnp.int32, shape, 0) & M_HEADS) == 0
    lane = lax.broadcasted_iota(jnp.int32, shape, 1)
    shifts = (1, 2, 4, 8, 16, 32, 64)

    pre = lf
    for sh in shifts:
        pre = pre + jnp.where(lane >= sh, pltpu.roll(pre, sh, axis=1), 0.0)
    tot = jnp.broadcast_to(pre[:, CHUNK - 1:CHUNK], shape)
    b = jnp.where(fwd, pre, tot - pre + lf)
    c = li - b
    run_f = c
    run_b = c
    for sh in shifts:
        run_f = jnp.maximum(run_f, jnp.where(lane >= sh, pltpu.roll(run_f, sh, axis=1), -jnp.inf))
        run_b = jnp.maximum(run_b, jnp.where(lane < CHUNK - sh, pltpu.roll(run_b, CHUNK - sh, axis=1), -jnp.inf))
    run = jnp.where(fwd, run_f, run_b)
    m_loc = tot + jnp.broadcast_to(jnp.max(c, axis=1, keepdims=True), shape)

    tot_s[...] = tot.reshape(three)
    mloc_s[...] = m_loc.reshape(three)
    fwd8 = lax.broadcasted_iota(jnp.int32, (N_STREAMS, CHUNK), 0) < M_HEADS

    def step(i, m):
        jb = nc - 1 - i
        mpf_s[i] = m
        mpb_s[jb] = m
        return jnp.maximum(jnp.where(fwd8, tot_s[i], tot_s[jb]) + m, jnp.where(fwd8, mloc_s[i], mloc_s[jb]))

    lax.fori_loop(0, nc, step, jnp.zeros((N_STREAMS, CHUNK), F32))
    m_prev = jnp.where(fwd, mpf_s[...].reshape(shape), mpb_s[...].reshape(shape))

    m_run = jnp.maximum(m_prev, run)
    floor = jnp.exp(-(b + m_run))
    w = jnp.exp(tot + c - m_loc)
    m_new = jnp.maximum(tot + m_prev, m_loc)
    c_ref[:, 0:N_STREAMS, :] = c.reshape(three)
    c_ref[:, N_STREAMS:2 * N_STREAMS, :] = w.reshape(three)
    c_ref[:, 2 * N_STREAMS:3 * N_STREAMS, :] = m_prev.reshape(three)
    dec_ref[:, 0:N_STREAMS, :] = jnp.exp(tot + m_prev - m_new).reshape(three)
    dec_ref[:, N_STREAMS:2 * N_STREAMS, :] = jnp.exp(m_loc - m_new).reshape(three)
    quantities = [a.reshape(three) for a in (m_run, floor)]
    pad = jnp.zeros((CHUNK - BLOCK_CHUNKS * 2 * N_STREAMS, CHUNK), F32)
    for blk in range(nc // BLOCK_CHUNKS):
        tile = jnp.concatenate(
            [a[blk * BLOCK_CHUNKS + cq] for cq in range(BLOCK_CHUNKS) for a in quantities] + [pad], axis=0)
        cols_ref[blk] = tile.T


def _gates(g, *, batch, seq):
    nc = seq // CHUNK
    nchunks = g.shape[0]
    three = (nc, N_STREAMS, CHUNK)
    return pl.pallas_call(
        functools.partial(_gates_kernel, nc=nc),
        grid=(batch,),
        in_specs=[pl.BlockSpec((nc, 2 * N_STREAMS, CHUNK), lambda b: (b, 0, 0))],
        out_specs=(
            pl.BlockSpec((nc, 3 * N_STREAMS, CHUNK), lambda b: (b, 0, 0)),
            pl.BlockSpec((nc, 2 * N_STREAMS, CHUNK), lambda b: (b, 0, 0)),
            pl.BlockSpec((nc // BLOCK_CHUNKS, CHUNK, LANES), lambda b: (b, 0, 0)),
        ),
        out_shape=(
            jax.ShapeDtypeStruct((nchunks, 3 * N_STREAMS, CHUNK), F32),
            jax.ShapeDtypeStruct((nchunks, 2 * N_STREAMS, CHUNK), F32),
            jax.ShapeDtypeStruct((nchunks // BLOCK_CHUNKS, CHUNK, LANES), F32),
        ),
        scratch_shapes=[pltpu.VMEM(three, F32)] * 4,
        compiler_params=_params("parallel"),
        name="gates",
    )(g)


def _mlstm_kernel(q_ref, k_ref, v_ref, c_ref, dec_ref, cols_ref, h_ref, st, bc, *, direction):
    @pl.when(pl.program_id(1) == 0)
    def _():
        st[...] = jnp.zeros_like(st)

    row = lax.broadcasted_iota(jnp.int32, (CHUNK, CHUNK), 0)
    col = lax.broadcasted_iota(jnp.int32, (CHUNK, CHUNK), 1)
    mask = (col <= row) if direction == 0 else (col >= row)
    ones = (col == 0).astype(BF16)
    cols_t = cols_ref[0]
    order = range(BLOCK_CHUNKS) if direction == 0 else range(BLOCK_CHUNKS - 1, -1, -1)
    for cq in range(BLOCK_CHUNKS):
        for hh in range(M_HEADS):
            base = cq * 2 * N_STREAMS + direction * M_HEADS + hh
            for qi in range(2):
                lane_i = base + qi * N_STREAMS
                bc[(qi * BLOCK_CHUNKS + cq) * M_HEADS + hh] = jnp.broadcast_to(
                    cols_t[:, lane_i:lane_i + 1], (CHUNK, LANES))
    heads = range(M_HEADS)
    hcs = [slice(hh * LANES, (hh + 1) * LANES) for hh in heads]
    qs, v1s, qk, c_loc = {}, {}, {}, {}
    for cq in order:
        rows = slice(cq * CHUNK, (cq + 1) * CHUNK)
        for hh in heads:
            qs[cq, hh] = q_ref[rows, hcs[hh]]
            qk[cq, hh] = _dot_nt(qs[cq, hh], k_ref[rows, hcs[hh]])
    for cq in order:
        rows = slice(cq * CHUNK, (cq + 1) * CHUNK)
        for hh in heads:
            s = direction * M_HEADS + hh
            v1s[cq, hh] = jnp.concatenate([v_ref[rows, hcs[hh]], ones], axis=1)
            kw_t = (k_ref[rows, hcs[hh]].astype(F32).T
                    * c_ref[cq, N_STREAMS + s:N_STREAMS + s + 1, :]).astype(BF16)
            c_loc[cq, hh] = _dot(kw_t, v1s[cq, hh])
    for cq in order:
        rows = slice(cq * CHUNK, (cq + 1) * CHUNK)
        for hh in heads:
            s = direction * M_HEADS + hh
            state = st[hh]
            m_run = bc[cq * M_HEADS + hh]
            floor = bc[(BLOCK_CHUNKS + cq) * M_HEADS + hh]
            c_ext = jnp.concatenate([c_ref[cq, s:s + 1, :], c_ref[cq, 2 * N_STREAMS + s:2 * N_STREAMS + s + 1, :]],
                                    axis=1)
            d_ext = jnp.exp(c_ext - jnp.concatenate([m_run, m_run], axis=1))
            p_ext = jnp.concatenate([qk[cq, hh] * jnp.where(mask, d_ext[:, 0:LANES], 0.0),
                                     qs[cq, hh].astype(F32) * d_ext[:, LANES:2 * LANES]], axis=1).astype(BF16)
            nd = _dot(p_ext, jnp.concatenate([v1s[cq, hh], state.astype(BF16)], axis=0))
            den = nd[:, LANES:LANES + 1]
            h_ref[rows, hcs[hh]] = nd[:, 0:LANES] / jnp.maximum(jnp.abs(den), floor)
            st[hh] = (dec_ref[cq, s:s + 1, 0:1] * state
                      + dec_ref[cq, N_STREAMS + s:N_STREAMS + s + 1, 0:1] * c_loc[cq, hh])


def _mlstm(mq, mk, mv, c_rows, dec, cols, *, batch, seq, direction):
    n = mq.shape[0]
    ts = BLOCK_CHUNKS * CHUNK
    nblk = seq // ts

    def blk(b, i):
        return b * nblk + (i if direction == 0 else nblk - 1 - i)

    tok = pl.BlockSpec((ts, M_WIDTH), lambda b, i: (blk(b, i), 0))
    return pl.pallas_call(
        functools.partial(_mlstm_kernel, direction=direction),
        grid=(batch, nblk),
        in_specs=[
            tok, tok, tok,
            pl.BlockSpec((BLOCK_CHUNKS, 3 * N_STREAMS, CHUNK), lambda b, i: (blk(b, i), 0, 0)),
            pl.BlockSpec((BLOCK_CHUNKS, 2 * N_STREAMS, CHUNK), lambda b, i: (blk(b, i), 0, 0)),
            pl.BlockSpec((1, CHUNK, LANES), lambda b, i: (blk(b, i), 0, 0)),
        ],
        out_specs=tok,
        out_shape=jax.ShapeDtypeStruct((n, M_WIDTH), F32),
        scratch_shapes=[pltpu.VMEM((M_HEADS, M_HEAD_DIM, 2 * LANES), F32),
                        pltpu.VMEM((2 * BLOCK_CHUNKS * M_HEADS, CHUNK, LANES), F32)],
        compiler_params=_params("parallel", "arbitrary"),
        name="mlstm_fwd" if direction == 0 else "mlstm_bwd",
    )(mq, mk, mv, c_rows, dec, cols)


def _ffn_body(xa, nrm_ref, wup_ref, cw_ref, cb_ref, wdn_ref, fin_ref, us, act, *, tm, final):
    xn = _rms(xa, nrm_ref[...]).astype(BF16)
    for c in range(D_FF // FF_COLS):
        g0 = c * FF_COLS
        v0 = D_FF + c * FF_COLS
        gate = _conv3(_dot(xn, wup_ref[:, g0:g0 + FF_COLS]), us.at[0], tm, cw_ref, cb_ref, g0)
        val = _conv3(_dot(xn, wup_ref[:, v0:v0 + FF_COLS]), us.at[1], tm, cw_ref, cb_ref, v0)
        act[:, g0:g0 + FF_COLS] = (gate * jax.nn.sigmoid(gate) * val).astype(BF16)
    y = xa[HALO:HALO + tm] + _dot(act[...], wdn_ref[...])
    if final:
        y = _rms(y, fin_ref[...])
    return y


def _ffn_kernel(x_ref, xp_ref, xn_ref, nrm_ref, wup_ref, cw_ref, cb_ref, wdn_ref, fin_ref, o_ref, us, act,
                *, tm, tps, final):
    jj = pl.program_id(0) % tps
    xa = jnp.concatenate([
        jnp.where(jj == 0, 0.0, xp_ref[...]),
        x_ref[...],
        jnp.where(jj == tps - 1, 0.0, xn_ref[...]),
    ], axis=0)
    o_ref[...] = _ffn_body(xa, nrm_ref, wup_ref, cw_ref, cb_ref, wdn_ref, fin_ref, us, act, tm=tm, final=final)


def _ffn_scratch(tm):
    return [pltpu.VMEM((2, FF_COLS // LANES, ROW_PITCH * (tm + 2 * HALO), LANES), F32),
            pltpu.VMEM((tm, D_FF), BF16)]


def _ab_out_ffn_kernel(x_ref, xp_ref, xn_ref, ya_ref, yap_ref, yan_ref, hf_ref, hfp_ref, hfn_ref, hb_ref, hbp_ref,
                       hbn_ref, mo_ref, mop_ref, mon_ref, on_ref, wa_ref, wm_ref, nrm_ref, wup_ref, cw_ref, cb_ref,
                       wdn_ref, fin_ref, o_ref, us, act, *, tm, tps):
    jj = pl.program_id(0) % tps

    def rows(main, prev, nxt):
        return jnp.concatenate([main[...], prev[...], nxt[...]], axis=0)

    hf, hb, mo = rows(hf_ref, hfp_ref, hfn_ref), rows(hb_ref, hbp_ref, hbn_ref), rows(mo_ref, mop_ref, mon_ref)
    parts = []
    for hh in range(M_HEADS):
        hc = slice(hh * LANES, (hh + 1) * LANES)
        t = hf[:, hc] + hb[:, hc]
        t = t * lax.rsqrt(jnp.mean(t * t, axis=-1, keepdims=True) + EPS) * on_ref[:, hc]
        parts.append((jax.nn.sigmoid(mo[:, hc]) * t).astype(BF16))
    ym = jnp.concatenate(parts, axis=1)
    x1 = (rows(x_ref, xp_ref, xn_ref) + _dot(rows(ya_ref, yap_ref, yan_ref), wa_ref[...]) + _dot(ym, wm_ref[...]))
    xa = jnp.concatenate([
        jnp.where(jj == 0, 0.0, x1[tm + PAIR - HALO:tm + PAIR]),
        x1[0:tm],
        jnp.where(jj == tps - 1, 0.0, x1[tm + PAIR:tm + PAIR + HALO]),
    ], axis=0)
    o_ref[...] = _ffn_body(xa, nrm_ref, wup_ref, cw_ref, cb_ref, wdn_ref, fin_ref, us, act, tm=tm, final=False)


def _ab_out_ffn(x, ya, hf, hb, mo, on, wa, wm, nrm, wup, cw, cb, wdn, fin, *, seq, tm):
    n = x.shape[0]
    row = lambda i: (i, 0)

    def with_halo(width):
        return [pl.BlockSpec((tm, width), row), *_halo_specs(tm, n, width, PAIR)]

    consts = [on, wa, wm, nrm, wup, cw, cb, wdn, fin]
    return pl.pallas_call(
        functools.partial(_ab_out_ffn_kernel, tm=tm, tps=seq // tm),
        grid=(n // tm,),
        in_specs=with_halo(D_MODEL) + 4 * with_halo(M_WIDTH) + [_const_spec(c.shape) for c in consts],
        out_specs=pl.BlockSpec((tm, D_MODEL), row),
        out_shape=jax.ShapeDtypeStruct((n, D_MODEL), F32),
        scratch_shapes=_ffn_scratch(tm),
        compiler_params=_params("parallel"),
        name="ab_out_ffn",
    )(x, x, x, ya, ya, ya, hf, hf, hf, hb, hb, hb, mo, mo, mo, *consts)


def _ffn(x, nrm, wup, cw, cb, wdn, fin, *, seq, tm, final):
    n = x.shape[0]
    row = lambda i: (i, 0)
    return pl.pallas_call(
        functools.partial(_ffn_kernel, tm=tm, tps=seq // tm, final=final),
        grid=(n // tm,),
        in_specs=[
            pl.BlockSpec((tm, D_MODEL), row), *_halo_specs(tm, n, D_MODEL),
            _const_spec(nrm.shape), _const_spec(wup.shape), _const_spec(cw.shape), _const_spec(cb.shape),
            _const_spec(wdn.shape), _const_spec(fin.shape),
        ],
        out_specs=pl.BlockSpec((tm, D_MODEL), row),
        out_shape=jax.ShapeDtypeStruct((n, D_MODEL), F32),
        scratch_shapes=_ffn_scratch(tm),
        compiler_params=_params("parallel"),
        name="ffn_final" if final else "ffn",
    )(x, x, x, nrm, wup, cw, cb, wdn, fin)


def _gelu(x):
    return 0.5 * x * (1.0 + lax.erf(x * (2.0 ** -0.5)))


def _mix_c_kernel(x_ref, nrm_ref, win_ref, vn_ref, ws_ref, bs_ref, wout_ref, o_ref, zs, *, tm):
    x = x_ref[...]
    xn = _rms(x, nrm_ref[...]).astype(BF16)
    v = _gelu(_dot(xn, win_ref[:, D_MODEL:2 * D_MODEL]))
    u = _gelu(_dot(xn, win_ref[:, 0:D_MODEL]))
    vn = _rms(v, vn_ref[...]).astype(BF16)
    nch = tm // CHUNK
    for g in range(G_GROUPS):
        cols = slice(g * LANES, (g + 1) * LANES)
        rhs = jnp.concatenate([vn[c * CHUNK:(c + 1) * CHUNK, cols] for c in range(nch)], axis=1)
        sv = _dot(ws_ref[g], rhs) + bs_ref[:, g:g + 1]
        for c in range(nch):
            zs[c * CHUNK:(c + 1) * CHUNK, cols] = (
                u[c * CHUNK:(c + 1) * CHUNK, cols] * sv[:, c * LANES:(c + 1) * LANES]).astype(BF16)
    o_ref[...] = x + _dot(zs[...], wout_ref[...])


def _mix_c(x, nrm, win, vn, ws, bs, wout, *, tm):
    n = x.shape[0]
    row = lambda i: (i, 0)
    return pl.pallas_call(
        functools.partial(_mix_c_kernel, tm=tm),
        grid=(n // tm,),
        in_specs=[
            pl.BlockSpec((tm, D_MODEL), row),
            _const_spec(nrm.shape), _const_spec(win.shape), _const_spec(vn.shape), _const_spec(ws.shape),
            _const_spec(bs.shape), _const_spec(wout.shape),
        ],
        out_specs=pl.BlockSpec((tm, D_MODEL), row),
        out_shape=jax.ShapeDtypeStruct((n, D_MODEL), F32),
        scratch_shapes=[pltpu.VMEM((tm, D_MODEL), BF16)],
        compiler_params=_params("parallel"),
        name="mix_c",
    )(x, nrm, win, vn, ws, bs, wout)


def _rope_tables(seq):
    pos = jnp.arange(seq, dtype=F32)
    inv = 1.0 / (ROPE_THETA ** (jnp.arange(0, QK_ROPE, 2, dtype=F32) / QK_ROPE))
    ang = pos[:, None] * inv[None, :]
    cos, sin = jnp.cos(ang), jnp.sin(ang)
    one = jnp.ones((seq, QK_NOPE), F32)
    zero_n = jnp.zeros((seq, QK_NOPE), F32)
    pad = jnp.zeros((seq, LANES - QK_NOPE - QK_ROPE), F32)
    return (jnp.concatenate([one, cos, cos, pad], axis=1), jnp.concatenate([zero_n, sin, sin, pad], axis=1))


def _head_block(nope, x1, x2):
    pad = jnp.zeros((nope.shape[0], LANES - QK_NOPE - QK_ROPE), nope.dtype)
    return jnp.concatenate([nope, x1, x2, pad], axis=1)


def _prep_even(i, ab_norm, ab_w_in, mla_q_norm, mla_w_uq, mla_kv_norm, mla_w_ukv, mlstm_conv_w, mlstm_conv_b,
               mlstm_gate_bias, mlstm_out_norm, ab_w_out):
    w_in = ab_w_in[i]
    half = QK_ROPE // 2
    zq = jnp.zeros((D_MODEL, QK_NOPE), F32)
    kr1 = w_in[:, 384:384 + half]
    kr2 = w_in[:, 384 + half:416]
    gate_order = jnp.array([0, 1, 2, 3, 8, 9, 10, 11, 4, 5, 6, 7, 12, 13, 14, 15])
    gates = jnp.pad(w_in[:, 2464:2480][:, gate_order], ((0, 0), (0, LANES - 2 * N_STREAMS)))
    w_big = jnp.concatenate([
        w_in[:, 0:384],
        _head_block(zq, kr1, kr2),
        _head_block(zq, -kr2, kr1),
        gates,
        w_in[:, 416:2464],
    ], axis=1).astype(BF16)
    w_uq = mla_w_uq[i].reshape(Q_LORA, MLA_HEADS, QK_NOPE + QK_ROPE)
    zn = jnp.zeros((Q_LORA, QK_NOPE), F32)
    wqa = jnp.concatenate([_head_block(w_uq[:, h, :QK_NOPE], w_uq[:, h, QK_NOPE:QK_NOPE + half],
                                       w_uq[:, h, QK_NOPE + half:]) for h in range(MLA_HEADS)], axis=1)
    wqb = jnp.concatenate([_head_block(zn, -w_uq[:, h, QK_NOPE + half:], w_uq[:, h, QK_NOPE:QK_NOPE + half])
                           for h in range(MLA_HEADS)], axis=1)
    w_ukv = mla_w_ukv[i].reshape(KV_LORA, MLA_HEADS, QK_NOPE + V_HEAD)
    wka = jnp.pad(w_ukv[:, :, :QK_NOPE], ((0, 0), (0, 0), (0, LANES - QK_NOPE))).reshape(KV_LORA, MLA_HEADS * LANES)
    zv = jnp.zeros((KV_LORA, V_HEAD), F32)
    wv = jnp.concatenate([jnp.concatenate([w_ukv[:, h, QK_NOPE:], zv] if h % 2 == 0 else [zv, w_ukv[:, h, QK_NOPE:]],
                                          axis=1) for h in range(MLA_HEADS)], axis=1)
    one_even = jnp.zeros((LANES,), F32).at[V_HEAD].set(1.0)
    one_odd = jnp.zeros((LANES,), F32).at[0].set(1.0)
    vone = jnp.concatenate([one_even if h % 2 == 0 else one_odd for h in range(MLA_HEADS)])[None, :]
    w_out = ab_w_out[i].astype(BF16)
    return dict(
        nrm=ab_norm[i][None, :], w_big=w_big, qn=mla_q_norm[i][None, :], wqa=wqa.astype(BF16),
        wqb=wqb.astype(BF16), kvn=mla_kv_norm[i][None, :], wka=wka.astype(BF16), wv=wv.astype(BF16),
        gb=jnp.broadcast_to(mlstm_gate_bias[i][gate_order][:, None], (2 * N_STREAMS, LANES)), vone=vone,
        conv_w=mlstm_conv_w[i], conv_b=mlstm_conv_b[i][None, :], on=mlstm_out_norm[i][None, :],
        wa=w_out[:MLA_HEADS * V_HEAD], wm=w_out[MLA_HEADS * V_HEAD:])


def _pick(pref, seq):
    return min(pref, seq)


def _forward(x3, even, odd, ffn, final_norm):
    batch, seq, _ = x3.shape
    x = x3.reshape(batch * seq, D_MODEL)
    tm = _pick(512, seq)
    cos_t, sin_t = _rope_tables(seq)
    e = even
    q, k, v, mq, mk, mv, mo, g = _ab_in(x, cos_t, sin_t, e, seq=seq, tm=tm)
    ya = _attn(q, k, v, batch=batch, seq=seq, tq=_pick(SCORE_ELEMS // seq, seq),
               pairs=min(MLA_HEADS // 2, max(1, KV_BLOCK_ELEMS // (seq * 2 * LANES))))
    c_rows, dec, cols = _gates(g, batch=batch, seq=seq)
    hf = _mlstm(mq, mk, mv, c_rows, dec, cols, batch=batch, seq=seq, direction=0)
    hb = _mlstm(mq, mk, mv, c_rows, dec, cols, batch=batch, seq=seq, direction=1)
    f = ffn[0]
    x = _ab_out_ffn(x, ya, hf, hb, mo, e["on"], e["wa"], e["wm"], f["nrm"], f["wup"], f["cw"], f["cb"], f["wdn"],
                    final_norm, seq=seq, tm=tm)
    o = odd
    x = _mix_c(x, o["nrm"], o["win"], o["vn"], o["ws"], o["bs"], o["wout"], tm=tm)
    f = ffn[1]
    x = _ffn(x, f["nrm"], f["wup"], f["cw"], f["cb"], f["wdn"], final_norm, seq=seq, tm=tm, final=True)
    return x.reshape(batch, seq, D_MODEL)


def _prep_rest(c_norm, c_w_in, c_v_norm, c_w_spatial, c_b_spatial, c_w_out, ffn_norm, ffn_w_up, ffn_conv_w,
               ffn_conv_b, ffn_w_down):
    odd = dict(nrm=c_norm[0][None, :], win=c_w_in[0].astype(BF16), vn=c_v_norm[0][None, :],
               ws=c_w_spatial[0].astype(BF16), bs=c_b_spatial[0].T, wout=c_w_out[0].astype(BF16))
    ffn = [dict(nrm=ffn_norm[l][None, :], wup=ffn_w_up[l].astype(BF16), cw=ffn_conv_w[l], cb=ffn_conv_b[l][None, :],
                wdn=ffn_w_down[l].astype(BF16)) for l in range(2)]
    return odd, ffn


def kernel(x_prompt, x_sample, ab_norm, ab_w_in, mla_q_norm, mla_w_uq, mla_kv_norm, mla_w_ukv, mlstm_conv_w,
           mlstm_conv_b, mlstm_gate_bias, mlstm_out_norm, ab_w_out, c_norm, c_w_in, c_v_norm, c_w_spatial,
           c_b_spatial, c_w_out, ffn_norm, ffn_w_up, ffn_conv_w, ffn_conv_b, ffn_w_down, final_norm):
    even = _prep_even(0, ab_norm, ab_w_in, mla_q_norm, mla_w_uq, mla_kv_norm, mla_w_ukv, mlstm_conv_w, mlstm_conv_b,
                      mlstm_gate_bias, mlstm_out_norm, ab_w_out)
    odd, ffn = _prep_rest(c_norm, c_w_in, c_v_norm, c_w_spatial, c_b_spatial, c_w_out, ffn_norm, ffn_w_up,
                          ffn_conv_w, ffn_conv_b, ffn_w_down)
    fin = final_norm[None, :]
    return (_forward(x_prompt, even, odd, ffn, fin), _forward(x_sample, even, odd, ffn, fin))
```

```python
import functools

import jax
import jax.numpy as jnp
from jax import lax
from jax.experimental import pallas as pl
from jax.experimental.pallas import tpu as pltpu

F32 = jnp.float32
BF16 = jnp.bfloat16

EPS = 1e-6
D_MODEL = 1024
MLA_HEADS = 8
Q_LORA = 256
KV_LORA = 128
QK_NOPE = 64
QK_ROPE = 32
V_HEAD = 64
ROPE_THETA = 10000.0
M_HEADS = 4
M_HEAD_DIM = 128
M_WIDTH = M_HEADS * M_HEAD_DIM
N_STREAMS = 2 * M_HEADS
CHUNK = 128
G_GROUPS = 8
D_FF = 2816
LANES = 128
SUBLANES = 8
HALO = SUBLANES
BLOCK_CHUNKS = 4
PAIR = 2 * SUBLANES
MXU_COLS = 256
ROW_PITCH = 2
FF_COLS = MXU_COLS
KEY_TILE = MXU_COLS
SCORE_ELEMS = 2 * 1024 * 1024
KV_BLOCK_ELEMS = 4 * 1024 * 1024
LOG2_E = 1.4426950408889634
VMEM_LIMIT = 56 * 1024 * 1024

_C_Q, _C_KV, _C_KRA, _C_KRB, _C_G, _C_MQK, _C_MV, _C_MO, _C_END = (
    0, 256, 384, 512, 640, 768, 1792, 2304, 2816)


def _rms(x, g):
    return x * lax.rsqrt(jnp.mean(x * x, axis=-1, keepdims=True) + EPS) * g


def _const_spec(shape):
    nd = len(shape)
    return pl.BlockSpec(shape, lambda *_: (0,) * nd, pipeline_mode=pl.Buffered(1))


def _params(*sem):
    return pltpu.CompilerParams(dimension_semantics=sem, vmem_limit_bytes=VMEM_LIMIT)


def _dot(a, b):
    return jnp.dot(a, b, preferred_element_type=F32)


def _dot_nt(a, b):
    return lax.dot_general(a, b, (((1,), (1,)), ((), ())), preferred_element_type=F32)


def _halo_specs(tile_rows, n_rows, width, halo=HALO):
    tb = tile_rows // halo
    last = n_rows // halo - 1
    return (pl.BlockSpec((halo, width), lambda i: (jnp.maximum(i * tb - 1, 0), 0)),
            pl.BlockSpec((halo, width), lambda i: (jnp.minimum((i + 1) * tb, last), 0)))


def _conv3(y, u, rows, w_ref, b_ref, c0):
    tall = rows + 2 * HALO
    outs = []
    for s in range(y.shape[1] // LANES):
        lanes = slice(s * LANES, (s + 1) * LANES)
        cols = slice(c0 + s * LANES, c0 + (s + 1) * LANES)
        u[s, pl.ds(0, tall, stride=ROW_PITCH), :] = y[:, lanes]
        prv = u[s, pl.ds(ROW_PITCH * (HALO - 1), rows, stride=ROW_PITCH), :]
        nxt = u[s, pl.ds(ROW_PITCH * (HALO + 1), rows, stride=ROW_PITCH), :]
        outs.append(prv * w_ref[0:1, cols] + y[HALO:HALO + rows, lanes] * w_ref[1:2, cols]
                    + nxt * w_ref[2:3, cols] + b_ref[:, cols])
    return jnp.concatenate(outs, axis=1)


def _ab_in_kernel(x_ref, xp_ref, xn_ref, cos_ref, sin_ref, nrm_ref, w_ref, qn_ref, wqa_ref, wqb_ref, kvn_ref,
                  wka_ref, wv_ref, gb_ref, vone_ref, cw_ref, cb_ref,
                  q_ref, k_ref, v_ref, mq_ref, mk_ref, mv_ref, mo_ref, g_ref, us, *, tm, tps):
    jj = pl.program_id(0) % tps
    xa = jnp.concatenate([
        jnp.where(jj == 0, 0.0, xp_ref[...]),
        x_ref[...],
        jnp.where(jj == tps - 1, 0.0, xn_ref[...]),
    ], axis=0)
    xe = _rms(xa, nrm_ref[...]).astype(BF16)
    xn = xe[HALO:HALO + tm]

    def proj(a, b):
        return _dot(xn, w_ref[:, a:b])

    cq = proj(_C_Q, _C_KV)
    ckv_kra = proj(_C_KV, _C_KRB)
    krb_g = proj(_C_KRB, _C_MQK)

    nq_blocks = M_WIDTH // MXU_COLS
    for cb in range(2 * nq_blocks):
        cols = slice(cb * MXU_COLS, (cb + 1) * MXU_COLS)
        y = _conv3(_dot(xe, w_ref[:, _C_MQK + cb * MXU_COLS:_C_MQK + (cb + 1) * MXU_COLS]), us.at[cb % 2], tm,
                   cw_ref, cb_ref, cb * MXU_COLS)
        y = y * jax.nn.sigmoid(y)
        if cb < nq_blocks:
            mq_ref[:, cols] = y.astype(BF16)
        else:
            mk_ref[:, (cb - nq_blocks) * MXU_COLS:(cb - nq_blocks + 1) * MXU_COLS] = (
                y * (M_HEAD_DIM ** -0.5)).astype(BF16)
    mv_ref[...] = proj(_C_MV, _C_MO).astype(BF16)
    mo_ref[...] = proj(_C_MO, _C_END)
    gt = krb_g[:, LANES:2 * LANES].T
    for c in range(tm // CHUNK):
        g_ref[c] = gt[0:2 * N_STREAMS, c * CHUNK:(c + 1) * CHUNK] + gb_ref[:, 0:1]

    cos = cos_ref[...]
    sin = sin_ref[...]
    scale = (QK_NOPE + QK_ROPE) ** -0.5 * LOG2_E
    cqn = _rms(cq, qn_ref[...]).astype(BF16)
    qa = _dot(cqn, wqa_ref[...])
    qb = _dot(cqn, wqb_ref[...])
    cos_q = cos * scale
    sin_q = sin * scale
    for h in range(MLA_HEADS):
        blk = slice(h * LANES, (h + 1) * LANES)
        q_ref[:, blk] = (qa[:, blk] * cos_q + qb[:, blk] * sin_q).astype(BF16)

    ckvn = _rms(ckv_kra[:, 0:LANES], kvn_ref[...]).astype(BF16)
    ka = _dot(ckvn, wka_ref[...])
    kr = ckv_kra[:, LANES:2 * LANES] * cos + krb_g[:, 0:LANES] * sin
    for h in range(MLA_HEADS):
        blk = slice(h * LANES, (h + 1) * LANES)
        k_ref[:, blk] = (ka[:, blk] + kr).astype(BF16)
    v_ref[...] = (_dot(ckvn, wv_ref[...]) + vone_ref[...]).astype(BF16)


def _ab_in(x, cos_t, sin_t, e, *, seq, tm):
    n = x.shape[0]
    nt = n // tm
    tps = seq // tm
    row = lambda i: (i, 0)
    tab = lambda i: (i % tps, 0)
    consts = [e[k] for k in ("nrm", "w_big", "qn", "wqa", "wqb", "kvn", "wka", "wv", "gb", "vone", "conv_w",
                             "conv_b")]
    out_shape = (
        jax.ShapeDtypeStruct((n, MLA_HEADS * LANES), BF16),
        jax.ShapeDtypeStruct((n, MLA_HEADS * LANES), BF16),
        jax.ShapeDtypeStruct((n, MLA_HEADS * LANES), BF16),
        jax.ShapeDtypeStruct((n, M_WIDTH), BF16),
        jax.ShapeDtypeStruct((n, M_WIDTH), BF16),
        jax.ShapeDtypeStruct((n, M_WIDTH), BF16),
        jax.ShapeDtypeStruct((n, M_WIDTH), F32),
        jax.ShapeDtypeStruct((n // CHUNK, 2 * N_STREAMS, CHUNK), F32),
    )
    wide = pl.BlockSpec((tm, MLA_HEADS * LANES), row)
    narrow = pl.BlockSpec((tm, M_WIDTH), row)
    return pl.pallas_call(
        functools.partial(_ab_in_kernel, tm=tm, tps=tps),
        grid=(nt,),
        in_specs=[pl.BlockSpec((tm, D_MODEL), row), *_halo_specs(tm, n, D_MODEL),
                  pl.BlockSpec((tm, LANES), tab), pl.BlockSpec((tm, LANES), tab)]
                 + [_const_spec(c.shape) for c in consts],
        out_specs=(wide, wide, wide, narrow, narrow, narrow, narrow,
                   pl.BlockSpec((tm // CHUNK, 2 * N_STREAMS, CHUNK), lambda i: (i, 0, 0))),
        out_shape=out_shape,
        scratch_shapes=[pltpu.VMEM((2, MXU_COLS // LANES, ROW_PITCH * (tm + 2 * HALO), LANES), F32)],
        compiler_params=_params("parallel"),
        name="ab_in",
    )(x, x, x, cos_t, sin_t, *consts)


def _attn_kernel(q_ref, k_ref, v_ref, o_ref, s_scr, *, tq, pairs):
    nt = k_ref.shape[0] // KEY_TILE
    lane = lax.broadcasted_iota(jnp.int32, (tq, LANES), 1)
    for pr in range(pairs):
        mbs = []
        for h in range(2):
            hc = slice((2 * pr + h) * LANES, (2 * pr + h + 1) * LANES)
            q = q_ref[:, hc]
            mx = jnp.full((tq, LANES), -jnp.inf, F32)
            for t in range(nt):
                keys = slice(t * KEY_TILE, (t + 1) * KEY_TILE)
                s = _dot_nt(q, k_ref[keys, hc])
                s_scr[h, :, keys] = s
                for j in range(KEY_TILE // LANES):
                    mx = jnp.maximum(mx, s[:, j * LANES:(j + 1) * LANES])
            mbs.append(jnp.broadcast_to(jnp.max(mx, axis=-1, keepdims=True), (tq, LANES)))
        outs = []
        for h in range(2):
            hc = slice((2 * pr + h) * LANES, (2 * pr + h + 1) * LANES)
            mb = jnp.concatenate([mbs[h]] * (KEY_TILE // LANES), axis=1)
            acc = jnp.zeros((tq, LANES), F32)
            for t in range(nt):
                keys = slice(t * KEY_TILE, (t + 1) * KEY_TILE)
                p = jnp.exp2(s_scr[h, :, keys] - mb).astype(BF16)
                acc = acc + _dot(p, v_ref[keys, hc])
            ones_lane = V_HEAD if h == 0 else 0
            outs.append(acc / acc[:, ones_lane:ones_lane + 1])
        o_ref[:, pr * LANES:(pr + 1) * LANES] = jnp.where(lane < V_HEAD, outs[0], outs[1]).astype(BF16)


def _attn(q, k, v, *, batch, seq, tq, pairs):
    n = q.shape[0]
    nq = seq // tq
    steps = MLA_HEADS // 2 // pairs
    return pl.pallas_call(
        functools.partial(_attn_kernel, tq=tq, pairs=pairs),
        grid=(batch, steps, nq),
        in_specs=[
            pl.BlockSpec((tq, 2 * pairs * LANES), lambda b, p, i: (b * nq + i, p)),
            pl.BlockSpec((seq, 2 * pairs * LANES), lambda b, p, i: (b, p)),
            pl.BlockSpec((seq, 2 * pairs * LANES), lambda b, p, i: (b, p)),
        ],
        out_specs=pl.BlockSpec((tq, pairs * LANES), lambda b, p, i: (b * nq + i, p)),
        out_shape=jax.ShapeDtypeStruct((n, MLA_HEADS * V_HEAD), BF16),
        scratch_shapes=[pltpu.VMEM((2, tq, seq), F32)],
        compiler_params=_params("parallel", "parallel", "arbitrary"),
        name="attn",
    )(q, k, v)


def _log_sigmoid(x):
    return jnp.minimum(x, 0.0) - jnp.log1p(jnp.exp(-jnp.abs(x)))


def _gates_kernel(g_ref, c_ref, dec_ref, cols_ref, tot_s, mloc_s, mpf_s, mpb_s, *, nc):
    rows = nc * N_STREAMS
    shape = (rows, CHUNK)
    three = (nc, N_STREAMS, CHUNK)
    li = g_ref[:, 0:N_STREAMS, :].reshape(shape)
    lf = _log_sigmoid(g_ref[:, N_STREAMS:2 * N_STREAMS, :].reshape(shape))
    fwd = (lax.broadcasted_iota(jnp.int32, shape, 0) & M_HEADS) == 0
    lane = lax.broadcasted_iota(jnp.int32, shape, 1)
    shifts = (1, 2, 4, 8, 16, 32, 64)

    pre = lf
    for sh in shifts:
        pre = pre + jnp.where(lane >= sh, pltpu.roll(pre, sh, axis=1), 0.0)
    tot = jnp.broadcast_to(pre[:, CHUNK - 1:CHUNK], shape)
    b = jnp.where(fwd, pre, tot - pre + lf)
    c = li - b
    run_f = c
    run_b = c
    for sh in shifts:
        run_f = jnp.maximum(run_f, jnp.where(lane >= sh, pltpu.roll(run_f, sh, axis=1), -jnp.inf))
        run_b = jnp.maximum(run_b, jnp.where(lane < CHUNK - sh, pltpu.roll(run_b, CHUNK - sh, axis=1), -jnp.inf))
    run = jnp.where(fwd, run_f, run_b)
    m_loc = tot + jnp.broadcast_to(jnp.max(c, axis=1, keepdims=True), shape)

    tot_s[...] = tot.reshape(three)
    mloc_s[...] = m_loc.reshape(three)
    fwd8 = lax.broadcasted_iota(jnp.int32, (N_STREAMS, CHUNK), 0) < M_HEADS

    def step(i, m):
        jb = nc - 1 - i
        mpf_s[i] = m
        mpb_s[jb] = m
        return jnp.maximum(jnp.where(fwd8, tot_s[i], tot_s[jb]) + m, jnp.where(fwd8, mloc_s[i], mloc_s[jb]))

    lax.fori_loop(0, nc, step, jnp.zeros((N_STREAMS, CHUNK), F32))
    m_prev = jnp.where(fwd, mpf_s[...].reshape(shape), mpb_s[...].reshape(shape))

    m_run = jnp.maximum(m_prev, run)
    floor = jnp.exp(-(b + m_run))
    w = jnp.exp(tot + c - m_loc)
    m_new = jnp.maximum(tot + m_prev, m_loc)
    c_ref[:, 0:N_STREAMS, :] = c.reshape(three)
    c_ref[:, N_STREAMS:2 * N_STREAMS, :] = w.reshape(three)
    c_ref[:, 2 * N_STREAMS:3 * N_STREAMS, :] = m_prev.reshape(three)
    dec_ref[:, 0:N_STREAMS, :] = jnp.exp(tot + m_prev - m_new).reshape(three)
    dec_ref[:, N_STREAMS:2 * N_STREAMS, :] = jnp.exp(m_loc - m_new).reshape(three)
    quantities = [a.reshape(three) for a in (m_run, floor)]
    pad = jnp.zeros((CHUNK - BLOCK_CHUNKS * 2 * N_STREAMS, CHUNK), F32)
    for blk in range(nc // BLOCK_CHUNKS):
        tile = jnp.concatenate(
            [a[blk * BLOCK_CHUNKS + cq] for cq in range(BLOCK_CHUNKS) for a in quantities] + [pad], axis=0)
        cols_ref[blk] = tile.T


def _gates(g, *, batch, seq):
    nc = seq // CHUNK
    nchunks = g.shape[0]
    three = (nc, N_STREAMS, CHUNK)
    return pl.pallas_call(
        functools.partial(_gates_kernel, nc=nc),
        grid=(batch,),
        in_specs=[pl.BlockSpec((nc, 2 * N_STREAMS, CHUNK), lambda b: (b, 0, 0))],
        out_specs=(
            pl.BlockSpec((nc, 3 * N_STREAMS, CHUNK), lambda b: (b, 0, 0)),
            pl.BlockSpec((nc, 2 * N_STREAMS, CHUNK), lambda b: (b, 0, 0)),
            pl.BlockSpec((nc // BLOCK_CHUNKS, CHUNK, LANES), lambda b: (b, 0, 0)),
        ),
        out_shape=(
            jax.ShapeDtypeStruct((nchunks, 3 * N_STREAMS, CHUNK), F32),
            jax.ShapeDtypeStruct((nchunks, 2 * N_STREAMS, CHUNK), F32),
            jax.ShapeDtypeStruct((nchunks // BLOCK_CHUNKS, CHUNK, LANES), F32),
        ),
        scratch_shapes=[pltpu.VMEM(three, F32)] * 4,
        compiler_params=_params("parallel"),
        name="gates",
    )(g)


def _mlstm_kernel(*refs):
    ins, (hf_ref, hb_ref, st, bc) = (refs[0:6], refs[6:12]), refs[12:]

    @pl.when(pl.program_id(1) == 0)
    def _():
        st[...] = jnp.zeros_like(st)

    row = lax.broadcasted_iota(jnp.int32, (CHUNK, CHUNK), 0)
    col = lax.broadcasted_iota(jnp.int32, (CHUNK, CHUNK), 1)
    masks = (col <= row, col >= row)
    ones = (col == 0).astype(BF16)
    orders = (range(BLOCK_CHUNKS), range(BLOCK_CHUNKS - 1, -1, -1))
    heads = range(M_HEADS)
    hcs = [slice(hh * LANES, (hh + 1) * LANES) for hh in heads]
    for d in range(2):
        cols_t = ins[d][5][0]
        for cq in range(BLOCK_CHUNKS):
            for hh in heads:
                base = cq * 2 * N_STREAMS + d * M_HEADS + hh
                for qi in range(2):
                    lane_i = base + qi * N_STREAMS
                    bc[d, (qi * BLOCK_CHUNKS + cq) * M_HEADS + hh] = jnp.broadcast_to(
                        cols_t[:, lane_i:lane_i + 1], (CHUNK, LANES))
    qs, v1s, qk, c_loc = {}, {}, {}, {}
    for d in range(2):
        q_ref, k_ref = ins[d][0], ins[d][1]
        for cq in orders[d]:
            rows = slice(cq * CHUNK, (cq + 1) * CHUNK)
            for hh in heads:
                qs[d, cq, hh] = q_ref[rows, hcs[hh]]
                qk[d, cq, hh] = _dot_nt(qs[d, cq, hh], k_ref[rows, hcs[hh]])
    for d in range(2):
        k_ref, v_ref, c_ref = ins[d][1], ins[d][2], ins[d][3]
        for cq in orders[d]:
            rows = slice(cq * CHUNK, (cq + 1) * CHUNK)
            for hh in heads:
                s = d * M_HEADS + hh
                v1s[d, cq, hh] = jnp.concatenate([v_ref[rows, hcs[hh]], ones], axis=1)
                kw_t = (k_ref[rows, hcs[hh]].astype(F32).T
                        * c_ref[cq, N_STREAMS + s:N_STREAMS + s + 1, :]).astype(BF16)
                c_loc[d, cq, hh] = _dot(kw_t, v1s[d, cq, hh])
    for step in range(BLOCK_CHUNKS):
        for d in range(2):
            _mlstm_chunk(d, orders[d][step], ins[d][3], ins[d][4], (hf_ref, hb_ref)[d], st, bc, masks[d], hcs,
                         qs, v1s, qk, c_loc)


def _mlstm_chunk(d, cq, c_ref, dec_ref, h_ref, st, bc, mask, hcs, qs, v1s, qk, c_loc):
    rows = slice(cq * CHUNK, (cq + 1) * CHUNK)
    for hh in range(M_HEADS):
        s = d * M_HEADS + hh
        key = (d, cq, hh)
        state = st[d, hh]
        m_run = bc[d, cq * M_HEADS + hh]
        floor = bc[d, (BLOCK_CHUNKS + cq) * M_HEADS + hh]
        c_ext = jnp.concatenate([c_ref[cq, s:s + 1, :], c_ref[cq, 2 * N_STREAMS + s:2 * N_STREAMS + s + 1, :]],
                                axis=1)
        d_ext = jnp.exp(c_ext - jnp.concatenate([m_run, m_run], axis=1))
        p_ext = jnp.concatenate([qk[key] * jnp.where(mask, d_ext[:, 0:LANES], 0.0),
                                 qs[key].astype(F32) * d_ext[:, LANES:2 * LANES]], axis=1).astype(BF16)
        nd = _dot(p_ext, jnp.concatenate([v1s[key], state.astype(BF16)], axis=0))
        den = nd[:, LANES:LANES + 1]
        h_ref[rows, hcs[hh]] = nd[:, 0:LANES] / jnp.maximum(jnp.abs(den), floor)
        st[d, hh] = (dec_ref[cq, s:s + 1, 0:1] * state
                     + dec_ref[cq, N_STREAMS + s:N_STREAMS + s + 1, 0:1] * c_loc[key])


def _mlstm(mq, mk, mv, c_rows, dec, cols, *, batch, seq):
    n = mq.shape[0]
    ts = BLOCK_CHUNKS * CHUNK
    nblk = seq // ts

    def specs(direction):
        def blk(b, i):
            return b * nblk + (i if direction == 0 else nblk - 1 - i)

        tok = pl.BlockSpec((ts, M_WIDTH), lambda b, i: (blk(b, i), 0))
        return tok, [tok, tok, tok,
                     pl.BlockSpec((BLOCK_CHUNKS, 3 * N_STREAMS, CHUNK), lambda b, i: (blk(b, i), 0, 0)),
                     pl.BlockSpec((BLOCK_CHUNKS, 2 * N_STREAMS, CHUNK), lambda b, i: (blk(b, i), 0, 0)),
                     pl.BlockSpec((1, CHUNK, LANES), lambda b, i: (blk(b, i), 0, 0))]

    (tok_f, in_f), (tok_b, in_b) = specs(0), specs(1)
    h_shape = jax.ShapeDtypeStruct((n, M_WIDTH), F32)
    args = (mq, mk, mv, c_rows, dec, cols)
    return pl.pallas_call(
        _mlstm_kernel,
        grid=(batch, nblk),
        in_specs=in_f + in_b,
        out_specs=(tok_f, tok_b),
        out_shape=(h_shape, h_shape),
        scratch_shapes=[pltpu.VMEM((2, M_HEADS, M_HEAD_DIM, 2 * LANES), F32),
                        pltpu.VMEM((2, 2 * BLOCK_CHUNKS * M_HEADS, CHUNK, LANES), F32)],
        compiler_params=_params("parallel", "arbitrary"),
        name="mlstm",
    )(*args, *args)


def _ffn_body(xa, nrm_ref, wup_ref, cw_ref, cb_ref, wdn_ref, fin_ref, us, act, *, tm, final):
    xn = _rms(xa, nrm_ref[...]).astype(BF16)
    for c in range(D_FF // FF_COLS):
        g0 = c * FF_COLS
        v0 = D_FF + c * FF_COLS
        gate = _conv3(_dot(xn, wup_ref[:, g0:g0 + FF_COLS]), us.at[0], tm, cw_ref, cb_ref, g0)
        val = _conv3(_dot(xn, wup_ref[:, v0:v0 + FF_COLS]), us.at[1], tm, cw_ref, cb_ref, v0)
        act[:, g0:g0 + FF_COLS] = (gate * jax.nn.sigmoid(gate) * val).astype(BF16)
    y = xa[HALO:HALO + tm] + _dot(act[...], wdn_ref[...])
    if final:
        y = _rms(y, fin_ref[...])
    return y


def _ffn_kernel(x_ref, xp_ref, xn_ref, nrm_ref, wup_ref, cw_ref, cb_ref, wdn_ref, fin_ref, o_ref, us, act,
                *, tm, tps, final):
    jj = pl.program_id(0) % tps
    xa = jnp.concatenate([
        jnp.where(jj == 0, 0.0, xp_ref[...]),
        x_ref[...],
        jnp.where(jj == tps - 1, 0.0, xn_ref[...]),
    ], axis=0)
    o_ref[...] = _ffn_body(xa, nrm_ref, wup_ref, cw_ref, cb_ref, wdn_ref, fin_ref, us, act, tm=tm, final=final)


def _ffn_scratch(tm):
    return [pltpu.VMEM((2, FF_COLS // LANES, ROW_PITCH * (tm + 2 * HALO), LANES), F32),
            pltpu.VMEM((tm, D_FF), BF16)]


def _ab_out_ffn_kernel(x_ref, xp_ref, xn_ref, ya_ref, yap_ref, yan_ref, hf_ref, hfp_ref, hfn_ref, hb_ref, hbp_ref,
                       hbn_ref, mo_ref, mop_ref, mon_ref, on_ref, wa_ref, wm_ref, nrm_ref, wup_ref, cw_ref, cb_ref,
                       wdn_ref, fin_ref, o_ref, us, act, *, tm, tps):
    jj = pl.program_id(0) % tps

    def rows(main, prev, nxt):
        return jnp.concatenate([main[...], prev[...], nxt[...]], axis=0)

    hf, hb, mo = rows(hf_ref, hfp_ref, hfn_ref), rows(hb_ref, hbp_ref, hbn_ref), rows(mo_ref, mop_ref, mon_ref)
    parts = []
    for hh in range(M_HEADS):
        hc = slice(hh * LANES, (hh + 1) * LANES)
        t = hf[:, hc] + hb[:, hc]
        t = t * lax.rsqrt(jnp.mean(t * t, axis=-1, keepdims=True) + EPS) * on_ref[:, hc]
        parts.append((jax.nn.sigmoid(mo[:, hc]) * t).astype(BF16))
    ym = jnp.concatenate(parts, axis=1)
    x1 = (rows(x_ref, xp_ref, xn_ref) + _dot(rows(ya_ref, yap_ref, yan_ref), wa_ref[...]) + _dot(ym, wm_ref[...]))
    xa = jnp.concatenate([
        jnp.where(jj == 0, 0.0, x1[tm + PAIR - HALO:tm + PAIR]),
        x1[0:tm],
        jnp.where(jj == tps - 1, 0.0, x1[tm + PAIR:tm + PAIR + HALO]),
    ], axis=0)
    o_ref[...] = _ffn_body(xa, nrm_ref, wup_ref, cw_ref, cb_ref, wdn_ref, fin_ref, us, act, tm=tm, final=False)


def _ab_out_ffn(x, ya, hf, hb, mo, on, wa, wm, nrm, wup, cw, cb, wdn, fin, *, seq, tm):
    n = x.shape[0]
    row = lambda i: (i, 0)

    def with_halo(width):
        return [pl.BlockSpec((tm, width), row), *_halo_specs(tm, n, width, PAIR)]

    consts = [on, wa, wm, nrm, wup, cw, cb, wdn, fin]
    return pl.pallas_call(
        functools.partial(_ab_out_ffn_kernel, tm=tm, tps=seq // tm),
        grid=(n // tm,),
        in_specs=with_halo(D_MODEL) + 4 * with_halo(M_WIDTH) + [_const_spec(c.shape) for c in consts],
        out_specs=pl.BlockSpec((tm, D_MODEL), row),
        out_shape=jax.ShapeDtypeStruct((n, D_MODEL), F32),
        scratch_shapes=_ffn_scratch(tm),
        compiler_params=_params("parallel"),
        name="ab_out_ffn",
    )(x, x, x, ya, ya, ya, hf, hf, hf, hb, hb, hb, mo, mo, mo, *consts)


def _ffn(x, nrm, wup, cw, cb, wdn, fin, *, seq, tm, final):
    n = x.shape[0]
    row = lambda i: (i, 0)
    return pl.pallas_call(
        functools.partial(_ffn_kernel, tm=tm, tps=seq // tm, final=final),
        grid=(n // tm,),
        in_specs=[
            pl.BlockSpec((tm, D_MODEL), row), *_halo_specs(tm, n, D_MODEL),
            _const_spec(nrm.shape), _const_spec(wup.shape), _const_spec(cw.shape), _const_spec(cb.shape),
            _const_spec(wdn.shape), _const_spec(fin.shape),
        ],
        out_specs=pl.BlockSpec((tm, D_MODEL), row),
        out_shape=jax.ShapeDtypeStruct((n, D_MODEL), F32),
        scratch_shapes=_ffn_scratch(tm),
        compiler_params=_params("parallel"),
        name="ffn_final" if final else "ffn",
    )(x, x, x, nrm, wup, cw, cb, wdn, fin)


def _gelu(x):
    return 0.5 * x * (1.0 + lax.erf(x * (2.0 ** -0.5)))


def _mix_c_kernel(x_ref, nrm_ref, win_ref, vn_ref, ws_ref, bs_ref, wout_ref, o_ref, zs, *, tm):
    x = x_ref[...]
    xn = _rms(x, nrm_ref[...]).astype(BF16)
    v = _gelu(_dot(xn, win_ref[:, D_MODEL:2 * D_MODEL]))
    u = _gelu(_dot(xn, win_ref[:, 0:D_MODEL]))
    vn = _rms(v, vn_ref[...]).astype(BF16)
    nch = tm // CHUNK
    for g in range(G_GROUPS):
        cols = slice(g * LANES, (g + 1) * LANES)
        rhs = jnp.concatenate([vn[c * CHUNK:(c + 1) * CHUNK, cols] for c in range(nch)], axis=1)
        sv = _dot(ws_ref[g], rhs) + bs_ref[:, g:g + 1]
        for c in range(nch):
            zs[c * CHUNK:(c + 1) * CHUNK, cols] = (
                u[c * CHUNK:(c + 1) * CHUNK, cols] * sv[:, c * LANES:(c + 1) * LANES]).astype(BF16)
    o_ref[...] = x + _dot(zs[...], wout_ref[...])


def _mix_c(x, nrm, win, vn, ws, bs, wout, *, tm):
    n = x.shape[0]
    row = lambda i: (i, 0)
    return pl.pallas_call(
        functools.partial(_mix_c_kernel, tm=tm),
        grid=(n // tm,),
        in_specs=[
            pl.BlockSpec((tm, D_MODEL), row),
            _const_spec(nrm.shape), _const_spec(win.shape), _const_spec(vn.shape), _const_spec(ws.shape),
            _const_spec(bs.shape), _const_spec(wout.shape),
        ],
        out_specs=pl.BlockSpec((tm, D_MODEL), row),
        out_shape=jax.ShapeDtypeStruct((n, D_MODEL), F32),
        scratch_shapes=[pltpu.VMEM((tm, D_MODEL), BF16)],
        compiler_params=_params("parallel"),
        name="mix_c",
    )(x, nrm, win, vn, ws, bs, wout)


def _rope_tables(seq):
    pos = jnp.arange(seq, dtype=F32)
    inv = 1.0 / (ROPE_THETA ** (jnp.arange(0, QK_ROPE, 2, dtype=F32) / QK_ROPE))
    ang = pos[:, None] * inv[None, :]
    cos, sin = jnp.cos(ang), jnp.sin(ang)
    one = jnp.ones((seq, QK_NOPE), F32)
    zero_n = jnp.zeros((seq, QK_NOPE), F32)
    pad = jnp.zeros((seq, LANES - QK_NOPE - QK_ROPE), F32)
    return (jnp.concatenate([one, cos, cos, pad], axis=1), jnp.concatenate([zero_n, sin, sin, pad], axis=1))


def _head_block(nope, x1, x2):
    pad = jnp.zeros((nope.shape[0], LANES - QK_NOPE - QK_ROPE), nope.dtype)
    return jnp.concatenate([nope, x1, x2, pad], axis=1)


def _prep_even(i, ab_norm, ab_w_in, mla_q_norm, mla_w_uq, mla_kv_norm, mla_w_ukv, mlstm_conv_w, mlstm_conv_b,
               mlstm_gate_bias, mlstm_out_norm, ab_w_out):
    w_in = ab_w_in[i]
    half = QK_ROPE // 2
    zq = jnp.zeros((D_MODEL, QK_NOPE), F32)
    kr1 = w_in[:, 384:384 + half]
    kr2 = w_in[:, 384 + half:416]
    gate_order = jnp.array([0, 1, 2, 3, 8, 9, 10, 11, 4, 5, 6, 7, 12, 13, 14, 15])
    gates = jnp.pad(w_in[:, 2464:2480][:, gate_order], ((0, 0), (0, LANES - 2 * N_STREAMS)))
    w_big = jnp.concatenate([
        w_in[:, 0:384],
        _head_block(zq, kr1, kr2),
        _head_block(zq, -kr2, kr1),
        gates,
        w_in[:, 416:2464],
    ], axis=1).astype(BF16)
    w_uq = mla_w_uq[i].reshape(Q_LORA, MLA_HEADS, QK_NOPE + QK_ROPE)
    zn = jnp.zeros((Q_LORA, QK_NOPE), F32)
    wqa = jnp.concatenate([_head_block(w_uq[:, h, :QK_NOPE], w_uq[:, h, QK_NOPE:QK_NOPE + half],
                                       w_uq[:, h, QK_NOPE + half:]) for h in range(MLA_HEADS)], axis=1)
    wqb = jnp.concatenate([_head_block(zn, -w_uq[:, h, QK_NOPE + half:], w_uq[:, h, QK_NOPE:QK_NOPE + half])
                           for h in range(MLA_HEADS)], axis=1)
    w_ukv = mla_w_ukv[i].reshape(KV_LORA, MLA_HEADS, QK_NOPE + V_HEAD)
    wka = jnp.pad(w_ukv[:, :, :QK_NOPE], ((0, 0), (0, 0), (0, LANES - QK_NOPE))).reshape(KV_LORA, MLA_HEADS * LANES)
    zv = jnp.zeros((KV_LORA, V_HEAD), F32)
    wv = jnp.concatenate([jnp.concatenate([w_ukv[:, h, QK_NOPE:], zv] if h % 2 == 0 else [zv, w_ukv[:, h, QK_NOPE:]],
                                          axis=1) for h in range(MLA_HEADS)], axis=1)
    one_even = jnp.zeros((LANES,), F32).at[V_HEAD].set(1.0)
    one_odd = jnp.zeros((LANES,), F32).at[0].set(1.0)
    vone = jnp.concatenate([one_even if h % 2 == 0 else one_odd for h in range(MLA_HEADS)])[None, :]
    w_out = ab_w_out[i].astype(BF16)
    return dict(
        nrm=ab_norm[i][None, :], w_big=w_big, qn=mla_q_norm[i][None, :], wqa=wqa.astype(BF16),
        wqb=wqb.astype(BF16), kvn=mla_kv_norm[i][None, :], wka=wka.astype(BF16), wv=wv.astype(BF16),
        gb=jnp.broadcast_to(mlstm_gate_bias[i][gate_order][:, None], (2 * N_STREAMS, LANES)), vone=vone,
        conv_w=mlstm_conv_w[i], conv_b=mlstm_conv_b[i][None, :], on=mlstm_out_norm[i][None, :],
        wa=w_out[:MLA_HEADS * V_HEAD], wm=w_out[MLA_HEADS * V_HEAD:])


def _pick(pref, seq):
    return min(pref, seq)


def _forward(x3, even, odd, ffn, final_norm):
    batch, seq, _ = x3.shape
    x = x3.reshape(batch * seq, D_MODEL)
    tm = _pick(512, seq)
    cos_t, sin_t = _rope_tables(seq)
    e = even
    q, k, v, mq, mk, mv, mo, g = _ab_in(x, cos_t, sin_t, e, seq=seq, tm=tm)
    ya = _attn(q, k, v, batch=batch, seq=seq, tq=_pick(SCORE_ELEMS // seq, seq),
               pairs=min(MLA_HEADS // 2, max(1, KV_BLOCK_ELEMS // (seq * 2 * LANES))))
    c_rows, dec, cols = _gates(g, batch=batch, seq=seq)
    hf, hb = _mlstm(mq, mk, mv, c_rows, dec, cols, batch=batch, seq=seq)
    f = ffn[0]
    x = _ab_out_ffn(x, ya, hf, hb, mo, e["on"], e["wa"], e["wm"], f["nrm"], f["wup"], f["cw"], f["cb"], f["wdn"],
                    final_norm, seq=seq, tm=tm)
    o = odd
    x = _mix_c(x, o["nrm"], o["win"], o["vn"], o["ws"], o["bs"], o["wout"], tm=tm)
    f = ffn[1]
    x = _ffn(x, f["nrm"], f["wup"], f["cw"], f["cb"], f["wdn"], final_norm, seq=seq, tm=tm, final=True)
    return x.reshape(batch, seq, D_MODEL)


def _prep_rest(c_norm, c_w_in, c_v_norm, c_w_spatial, c_b_spatial, c_w_out, ffn_norm, ffn_w_up, ffn_conv_w,
               ffn_conv_b, ffn_w_down):
    odd = dict(nrm=c_norm[0][None, :], win=c_w_in[0].astype(BF16), vn=c_v_norm[0][None, :],
               ws=c_w_spatial[0].astype(BF16), bs=c_b_spatial[0].T, wout=c_w_out[0].astype(BF16))
    ffn = [dict(nrm=ffn_norm[l][None, :], wup=ffn_w_up[l].astype(BF16), cw=ffn_conv_w[l], cb=ffn_conv_b[l][None, :],
                wdn=ffn_w_down[l].astype(BF16)) for l in range(2)]
    return odd, ffn


def kernel(x_prompt, x_sample, ab_norm, ab_w_in, mla_q_norm, mla_w_uq, mla_kv_norm, mla_w_ukv, mlstm_conv_w,
           mlstm_conv_b, mlstm_gate_bias, mlstm_out_norm, ab_w_out, c_norm, c_w_in, c_v_norm, c_w_spatial,
           c_b_spatial, c_w_out, ffn_norm, ffn_w_up, ffn_conv_w, ffn_conv_b, ffn_w_down, final_norm):
    even = _prep_even(0, ab_norm, ab_w_in, mla_q_norm, mla_w_uq, mla_kv_norm, mla_w_ukv, mlstm_conv_w, mlstm_conv_b,
                      mlstm_gate_bias, mlstm_out_norm, ab_w_out)
    odd, ffn = _prep_rest(c_norm, c_w_in, c_v_norm, c_w_spatial, c_b_spatial, c_w_out, ffn_norm, ffn_w_up,
                          ffn_conv_w, ffn_conv_b, ffn_w_down)
    fin = final_norm[None, :]
    return (_forward(x_prompt, even, odd, ffn, fin), _forward(x_sample, even, odd, ffn, fin))
```

```python
import functools

import jax
import jax.numpy as jnp
from jax import lax
from jax.experimental import pallas as pl
from jax.experimental.pallas import tpu as pltpu

F32 = jnp.float32
BF16 = jnp.bfloat16

EPS = 1e-6
D_MODEL = 1024
MLA_HEADS = 8
Q_LORA = 256
KV_LORA = 128
QK_NOPE = 64
QK_ROPE = 32
V_HEAD = 64
ROPE_THETA = 10000.0
M_HEADS = 4
M_HEAD_DIM = 128
M_WIDTH = M_HEADS * M_HEAD_DIM
N_STREAMS = 2 * M_HEADS
CHUNK = 128
G_GROUPS = 8
D_FF = 2816
LANES = 128
SUBLANES = 8
HALO = SUBLANES
BLOCK_CHUNKS = 4
PAIR = 2 * SUBLANES
MXU_COLS = 256
ROW_PITCH = 2
FF_COLS = MXU_COLS
KEY_TILE = MXU_COLS
ATTN_TQ = 512
ATTN_VMEM = 48 * 1024 * 1024
LOG2_E = 1.4426950408889634
VMEM_LIMIT = 56 * 1024 * 1024

_C_Q, _C_KV, _C_KRA, _C_KRB, _C_G, _C_MQK, _C_MV, _C_MO, _C_END = (
    0, 256, 384, 512, 640, 768, 1792, 2304, 2816)


def _rms(x, g):
    return x * lax.rsqrt(jnp.mean(x * x, axis=-1, keepdims=True) + EPS) * g


def _const_spec(shape):
    nd = len(shape)
    return pl.BlockSpec(shape, lambda *_: (0,) * nd, pipeline_mode=pl.Buffered(1))


def _params(*sem):
    return pltpu.CompilerParams(dimension_semantics=sem, vmem_limit_bytes=VMEM_LIMIT)


def _dot(a, b):
    return jnp.dot(a, b, preferred_element_type=F32)


def _dot_nt(a, b):
    return lax.dot_general(a, b, (((1,), (1,)), ((), ())), preferred_element_type=F32)


def _halo_specs(tile_rows, n_rows, width, halo=HALO):
    tb = tile_rows // halo
    last = n_rows // halo - 1
    return (pl.BlockSpec((halo, width), lambda i: (jnp.maximum(i * tb - 1, 0), 0)),
            pl.BlockSpec((halo, width), lambda i: (jnp.minimum((i + 1) * tb, last), 0)))


def _conv3(y, u, rows, w_ref, b_ref, c0):
    tall = rows + 2 * HALO
    outs = []
    for s in range(y.shape[1] // LANES):
        lanes = slice(s * LANES, (s + 1) * LANES)
        cols = slice(c0 + s * LANES, c0 + (s + 1) * LANES)
        u[s, pl.ds(0, tall, stride=ROW_PITCH), :] = y[:, lanes]
        prv = u[s, pl.ds(ROW_PITCH * (HALO - 1), rows, stride=ROW_PITCH), :]
        nxt = u[s, pl.ds(ROW_PITCH * (HALO + 1), rows, stride=ROW_PITCH), :]
        outs.append(prv * w_ref[0:1, cols] + y[HALO:HALO + rows, lanes] * w_ref[1:2, cols]
                    + nxt * w_ref[2:3, cols] + b_ref[:, cols])
    return jnp.concatenate(outs, axis=1)


def _ab_in_kernel(x_ref, xp_ref, xn_ref, cos_ref, sin_ref, nrm_ref, w_ref, qn_ref, wqa_ref, wqb_ref, kvn_ref,
                  wka_ref, wv_ref, gb_ref, vone_ref, cw_ref, cb_ref,
                  q_ref, k_ref, v_ref, mq_ref, mk_ref, mv_ref, mo_ref, g_ref, us, *, tm, tps):
    jj = pl.program_id(0) % tps
    xa = jnp.concatenate([
        jnp.where(jj == 0, 0.0, xp_ref[...]),
        x_ref[...],
        jnp.where(jj == tps - 1, 0.0, xn_ref[...]),
    ], axis=0)
    xe = _rms(xa, nrm_ref[...]).astype(BF16)
    xn = xe[HALO:HALO + tm]

    def proj(a, b):
        return _dot(xn, w_ref[:, a:b])

    cq = proj(_C_Q, _C_KV)
    ckv_kra = proj(_C_KV, _C_KRB)
    krb_g = proj(_C_KRB, _C_MQK)

    nq_blocks = M_WIDTH // MXU_COLS
    for cb in range(2 * nq_blocks):
        cols = slice(cb * MXU_COLS, (cb + 1) * MXU_COLS)
        y = _conv3(_dot(xe, w_ref[:, _C_MQK + cb * MXU_COLS:_C_MQK + (cb + 1) * MXU_COLS]), us.at[cb % 2], tm,
                   cw_ref, cb_ref, cb * MXU_COLS)
        y = y * jax.nn.sigmoid(y)
        if cb < nq_blocks:
            mq_ref[:, cols] = y.astype(BF16)
        else:
            mk_ref[:, (cb - nq_blocks) * MXU_COLS:(cb - nq_blocks + 1) * MXU_COLS] = (
                y * (M_HEAD_DIM ** -0.5)).astype(BF16)
    mv_ref[...] = proj(_C_MV, _C_MO).astype(BF16)
    mo_ref[...] = proj(_C_MO, _C_END)
    gt = krb_g[:, LANES:2 * LANES].T
    for c in range(tm // CHUNK):
        g_ref[c] = gt[0:2 * N_STREAMS, c * CHUNK:(c + 1) * CHUNK] + gb_ref[:, 0:1]

    cos = cos_ref[...]
    sin = sin_ref[...]
    scale = (QK_NOPE + QK_ROPE) ** -0.5 * LOG2_E
    cqn = _rms(cq, qn_ref[...]).astype(BF16)
    qa = _dot(cqn, wqa_ref[...])
    qb = _dot(cqn, wqb_ref[...])
    cos_q = cos * scale
    sin_q = sin * scale
    for h in range(MLA_HEADS):
        blk = slice(h * LANES, (h + 1) * LANES)
        q_ref[:, blk] = (qa[:, blk] * cos_q + qb[:, blk] * sin_q).astype(BF16)

    ckvn = _rms(ckv_kra[:, 0:LANES], kvn_ref[...]).astype(BF16)
    ka = _dot(ckvn, wka_ref[...])
    kr = ckv_kra[:, LANES:2 * LANES] * cos + krb_g[:, 0:LANES] * sin
    for h in range(MLA_HEADS):
        blk = slice(h * LANES, (h + 1) * LANES)
        k_ref[:, blk] = (ka[:, blk] + kr).astype(BF16)
    v_ref[...] = (_dot(ckvn, wv_ref[...]) + vone_ref[...]).astype(BF16)


def _ab_in(x, cos_t, sin_t, e, *, seq, tm):
    n = x.shape[0]
    nt = n // tm
    tps = seq // tm
    row = lambda i: (i, 0)
    tab = lambda i: (i % tps, 0)
    consts = [e[k] for k in ("nrm", "w_big", "qn", "wqa", "wqb", "kvn", "wka", "wv", "gb", "vone", "conv_w",
                             "conv_b")]
    out_shape = (
        jax.ShapeDtypeStruct((n, MLA_HEADS * LANES), BF16),
        jax.ShapeDtypeStruct((n, MLA_HEADS * LANES), BF16),
        jax.ShapeDtypeStruct((n, MLA_HEADS * LANES), BF16),
        jax.ShapeDtypeStruct((n, M_WIDTH), BF16),
        jax.ShapeDtypeStruct((n, M_WIDTH), BF16),
        jax.ShapeDtypeStruct((n, M_WIDTH), BF16),
        jax.ShapeDtypeStruct((n, M_WIDTH), F32),
        jax.ShapeDtypeStruct((n // CHUNK, 2 * N_STREAMS, CHUNK), F32),
    )
    wide = pl.BlockSpec((tm, MLA_HEADS * LANES), row)
    narrow = pl.BlockSpec((tm, M_WIDTH), row)
    return pl.pallas_call(
        functools.partial(_ab_in_kernel, tm=tm, tps=tps),
        grid=(nt,),
        in_specs=[pl.BlockSpec((tm, D_MODEL), row), *_halo_specs(tm, n, D_MODEL),
                  pl.BlockSpec((tm, LANES), tab), pl.BlockSpec((tm, LANES), tab)]
                 + [_const_spec(c.shape) for c in consts],
        out_specs=(wide, wide, wide, narrow, narrow, narrow, narrow,
                   pl.BlockSpec((tm // CHUNK, 2 * N_STREAMS, CHUNK), lambda i: (i, 0, 0))),
        out_shape=out_shape,
        scratch_shapes=[pltpu.VMEM((2, MXU_COLS // LANES, ROW_PITCH * (tm + 2 * HALO), LANES), F32)],
        compiler_params=_params("parallel"),
        name="ab_in",
    )(x, x, x, cos_t, sin_t, *consts)


def _attn_kernel(q_ref, k_ref, v_ref, o_ref, s_scr, *, tq, pairs):
    nt = k_ref.shape[0] // KEY_TILE
    lane = lax.broadcasted_iota(jnp.int32, (tq, LANES), 1)
    for pr in range(pairs):
        mbs = []
        for h in range(2):
            hc = slice((2 * pr + h) * LANES, (2 * pr + h + 1) * LANES)
            q = q_ref[:, hc]
            mx = jnp.full((tq, LANES), -jnp.inf, F32)
            for t in range(nt):
                keys = slice(t * KEY_TILE, (t + 1) * KEY_TILE)
                s = _dot_nt(q, k_ref[keys, hc])
                s_scr[h, :, keys] = s
                for j in range(KEY_TILE // LANES):
                    mx = jnp.maximum(mx, s[:, j * LANES:(j + 1) * LANES])
            mbs.append(jnp.broadcast_to(jnp.max(mx, axis=-1, keepdims=True), (tq, LANES)))
        outs = []
        for h in range(2):
            hc = slice((2 * pr + h) * LANES, (2 * pr + h + 1) * LANES)
            mb = jnp.concatenate([mbs[h]] * (KEY_TILE // LANES), axis=1)
            acc = jnp.zeros((tq, LANES), F32)
            for t in range(nt):
                keys = slice(t * KEY_TILE, (t + 1) * KEY_TILE)
                p = jnp.exp2(s_scr[h, :, keys] - mb).astype(BF16)
                acc = acc + _dot(p, v_ref[keys, hc])
            ones_lane = V_HEAD if h == 0 else 0
            outs.append(acc / acc[:, ones_lane:ones_lane + 1])
        o_ref[:, pr * LANES:(pr + 1) * LANES] = jnp.where(lane < V_HEAD, outs[0], outs[1]).astype(BF16)


def _attn(q, k, v, *, batch, seq, tq, pairs):
    n = q.shape[0]
    nq = seq // tq
    steps = MLA_HEADS // 2 // pairs
    return pl.pallas_call(
        functools.partial(_attn_kernel, tq=tq, pairs=pairs),
        grid=(batch, steps, nq),
        in_specs=[
            pl.BlockSpec((tq, 2 * pairs * LANES), lambda b, p, i: (b * nq + i, p)),
            pl.BlockSpec((seq, 2 * pairs * LANES), lambda b, p, i: (b, p)),
            pl.BlockSpec((seq, 2 * pairs * LANES), lambda b, p, i: (b, p)),
        ],
        out_specs=pl.BlockSpec((tq, pairs * LANES), lambda b, p, i: (b * nq + i, p)),
        out_shape=jax.ShapeDtypeStruct((n, MLA_HEADS * V_HEAD), BF16),
        scratch_shapes=[pltpu.VMEM((2, tq, seq), F32)],
        compiler_params=_params("parallel", "parallel", "arbitrary"),
        name="attn",
    )(q, k, v)


def _log_sigmoid(x):
    return jnp.minimum(x, 0.0) - jnp.log1p(jnp.exp(-jnp.abs(x)))


def _gates_kernel(g_ref, c_ref, dec_ref, cols_ref, tot_s, mloc_s, mpf_s, mpb_s, *, nc):
    rows = nc * N_STREAMS
    shape = (rows, CHUNK)
    three = (nc, N_STREAMS, CHUNK)
    li = g_ref[:, 0:N_STREAMS, :].reshape(shape)
    lf = _log_sigmoid(g_ref[:, N_STREAMS:2 * N_STREAMS, :].reshape(shape))
    fwd = (lax.broadcasted_iota(jnp.int32, shape, 0) & M_HEADS) == 0
    lane = lax.broadcasted_iota(jnp.int32, shape, 1)
    shifts = (1, 2, 4, 8, 16, 32, 64)

    pre = lf
    for sh in shifts:
        pre = pre + jnp.where(lane >= sh, pltpu.roll(pre, sh, axis=1), 0.0)
    tot = jnp.broadcast_to(pre[:, CHUNK - 1:CHUNK], shape)
    b = jnp.where(fwd, pre, tot - pre + lf)
    c = li - b
    run_f = c
    run_b = c
    for sh in shifts:
        run_f = jnp.maximum(run_f, jnp.where(lane >= sh, pltpu.roll(run_f, sh, axis=1), -jnp.inf))
        run_b = jnp.maximum(run_b, jnp.where(lane < CHUNK - sh, pltpu.roll(run_b, CHUNK - sh, axis=1), -jnp.inf))
    run = jnp.where(fwd, run_f, run_b)
    m_loc = tot + jnp.broadcast_to(jnp.max(c, axis=1, keepdims=True), shape)

    tot_s[...] = tot.reshape(three)
    mloc_s[...] = m_loc.reshape(three)
    fwd8 = lax.broadcasted_iota(jnp.int32, (N_STREAMS, CHUNK), 0) < M_HEADS

    def step(i, m):
        jb = nc - 1 - i
        mpf_s[i] = m
        mpb_s[jb] = m
        return jnp.maximum(jnp.where(fwd8, tot_s[i], tot_s[jb]) + m, jnp.where(fwd8, mloc_s[i], mloc_s[jb]))

    lax.fori_loop(0, nc, step, jnp.zeros((N_STREAMS, CHUNK), F32))
    m_prev = jnp.where(fwd, mpf_s[...].reshape(shape), mpb_s[...].reshape(shape))

    m_run = jnp.maximum(m_prev, run)
    floor = jnp.exp(-(b + m_run))
    w = jnp.exp(tot + c - m_loc)
    m_new = jnp.maximum(tot + m_prev, m_loc)
    c_ref[:, 0:N_STREAMS, :] = c.reshape(three)
    c_ref[:, N_STREAMS:2 * N_STREAMS, :] = w.reshape(three)
    c_ref[:, 2 * N_STREAMS:3 * N_STREAMS, :] = m_prev.reshape(three)
    dec_ref[:, 0:N_STREAMS, :] = jnp.exp(tot + m_prev - m_new).reshape(three)
    dec_ref[:, N_STREAMS:2 * N_STREAMS, :] = jnp.exp(m_loc - m_new).reshape(three)
    quantities = [a.reshape(three) for a in (m_run, floor)]
    pad = jnp.zeros((CHUNK - BLOCK_CHUNKS * 2 * N_STREAMS, CHUNK), F32)
    for blk in range(nc // BLOCK_CHUNKS):
        tile = jnp.concatenate(
            [a[blk * BLOCK_CHUNKS + cq] for cq in range(BLOCK_CHUNKS) for a in quantities] + [pad], axis=0)
        cols_ref[blk] = tile.T


def _gates(g, *, batch, seq):
    nc = seq // CHUNK
    nchunks = g.shape[0]
    three = (nc, N_STREAMS, CHUNK)
    return pl.pallas_call(
        functools.partial(_gates_kernel, nc=nc),
        grid=(batch,),
        in_specs=[pl.BlockSpec((nc, 2 * N_STREAMS, CHUNK), lambda b: (b, 0, 0))],
        out_specs=(
            pl.BlockSpec((nc, 3 * N_STREAMS, CHUNK), lambda b: (b, 0, 0)),
            pl.BlockSpec((nc, 2 * N_STREAMS, CHUNK), lambda b: (b, 0, 0)),
            pl.BlockSpec((nc // BLOCK_CHUNKS, CHUNK, LANES), lambda b: (b, 0, 0)),
        ),
        out_shape=(
            jax.ShapeDtypeStruct((nchunks, 3 * N_STREAMS, CHUNK), F32),
            jax.ShapeDtypeStruct((nchunks, 2 * N_STREAMS, CHUNK), F32),
            jax.ShapeDtypeStruct((nchunks // BLOCK_CHUNKS, CHUNK, LANES), F32),
        ),
        scratch_shapes=[pltpu.VMEM(three, F32)] * 4,
        compiler_params=_params("parallel"),
        name="gates",
    )(g)


def _mlstm_kernel(*refs):
    ins, (hf_ref, hb_ref, st, bc) = (refs[0:6], refs[6:12]), refs[12:]

    @pl.when(pl.program_id(1) == 0)
    def _():
        st[...] = jnp.zeros_like(st)

    row = lax.broadcasted_iota(jnp.int32, (CHUNK, CHUNK), 0)
    col = lax.broadcasted_iota(jnp.int32, (CHUNK, CHUNK), 1)
    masks = (col <= row, col >= row)
    ones = (col == 0).astype(BF16)
    orders = (range(BLOCK_CHUNKS), range(BLOCK_CHUNKS - 1, -1, -1))
    heads = range(M_HEADS)
    hcs = [slice(hh * LANES, (hh + 1) * LANES) for hh in heads]
    for d in range(2):
        cols_t = ins[d][5][0]
        for cq in range(BLOCK_CHUNKS):
            for hh in heads:
                base = cq * 2 * N_STREAMS + d * M_HEADS + hh
                for qi in range(2):
                    lane_i = base + qi * N_STREAMS
                    bc[d, (qi * BLOCK_CHUNKS + cq) * M_HEADS + hh] = jnp.broadcast_to(
                        cols_t[:, lane_i:lane_i + 1], (CHUNK, LANES))
    qs, v1s, qk, c_loc = {}, {}, {}, {}
    for d in range(2):
        q_ref, k_ref = ins[d][0], ins[d][1]
        for cq in orders[d]:
            rows = slice(cq * CHUNK, (cq + 1) * CHUNK)
            for hh in heads:
                qs[d, cq, hh] = q_ref[rows, hcs[hh]]
                qk[d, cq, hh] = _dot_nt(qs[d, cq, hh], k_ref[rows, hcs[hh]])
    for d in range(2):
        k_ref, v_ref, c_ref = ins[d][1], ins[d][2], ins[d][3]
        for cq in orders[d]:
            rows = slice(cq * CHUNK, (cq + 1) * CHUNK)
            for hh in heads:
                s = d * M_HEADS + hh
                v1s[d, cq, hh] = jnp.concatenate([v_ref[rows, hcs[hh]], ones], axis=1)
                kw_t = (k_ref[rows, hcs[hh]].astype(F32).T
                        * c_ref[cq, N_STREAMS + s:N_STREAMS + s + 1, :]).astype(BF16)
                c_loc[d, cq, hh] = _dot(kw_t, v1s[d, cq, hh])
    for step in range(BLOCK_CHUNKS):
        for d in range(2):
            _mlstm_chunk(d, orders[d][step], ins[d][3], ins[d][4], (hf_ref, hb_ref)[d], st, bc, masks[d], hcs,
                         qs, v1s, qk, c_loc)


def _mlstm_chunk(d, cq, c_ref, dec_ref, h_ref, st, bc, mask, hcs, qs, v1s, qk, c_loc):
    rows = slice(cq * CHUNK, (cq + 1) * CHUNK)
    for hh in range(M_HEADS):
        s = d * M_HEADS + hh
        key = (d, cq, hh)
        state = st[d, hh]
        m_run = bc[d, cq * M_HEADS + hh]
        floor = bc[d, (BLOCK_CHUNKS + cq) * M_HEADS + hh]
        c_ext = jnp.concatenate([c_ref[cq, s:s + 1, :], c_ref[cq, 2 * N_STREAMS + s:2 * N_STREAMS + s + 1, :]],
                                axis=1)
        d_ext = jnp.exp(c_ext - jnp.concatenate([m_run, m_run], axis=1))
        p_ext = jnp.concatenate([qk[key] * jnp.where(mask, d_ext[:, 0:LANES], 0.0),
                                 qs[key].astype(F32) * d_ext[:, LANES:2 * LANES]], axis=1).astype(BF16)
        nd = _dot(p_ext, jnp.concatenate([v1s[key], state.astype(BF16)], axis=0))
        den = nd[:, LANES:LANES + 1]
        h_ref[rows, hcs[hh]] = nd[:, 0:LANES] / jnp.maximum(jnp.abs(den), floor)
        st[d, hh] = (dec_ref[cq, s:s + 1, 0:1] * state
                     + dec_ref[cq, N_STREAMS + s:N_STREAMS + s + 1, 0:1] * c_loc[key])


def _mlstm(mq, mk, mv, c_rows, dec, cols, *, batch, seq):
    n = mq.shape[0]
    ts = BLOCK_CHUNKS * CHUNK
    nblk = seq // ts

    def specs(direction):
        def blk(b, i):
            return b * nblk + (i if direction == 0 else nblk - 1 - i)

        tok = pl.BlockSpec((ts, M_WIDTH), lambda b, i: (blk(b, i), 0))
        return tok, [tok, tok, tok,
                     pl.BlockSpec((BLOCK_CHUNKS, 3 * N_STREAMS, CHUNK), lambda b, i: (blk(b, i), 0, 0)),
                     pl.BlockSpec((BLOCK_CHUNKS, 2 * N_STREAMS, CHUNK), lambda b, i: (blk(b, i), 0, 0)),
                     pl.BlockSpec((1, CHUNK, LANES), lambda b, i: (blk(b, i), 0, 0))]

    (tok_f, in_f), (tok_b, in_b) = specs(0), specs(1)
    h_shape = jax.ShapeDtypeStruct((n, M_WIDTH), F32)
    args = (mq, mk, mv, c_rows, dec, cols)
    return pl.pallas_call(
        _mlstm_kernel,
        grid=(batch, nblk),
        in_specs=in_f + in_b,
        out_specs=(tok_f, tok_b),
        out_shape=(h_shape, h_shape),
        scratch_shapes=[pltpu.VMEM((2, M_HEADS, M_HEAD_DIM, 2 * LANES), F32),
                        pltpu.VMEM((2, 2 * BLOCK_CHUNKS * M_HEADS, CHUNK, LANES), F32)],
        compiler_params=_params("parallel", "arbitrary"),
        name="mlstm",
    )(*args, *args)


def _ffn_body(xa, nrm_ref, wup_ref, cw_ref, cb_ref, wdn_ref, fin_ref, us, act, *, tm, final):
    xn = _rms(xa, nrm_ref[...]).astype(BF16)
    for c in range(D_FF // FF_COLS):
        g0 = c * FF_COLS
        v0 = D_FF + c * FF_COLS
        gate = _conv3(_dot(xn, wup_ref[:, g0:g0 + FF_COLS]), us.at[0], tm, cw_ref, cb_ref, g0)
        val = _conv3(_dot(xn, wup_ref[:, v0:v0 + FF_COLS]), us.at[1], tm, cw_ref, cb_ref, v0)
        act[:, g0:g0 + FF_COLS] = (gate * jax.nn.sigmoid(gate) * val).astype(BF16)
    y = xa[HALO:HALO + tm] + _dot(act[...], wdn_ref[...])
    if final:
        y = _rms(y, fin_ref[...])
    return y


def _ffn_kernel(x_ref, xp_ref, xn_ref, nrm_ref, wup_ref, cw_ref, cb_ref, wdn_ref, fin_ref, o_ref, us, act,
                *, tm, tps, final):
    jj = pl.program_id(0) % tps
    xa = jnp.concatenate([
        jnp.where(jj == 0, 0.0, xp_ref[...]),
        x_ref[...],
        jnp.where(jj == tps - 1, 0.0, xn_ref[...]),
    ], axis=0)
    o_ref[...] = _ffn_body(xa, nrm_ref, wup_ref, cw_ref, cb_ref, wdn_ref, fin_ref, us, act, tm=tm, final=final)


def _ffn_scratch(tm):
    return [pltpu.VMEM((2, FF_COLS // LANES, ROW_PITCH * (tm + 2 * HALO), LANES), F32),
            pltpu.VMEM((tm, D_FF), BF16)]


def _ab_out_ffn_kernel(x_ref, xp_ref, xn_ref, ya_ref, yap_ref, yan_ref, hf_ref, hfp_ref, hfn_ref, hb_ref, hbp_ref,
                       hbn_ref, mo_ref, mop_ref, mon_ref, on_ref, wa_ref, wm_ref, nrm_ref, wup_ref, cw_ref, cb_ref,
                       wdn_ref, fin_ref, o_ref, us, act, *, tm, tps):
    jj = pl.program_id(0) % tps

    def rows(main, prev, nxt):
        return jnp.concatenate([main[...], prev[...], nxt[...]], axis=0)

    hf, hb, mo = rows(hf_ref, hfp_ref, hfn_ref), rows(hb_ref, hbp_ref, hbn_ref), rows(mo_ref, mop_ref, mon_ref)
    parts = []
    for hh in range(M_HEADS):
        hc = slice(hh * LANES, (hh + 1) * LANES)
        t = hf[:, hc] + hb[:, hc]
        t = t * lax.rsqrt(jnp.mean(t * t, axis=-1, keepdims=True) + EPS) * on_ref[:, hc]
        parts.append((jax.nn.sigmoid(mo[:, hc]) * t).astype(BF16))
    ym = jnp.concatenate(parts, axis=1)
    x1 = (rows(x_ref, xp_ref, xn_ref) + _dot(rows(ya_ref, yap_ref, yan_ref), wa_ref[...]) + _dot(ym, wm_ref[...]))
    xa = jnp.concatenate([
        jnp.where(jj == 0, 0.0, x1[tm + PAIR - HALO:tm + PAIR]),
        x1[0:tm],
        jnp.where(jj == tps - 1, 0.0, x1[tm + PAIR:tm + PAIR + HALO]),
    ], axis=0)
    o_ref[...] = _ffn_body(xa, nrm_ref, wup_ref, cw_ref, cb_ref, wdn_ref, fin_ref, us, act, tm=tm, final=False)


def _ab_out_ffn(x, ya, hf, hb, mo, on, wa, wm, nrm, wup, cw, cb, wdn, fin, *, seq, tm):
    n = x.shape[0]
    row = lambda i: (i, 0)

    def with_halo(width):
        return [pl.BlockSpec((tm, width), row), *_halo_specs(tm, n, width, PAIR)]

    consts = [on, wa, wm, nrm, wup, cw, cb, wdn, fin]
    return pl.pallas_call(
        functools.partial(_ab_out_ffn_kernel, tm=tm, tps=seq // tm),
        grid=(n // tm,),
        in_specs=with_halo(D_MODEL) + 4 * with_halo(M_WIDTH) + [_const_spec(c.shape) for c in consts],
        out_specs=pl.BlockSpec((tm, D_MODEL), row),
        out_shape=jax.ShapeDtypeStruct((n, D_MODEL), F32),
        scratch_shapes=_ffn_scratch(tm),
        compiler_params=_params("parallel"),
        name="ab_out_ffn",
    )(x, x, x, ya, ya, ya, hf, hf, hf, hb, hb, hb, mo, mo, mo, *consts)


def _ffn(x, nrm, wup, cw, cb, wdn, fin, *, seq, tm, final):
    n = x.shape[0]
    row = lambda i: (i, 0)
    return pl.pallas_call(
        functools.partial(_ffn_kernel, tm=tm, tps=seq // tm, final=final),
        grid=(n // tm,),
        in_specs=[
            pl.BlockSpec((tm, D_MODEL), row), *_halo_specs(tm, n, D_MODEL),
            _const_spec(nrm.shape), _const_spec(wup.shape), _const_spec(cw.shape), _const_spec(cb.shape),
            _const_spec(wdn.shape), _const_spec(fin.shape),
        ],
        out_specs=pl.BlockSpec((tm, D_MODEL), row),
        out_shape=jax.ShapeDtypeStruct((n, D_MODEL), F32),
        scratch_shapes=_ffn_scratch(tm),
        compiler_params=_params("parallel"),
        name="ffn_final" if final else "ffn",
    )(x, x, x, nrm, wup, cw, cb, wdn, fin)


def _gelu(x):
    return 0.5 * x * (1.0 + lax.erf(x * (2.0 ** -0.5)))


def _mix_c_kernel(x_ref, nrm_ref, win_ref, vn_ref, ws_ref, bs_ref, wout_ref, o_ref, zs, *, tm):
    x = x_ref[...]
    xn = _rms(x, nrm_ref[...]).astype(BF16)
    v = _gelu(_dot(xn, win_ref[:, D_MODEL:2 * D_MODEL]))
    u = _gelu(_dot(xn, win_ref[:, 0:D_MODEL]))
    vn = _rms(v, vn_ref[...]).astype(BF16)
    nch = tm // CHUNK
    for g in range(G_GROUPS):
        cols = slice(g * LANES, (g + 1) * LANES)
        rhs = jnp.concatenate([vn[c * CHUNK:(c + 1) * CHUNK, cols] for c in range(nch)], axis=1)
        sv = _dot(ws_ref[g], rhs) + bs_ref[:, g:g + 1]
        for c in range(nch):
            zs[c * CHUNK:(c + 1) * CHUNK, cols] = (
                u[c * CHUNK:(c + 1) * CHUNK, cols] * sv[:, c * LANES:(c + 1) * LANES]).astype(BF16)
    o_ref[...] = x + _dot(zs[...], wout_ref[...])


def _mix_c(x, nrm, win, vn, ws, bs, wout, *, tm):
    n = x.shape[0]
    row = lambda i: (i, 0)
    return pl.pallas_call(
        functools.partial(_mix_c_kernel, tm=tm),
        grid=(n // tm,),
        in_specs=[
            pl.BlockSpec((tm, D_MODEL), row),
            _const_spec(nrm.shape), _const_spec(win.shape), _const_spec(vn.shape), _const_spec(ws.shape),
            _const_spec(bs.shape), _const_spec(wout.shape),
        ],
        out_specs=pl.BlockSpec((tm, D_MODEL), row),
        out_shape=jax.ShapeDtypeStruct((n, D_MODEL), F32),
        scratch_shapes=[pltpu.VMEM((tm, D_MODEL), BF16)],
        compiler_params=_params("parallel"),
        name="mix_c",
    )(x, nrm, win, vn, ws, bs, wout)


def _rope_tables(seq):
    pos = jnp.arange(seq, dtype=F32)
    inv = 1.0 / (ROPE_THETA ** (jnp.arange(0, QK_ROPE, 2, dtype=F32) / QK_ROPE))
    ang = pos[:, None] * inv[None, :]
    cos, sin = jnp.cos(ang), jnp.sin(ang)
    one = jnp.ones((seq, QK_NOPE), F32)
    zero_n = jnp.zeros((seq, QK_NOPE), F32)
    pad = jnp.zeros((seq, LANES - QK_NOPE - QK_ROPE), F32)
    return (jnp.concatenate([one, cos, cos, pad], axis=1), jnp.concatenate([zero_n, sin, sin, pad], axis=1))


def _head_block(nope, x1, x2):
    pad = jnp.zeros((nope.shape[0], LANES - QK_NOPE - QK_ROPE), nope.dtype)
    return jnp.concatenate([nope, x1, x2, pad], axis=1)


def _prep_even(i, ab_norm, ab_w_in, mla_q_norm, mla_w_uq, mla_kv_norm, mla_w_ukv, mlstm_conv_w, mlstm_conv_b,
               mlstm_gate_bias, mlstm_out_norm, ab_w_out):
    w_in = ab_w_in[i]
    half = QK_ROPE // 2
    zq = jnp.zeros((D_MODEL, QK_NOPE), F32)
    kr1 = w_in[:, 384:384 + half]
    kr2 = w_in[:, 384 + half:416]
    gate_order = jnp.array([0, 1, 2, 3, 8, 9, 10, 11, 4, 5, 6, 7, 12, 13, 14, 15])
    gates = jnp.pad(w_in[:, 2464:2480][:, gate_order], ((0, 0), (0, LANES - 2 * N_STREAMS)))
    w_big = jnp.concatenate([
        w_in[:, 0:384],
        _head_block(zq, kr1, kr2),
        _head_block(zq, -kr2, kr1),
        gates,
        w_in[:, 416:2464],
    ], axis=1).astype(BF16)
    w_uq = mla_w_uq[i].reshape(Q_LORA, MLA_HEADS, QK_NOPE + QK_ROPE)
    zn = jnp.zeros((Q_LORA, QK_NOPE), F32)
    wqa = jnp.concatenate([_head_block(w_uq[:, h, :QK_NOPE], w_uq[:, h, QK_NOPE:QK_NOPE + half],
                                       w_uq[:, h, QK_NOPE + half:]) for h in range(MLA_HEADS)], axis=1)
    wqb = jnp.concatenate([_head_block(zn, -w_uq[:, h, QK_NOPE + half:], w_uq[:, h, QK_NOPE:QK_NOPE + half])
                           for h in range(MLA_HEADS)], axis=1)
    w_ukv = mla_w_ukv[i].reshape(KV_LORA, MLA_HEADS, QK_NOPE + V_HEAD)
    wka = jnp.pad(w_ukv[:, :, :QK_NOPE], ((0, 0), (0, 0), (0, LANES - QK_NOPE))).reshape(KV_LORA, MLA_HEADS * LANES)
    zv = jnp.zeros((KV_LORA, V_HEAD), F32)
    wv = jnp.concatenate([jnp.concatenate([w_ukv[:, h, QK_NOPE:], zv] if h % 2 == 0 else [zv, w_ukv[:, h, QK_NOPE:]],
                                          axis=1) for h in range(MLA_HEADS)], axis=1)
    one_even = jnp.zeros((LANES,), F32).at[V_HEAD].set(1.0)
    one_odd = jnp.zeros((LANES,), F32).at[0].set(1.0)
    vone = jnp.concatenate([one_even if h % 2 == 0 else one_odd for h in range(MLA_HEADS)])[None, :]
    w_out = ab_w_out[i].astype(BF16)
    return dict(
        nrm=ab_norm[i][None, :], w_big=w_big, qn=mla_q_norm[i][None, :], wqa=wqa.astype(BF16),
        wqb=wqb.astype(BF16), kvn=mla_kv_norm[i][None, :], wka=wka.astype(BF16), wv=wv.astype(BF16),
        gb=jnp.broadcast_to(mlstm_gate_bias[i][gate_order][:, None], (2 * N_STREAMS, LANES)), vone=vone,
        conv_w=mlstm_conv_w[i], conv_b=mlstm_conv_b[i][None, :], on=mlstm_out_norm[i][None, :],
        wa=w_out[:MLA_HEADS * V_HEAD], wm=w_out[MLA_HEADS * V_HEAD:])


def _pick(pref, seq):
    return min(pref, seq)


def _forward(x3, even, odd, ffn, final_norm):
    batch, seq, _ = x3.shape
    x = x3.reshape(batch * seq, D_MODEL)
    tm = _pick(512, seq)
    cos_t, sin_t = _rope_tables(seq)
    e = even
    q, k, v, mq, mk, mv, mo, g = _ab_in(x, cos_t, sin_t, e, seq=seq, tm=tm)
    tq = _pick(ATTN_TQ, seq)
    score_bytes = 2 * tq * seq * 4
    pair_bytes = 2 * 2 * seq * 2 * LANES * 2
    pairs = min(MLA_HEADS // 2, max(1, (ATTN_VMEM - score_bytes) // pair_bytes))
    while (MLA_HEADS // 2) % pairs:
        pairs -= 1
    ya = _attn(q, k, v, batch=batch, seq=seq, tq=tq, pairs=pairs)
    c_rows, dec, cols = _gates(g, batch=batch, seq=seq)
    hf, hb = _mlstm(mq, mk, mv, c_rows, dec, cols, batch=batch, seq=seq)
    f = ffn[0]
    x = _ab_out_ffn(x, ya, hf, hb, mo, e["on"], e["wa"], e["wm"], f["nrm"], f["wup"], f["cw"], f["cb"], f["wdn"],
                    final_norm, seq=seq, tm=tm)
    o = odd
    x = _mix_c(x, o["nrm"], o["win"], o["vn"], o["ws"], o["bs"], o["wout"], tm=tm)
    f = ffn[1]
    x = _ffn(x, f["nrm"], f["wup"], f["cw"], f["cb"], f["wdn"], final_norm, seq=seq, tm=tm, final=True)
    return x.reshape(batch, seq, D_MODEL)


def _prep_rest(c_norm, c_w_in, c_v_norm, c_w_spatial, c_b_spatial, c_w_out, ffn_norm, ffn_w_up, ffn_conv_w,
               ffn_conv_b, ffn_w_down):
    odd = dict(nrm=c_norm[0][None, :], win=c_w_in[0].astype(BF16), vn=c_v_norm[0][None, :],
               ws=c_w_spatial[0].astype(BF16), bs=c_b_spatial[0].T, wout=c_w_out[0].astype(BF16))
    ffn = [dict(nrm=ffn_norm[l][None, :], wup=ffn_w_up[l].astype(BF16), cw=ffn_conv_w[l], cb=ffn_conv_b[l][None, :],
                wdn=ffn_w_down[l].astype(BF16)) for l in range(2)]
    return odd, ffn


def kernel(x_prompt, x_sample, ab_norm, ab_w_in, mla_q_norm, mla_w_uq, mla_kv_norm, mla_w_ukv, mlstm_conv_w,
           mlstm_conv_b, mlstm_gate_bias, mlstm_out_norm, ab_w_out, c_norm, c_w_in, c_v_norm, c_w_spatial,
           c_b_spatial, c_w_out, ffn_norm, ffn_w_up, ffn_conv_w, ffn_conv_b, ffn_w_down, final_norm):
    even = _prep_even(0, ab_norm, ab_w_in, mla_q_norm, mla_w_uq, mla_kv_norm, mla_w_ukv, mlstm_conv_w, mlstm_conv_b,
                      mlstm_gate_bias, mlstm_out_norm, ab_w_out)
    odd, ffn = _prep_rest(c_norm, c_w_in, c_v_norm, c_w_spatial, c_b_spatial, c_w_out, ffn_norm, ffn_w_up,
                          ffn_conv_w, ffn_conv_b, ffn_w_down)
    fin = final_norm[None, :]
    return (_forward(x_prompt, even, odd, ffn, fin), _forward(x_sample, even, odd, ffn, fin))
```

```python
import functools

import jax
import jax.numpy as jnp
from jax import lax
from jax.experimental import pallas as pl
from jax.experimental.pallas import tpu as pltpu

F32 = jnp.float32
BF16 = jnp.bfloat16

EPS = 1e-6
D_MODEL = 1024
MLA_HEADS = 8
Q_LORA = 256
KV_LORA = 128
QK_NOPE = 64
QK_ROPE = 32
V_HEAD = 64
ROPE_THETA = 10000.0
M_HEADS = 4
M_HEAD_DIM = 128
M_WIDTH = M_HEADS * M_HEAD_DIM
N_STREAMS = 2 * M_HEADS
CHUNK = 128
G_GROUPS = 8
D_FF = 2816
LANES = 128
SUBLANES = 8
HALO = SUBLANES
BLOCK_CHUNKS = 4
PAIR = 2 * SUBLANES
MXU_COLS = 256
ROW_PITCH = 2
FF_COLS = MXU_COLS
KEY_TILE = MXU_COLS
ATTN_TQ = 512
ATTN_VMEM = 48 * 1024 * 1024
LOG2_E = 1.4426950408889634
VMEM_LIMIT = 56 * 1024 * 1024

_C_Q, _C_KV, _C_KRA, _C_KRB, _C_G, _C_MQK, _C_MV, _C_MO, _C_END = (
    0, 256, 384, 512, 640, 768, 1792, 2304, 2816)


def _rms(x, g):
    return x * lax.rsqrt(jnp.mean(x * x, axis=-1, keepdims=True) + EPS) * g


def _const_spec(shape):
    nd = len(shape)
    return pl.BlockSpec(shape, lambda *_: (0,) * nd, pipeline_mode=pl.Buffered(1))


def _params(*sem):
    return pltpu.CompilerParams(dimension_semantics=sem, vmem_limit_bytes=VMEM_LIMIT)


def _dot(a, b):
    return jnp.dot(a, b, preferred_element_type=F32)


def _dot_nt(a, b):
    return lax.dot_general(a, b, (((1,), (1,)), ((), ())), preferred_element_type=F32)


def _halo_specs(tile_rows, n_rows, width, halo=HALO):
    tb = tile_rows // halo
    last = n_rows // halo - 1
    return (pl.BlockSpec((halo, width), lambda i: (jnp.maximum(i * tb - 1, 0), 0)),
            pl.BlockSpec((halo, width), lambda i: (jnp.minimum((i + 1) * tb, last), 0)))


def _conv3(y, u, rows, w_ref, b_ref, c0):
    tall = rows + 2 * HALO
    outs = []
    for s in range(y.shape[1] // LANES):
        lanes = slice(s * LANES, (s + 1) * LANES)
        cols = slice(c0 + s * LANES, c0 + (s + 1) * LANES)
        u[s, pl.ds(0, tall, stride=ROW_PITCH), :] = y[:, lanes]
        prv = u[s, pl.ds(ROW_PITCH * (HALO - 1), rows, stride=ROW_PITCH), :]
        nxt = u[s, pl.ds(ROW_PITCH * (HALO + 1), rows, stride=ROW_PITCH), :]
        outs.append(prv * w_ref[0:1, cols] + y[HALO:HALO + rows, lanes] * w_ref[1:2, cols]
                    + nxt * w_ref[2:3, cols] + b_ref[:, cols])
    return jnp.concatenate(outs, axis=1)


def _ab_in_kernel(x_ref, xp_ref, xn_ref, cos_ref, sin_ref, nrm_ref, w_ref, qn_ref, wqa_ref, wqb_ref, kvn_ref,
                  wka_ref, wv_ref, gb_ref, vone_ref, cw_ref, cb_ref,
                  q_ref, k_ref, v_ref, mq_ref, mk_ref, mv_ref, mo_ref, g_ref, us, *, tm, tps):
    jj = pl.program_id(0) % tps
    xa = jnp.concatenate([
        jnp.where(jj == 0, 0.0, xp_ref[...]),
        x_ref[...],
        jnp.where(jj == tps - 1, 0.0, xn_ref[...]),
    ], axis=0)
    xe = _rms(xa, nrm_ref[...]).astype(BF16)
    xn = xe[HALO:HALO + tm]

    def proj(a, b):
        return _dot(xn, w_ref[:, a:b])

    cq = proj(_C_Q, _C_KV)
    ckv_kra = proj(_C_KV, _C_KRB)
    krb_g = proj(_C_KRB, _C_MQK)

    nq_blocks = M_WIDTH // MXU_COLS
    for cb in range(2 * nq_blocks):
        cols = slice(cb * MXU_COLS, (cb + 1) * MXU_COLS)
        y = _conv3(_dot(xe, w_ref[:, _C_MQK + cb * MXU_COLS:_C_MQK + (cb + 1) * MXU_COLS]), us.at[cb % 2], tm,
                   cw_ref, cb_ref, cb * MXU_COLS)
        y = y * jax.nn.sigmoid(y)
        if cb < nq_blocks:
            mq_ref[:, cols] = y.astype(BF16)
        else:
            mk_ref[:, (cb - nq_blocks) * MXU_COLS:(cb - nq_blocks + 1) * MXU_COLS] = (
                y * (M_HEAD_DIM ** -0.5)).astype(BF16)
    mv_ref[...] = proj(_C_MV, _C_MO).astype(BF16)
    mo_ref[...] = proj(_C_MO, _C_END)
    gt = krb_g[:, LANES:2 * LANES].T
    for c in range(tm // CHUNK):
        g_ref[c] = gt[0:2 * N_STREAMS, c * CHUNK:(c + 1) * CHUNK] + gb_ref[:, 0:1]

    cos = cos_ref[...]
    sin = sin_ref[...]
    scale = (QK_NOPE + QK_ROPE) ** -0.5 * LOG2_E
    cqn = _rms(cq, qn_ref[...]).astype(BF16)
    qa = _dot(cqn, wqa_ref[...])
    qb = _dot(cqn, wqb_ref[...])
    cos_q = cos * scale
    sin_q = sin * scale
    for h in range(MLA_HEADS):
        blk = slice(h * LANES, (h + 1) * LANES)
        q_ref[:, blk] = (qa[:, blk] * cos_q + qb[:, blk] * sin_q).astype(BF16)

    ckvn = _rms(ckv_kra[:, 0:LANES], kvn_ref[...]).astype(BF16)
    ka = _dot(ckvn, wka_ref[...])
    kr = ckv_kra[:, LANES:2 * LANES] * cos + krb_g[:, 0:LANES] * sin
    for h in range(MLA_HEADS):
        blk = slice(h * LANES, (h + 1) * LANES)
        k_ref[:, blk] = (ka[:, blk] + kr).astype(BF16)
    v_ref[...] = (_dot(ckvn, wv_ref[...]) + vone_ref[...]).astype(BF16)


def _ab_in(x, cos_t, sin_t, e, *, seq, tm):
    n = x.shape[0]
    nt = n // tm
    tps = seq // tm
    row = lambda i: (i, 0)
    tab = lambda i: (i % tps, 0)
    consts = [e[k] for k in ("nrm", "w_big", "qn", "wqa", "wqb", "kvn", "wka", "wv", "gb", "vone", "conv_w",
                             "conv_b")]
    out_shape = (
        jax.ShapeDtypeStruct((n, MLA_HEADS * LANES), BF16),
        jax.ShapeDtypeStruct((n, MLA_HEADS * LANES), BF16),
        jax.ShapeDtypeStruct((n, MLA_HEADS * LANES), BF16),
        jax.ShapeDtypeStruct((n, M_WIDTH), BF16),
        jax.ShapeDtypeStruct((n, M_WIDTH), BF16),
        jax.ShapeDtypeStruct((n, M_WIDTH), BF16),
        jax.ShapeDtypeStruct((n, M_WIDTH), F32),
        jax.ShapeDtypeStruct((n // CHUNK, 2 * N_STREAMS, CHUNK), F32),
    )
    wide = pl.BlockSpec((tm, MLA_HEADS * LANES), row)
    narrow = pl.BlockSpec((tm, M_WIDTH), row)
    return pl.pallas_call(
        functools.partial(_ab_in_kernel, tm=tm, tps=tps),
        grid=(nt,),
        in_specs=[pl.BlockSpec((tm, D_MODEL), row), *_halo_specs(tm, n, D_MODEL),
                  pl.BlockSpec((tm, LANES), tab), pl.BlockSpec((tm, LANES), tab)]
                 + [_const_spec(c.shape) for c in consts],
        out_specs=(wide, wide, wide, narrow, narrow, narrow, narrow,
                   pl.BlockSpec((tm // CHUNK, 2 * N_STREAMS, CHUNK), lambda i: (i, 0, 0))),
        out_shape=out_shape,
        scratch_shapes=[pltpu.VMEM((2, MXU_COLS // LANES, ROW_PITCH * (tm + 2 * HALO), LANES), F32)],
        compiler_params=_params("parallel"),
        name="ab_in",
    )(x, x, x, cos_t, sin_t, *consts)


def _attn_kernel(q_ref, k_ref, v_ref, o_ref, s_scr, *, tq, pairs):
    nt = k_ref.shape[0] // KEY_TILE
    lane = lax.broadcasted_iota(jnp.int32, (tq, LANES), 1)
    for pr in range(pairs):
        mbs = []
        for h in range(2):
            hc = slice((2 * pr + h) * LANES, (2 * pr + h + 1) * LANES)
            q = q_ref[:, hc]
            mx = jnp.full((tq, LANES), -jnp.inf, F32)
            for t in range(nt):
                keys = slice(t * KEY_TILE, (t + 1) * KEY_TILE)
                s = _dot_nt(q, k_ref[keys, hc])
                s_scr[h, :, keys] = s
                for j in range(KEY_TILE // LANES):
                    mx = jnp.maximum(mx, s[:, j * LANES:(j + 1) * LANES])
            mbs.append(jnp.broadcast_to(jnp.max(mx, axis=-1, keepdims=True), (tq, LANES)))
        outs = []
        for h in range(2):
            hc = slice((2 * pr + h) * LANES, (2 * pr + h + 1) * LANES)
            mb = jnp.concatenate([mbs[h]] * (KEY_TILE // LANES), axis=1)
            acc = jnp.zeros((tq, LANES), F32)
            for t in range(nt):
                keys = slice(t * KEY_TILE, (t + 1) * KEY_TILE)
                p = jnp.exp2(s_scr[h, :, keys] - mb).astype(BF16)
                acc = acc + _dot(p, v_ref[keys, hc])
            ones_lane = V_HEAD if h == 0 else 0
            outs.append(acc / acc[:, ones_lane:ones_lane + 1])
        o_ref[:, pr * LANES:(pr + 1) * LANES] = jnp.where(lane < V_HEAD, outs[0], outs[1]).astype(BF16)


def _attn(q, k, v, *, batch, seq, tq, pairs):
    n = q.shape[0]
    nq = seq // tq
    steps = MLA_HEADS // 2 // pairs
    return pl.pallas_call(
        functools.partial(_attn_kernel, tq=tq, pairs=pairs),
        grid=(batch, steps, nq),
        in_specs=[
            pl.BlockSpec((tq, 2 * pairs * LANES), lambda b, p, i: (b * nq + i, p)),
            pl.BlockSpec((seq, 2 * pairs * LANES), lambda b, p, i: (b, p)),
            pl.BlockSpec((seq, 2 * pairs * LANES), lambda b, p, i: (b, p)),
        ],
        out_specs=pl.BlockSpec((tq, pairs * LANES), lambda b, p, i: (b * nq + i, p)),
        out_shape=jax.ShapeDtypeStruct((n, MLA_HEADS * V_HEAD), BF16),
        scratch_shapes=[pltpu.VMEM((2, tq, seq), F32)],
        compiler_params=_params("parallel", "parallel", "arbitrary"),
        name="attn",
    )(q, k, v)


def _log_sigmoid(x):
    return jnp.minimum(x, 0.0) - jnp.log1p(jnp.exp(-jnp.abs(x)))


def _gates_kernel(g_ref, c_ref, dec_ref, cols_ref, tot_s, mloc_s, mpf_s, mpb_s, *, nc):
    rows = nc * N_STREAMS
    shape = (rows, CHUNK)
    three = (nc, N_STREAMS, CHUNK)
    li = g_ref[:, 0:N_STREAMS, :].reshape(shape)
    lf = _log_sigmoid(g_ref[:, N_STREAMS:2 * N_STREAMS, :].reshape(shape))
    fwd = (lax.broadcasted_iota(jnp.int32, shape, 0) & M_HEADS) == 0
    lane = lax.broadcasted_iota(jnp.int32, shape, 1)
    shifts = (1, 2, 4, 8, 16, 32, 64)

    pre = lf
    for sh in shifts:
        pre = pre + jnp.where(lane >= sh, pltpu.roll(pre, sh, axis=1), 0.0)
    tot = jnp.broadcast_to(pre[:, CHUNK - 1:CHUNK], shape)
    b = jnp.where(fwd, pre, tot - pre + lf)
    c = li - b
    run_f = c
    run_b = c
    for sh in shifts:
        run_f = jnp.maximum(run_f, jnp.where(lane >= sh, pltpu.roll(run_f, sh, axis=1), -jnp.inf))
        run_b = jnp.maximum(run_b, jnp.where(lane < CHUNK - sh, pltpu.roll(run_b, CHUNK - sh, axis=1), -jnp.inf))
    run = jnp.where(fwd, run_f, run_b)
    m_loc = tot + jnp.broadcast_to(jnp.max(c, axis=1, keepdims=True), shape)

    tot_s[...] = tot.reshape(three)
    mloc_s[...] = m_loc.reshape(three)
    fwd8 = lax.broadcasted_iota(jnp.int32, (N_STREAMS, CHUNK), 0) < M_HEADS

    def step(i, m):
        jb = nc - 1 - i
        mpf_s[i] = m
        mpb_s[jb] = m
        return jnp.maximum(jnp.where(fwd8, tot_s[i], tot_s[jb]) + m, jnp.where(fwd8, mloc_s[i], mloc_s[jb]))

    lax.fori_loop(0, nc, step, jnp.zeros((N_STREAMS, CHUNK), F32))
    m_prev = jnp.where(fwd, mpf_s[...].reshape(shape), mpb_s[...].reshape(shape))

    m_run = jnp.maximum(m_prev, run)
    floor = jnp.exp(-(b + m_run))
    w = jnp.exp(tot + c - m_loc)
    m_new = jnp.maximum(tot + m_prev, m_loc)
    c_ref[:, 0:N_STREAMS, :] = c.reshape(three)
    c_ref[:, N_STREAMS:2 * N_STREAMS, :] = w.reshape(three)
    c_ref[:, 2 * N_STREAMS:3 * N_STREAMS, :] = m_prev.reshape(three)
    dec_ref[:, 0:N_STREAMS, :] = jnp.exp(tot + m_prev - m_new).reshape(three)
    dec_ref[:, N_STREAMS:2 * N_STREAMS, :] = jnp.exp(m_loc - m_new).reshape(three)
    quantities = [a.reshape(three) for a in (m_run, floor)]
    pad = jnp.zeros((CHUNK - BLOCK_CHUNKS * 2 * N_STREAMS, CHUNK), F32)
    for blk in range(nc // BLOCK_CHUNKS):
        tile = jnp.concatenate(
            [a[blk * BLOCK_CHUNKS + cq] for cq in range(BLOCK_CHUNKS) for a in quantities] + [pad], axis=0)
        cols_ref[blk] = tile.T


def _gates(g, *, batch, seq):
    nc = seq // CHUNK
    nchunks = g.shape[0]
    three = (nc, N_STREAMS, CHUNK)
    return pl.pallas_call(
        functools.partial(_gates_kernel, nc=nc),
        grid=(batch,),
        in_specs=[pl.BlockSpec((nc, 2 * N_STREAMS, CHUNK), lambda b: (b, 0, 0))],
        out_specs=(
            pl.BlockSpec((nc, 3 * N_STREAMS, CHUNK), lambda b: (b, 0, 0)),
            pl.BlockSpec((nc, 2 * N_STREAMS, CHUNK), lambda b: (b, 0, 0)),
            pl.BlockSpec((nc // BLOCK_CHUNKS, CHUNK, LANES), lambda b: (b, 0, 0)),
        ),
        out_shape=(
            jax.ShapeDtypeStruct((nchunks, 3 * N_STREAMS, CHUNK), F32),
            jax.ShapeDtypeStruct((nchunks, 2 * N_STREAMS, CHUNK), F32),
            jax.ShapeDtypeStruct((nchunks // BLOCK_CHUNKS, CHUNK, LANES), F32),
        ),
        scratch_shapes=[pltpu.VMEM(three, F32)] * 4,
        compiler_params=_params("parallel"),
        name="gates",
    )(g)


def _mlstm_kernel(*refs):
    ins, (hf_ref, hb_ref, st, bc) = (refs[0:6], refs[6:12]), refs[12:]

    @pl.when(pl.program_id(1) == 0)
    def _():
        st[...] = jnp.zeros_like(st)

    row = lax.broadcasted_iota(jnp.int32, (CHUNK, CHUNK), 0)
    col = lax.broadcasted_iota(jnp.int32, (CHUNK, CHUNK), 1)
    masks = (col <= row, col >= row)
    ones = (col == 0).astype(BF16)
    orders = (range(BLOCK_CHUNKS), range(BLOCK_CHUNKS - 1, -1, -1))
    heads = range(M_HEADS)
    hcs = [slice(hh * LANES, (hh + 1) * LANES) for hh in heads]
    for d in range(2):
        cols_t = ins[d][5][0]
        for cq in range(BLOCK_CHUNKS):
            for hh in heads:
                base = cq * 2 * N_STREAMS + d * M_HEADS + hh
                for qi in range(2):
                    lane_i = base + qi * N_STREAMS
                    bc[d, (qi * BLOCK_CHUNKS + cq) * M_HEADS + hh] = jnp.broadcast_to(
                        cols_t[:, lane_i:lane_i + 1], (CHUNK, LANES))
    qs, v1s, qk, c_loc = {}, {}, {}, {}
    for d in range(2):
        q_ref, k_ref = ins[d][0], ins[d][1]
        for cq in orders[d]:
            rows = slice(cq * CHUNK, (cq + 1) * CHUNK)
            for hh in heads:
                qs[d, cq, hh] = q_ref[rows, hcs[hh]]
                qk[d, cq, hh] = _dot_nt(qs[d, cq, hh], k_ref[rows, hcs[hh]])
    for d in range(2):
        k_ref, v_ref, c_ref = ins[d][1], ins[d][2], ins[d][3]
        for cq in orders[d]:
            rows = slice(cq * CHUNK, (cq + 1) * CHUNK)
            for hh in heads:
                s = d * M_HEADS + hh
                v1s[d, cq, hh] = jnp.concatenate([v_ref[rows, hcs[hh]], ones], axis=1)
                kw_t = (k_ref[rows, hcs[hh]].astype(F32).T
                        * c_ref[cq, N_STREAMS + s:N_STREAMS + s + 1, :]).astype(BF16)
                c_loc[d, cq, hh] = _dot(kw_t, v1s[d, cq, hh])
    for step in range(BLOCK_CHUNKS):
        for d in range(2):
            _mlstm_chunk(d, orders[d][step], ins[d][3], ins[d][4], (hf_ref, hb_ref)[d], st, bc, masks[d], hcs,
                         qs, v1s, qk, c_loc)


def _mlstm_chunk(d, cq, c_ref, dec_ref, h_ref, st, bc, mask, hcs, qs, v1s, qk, c_loc):
    rows = slice(cq * CHUNK, (cq + 1) * CHUNK)
    for hh in range(M_HEADS):
        s = d * M_HEADS + hh
        key = (d, cq, hh)
        state = st[d, hh]
        m_run = bc[d, cq * M_HEADS + hh]
        floor = bc[d, (BLOCK_CHUNKS + cq) * M_HEADS + hh]
        c_ext = jnp.concatenate([c_ref[cq, s:s + 1, :], c_ref[cq, 2 * N_STREAMS + s:2 * N_STREAMS + s + 1, :]],
                                axis=1)
        d_ext = jnp.exp(c_ext - jnp.concatenate([m_run, m_run], axis=1))
        p_ext = jnp.concatenate([qk[key] * jnp.where(mask, d_ext[:, 0:LANES], 0.0),
                                 qs[key].astype(F32) * d_ext[:, LANES:2 * LANES]], axis=1).astype(BF16)
        nd = _dot(p_ext, jnp.concatenate([v1s[key], state.astype(BF16)], axis=0))
        den = nd[:, LANES:LANES + 1]
        h_ref[rows, hcs[hh]] = nd[:, 0:LANES] / jnp.maximum(jnp.abs(den), floor)
        st[d, hh] = (dec_ref[cq, s:s + 1, 0:1] * state
                     + dec_ref[cq, N_STREAMS + s:N_STREAMS + s + 1, 0:1] * c_loc[key])


def _mlstm(mq, mk, mv, c_rows, dec, cols, *, batch, seq):
    n = mq.shape[0]
    ts = BLOCK_CHUNKS * CHUNK
    nblk = seq // ts

    def specs(direction):
        def blk(b, i):
            return b * nblk + (i if direction == 0 else nblk - 1 - i)

        tok = pl.BlockSpec((ts, M_WIDTH), lambda b, i: (blk(b, i), 0))
        return tok, [tok, tok, tok,
                     pl.BlockSpec((BLOCK_CHUNKS, 3 * N_STREAMS, CHUNK), lambda b, i: (blk(b, i), 0, 0)),
                     pl.BlockSpec((BLOCK_CHUNKS, 2 * N_STREAMS, CHUNK), lambda b, i: (blk(b, i), 0, 0)),
                     pl.BlockSpec((1, CHUNK, LANES), lambda b, i: (blk(b, i), 0, 0))]

    (tok_f, in_f), (tok_b, in_b) = specs(0), specs(1)
    h_shape = jax.ShapeDtypeStruct((n, M_WIDTH), F32)
    args = (mq, mk, mv, c_rows, dec, cols)
    return pl.pallas_call(
        _mlstm_kernel,
        grid=(batch, nblk),
        in_specs=in_f + in_b,
        out_specs=(tok_f, tok_b),
        out_shape=(h_shape, h_shape),
        scratch_shapes=[pltpu.VMEM((2, M_HEADS, M_HEAD_DIM, 2 * LANES), F32),
                        pltpu.VMEM((2, 2 * BLOCK_CHUNKS * M_HEADS, CHUNK, LANES), F32)],
        compiler_params=_params("parallel", "arbitrary"),
        name="mlstm",
    )(*args, *args)


def _ffn_body(xa, nrm_ref, wup_ref, cw_ref, cb_ref, wdn_ref, fin_ref, us, act, *, tm, final):
    xn = _rms(xa, nrm_ref[...]).astype(BF16)
    for c in range(D_FF // FF_COLS):
        g0 = c * FF_COLS
        v0 = D_FF + c * FF_COLS
        gate = _conv3(_dot(xn, wup_ref[:, g0:g0 + FF_COLS]), us.at[0], tm, cw_ref, cb_ref, g0)
        val = _conv3(_dot(xn, wup_ref[:, v0:v0 + FF_COLS]), us.at[1], tm, cw_ref, cb_ref, v0)
        act[:, g0:g0 + FF_COLS] = (gate * jax.nn.sigmoid(gate) * val).astype(BF16)
    y = xa[HALO:HALO + tm] + _dot(act[...], wdn_ref[...])
    if final:
        y = _rms(y, fin_ref[...])
    return y


def _ffn_kernel(x_ref, xp_ref, xn_ref, nrm_ref, wup_ref, cw_ref, cb_ref, wdn_ref, fin_ref, o_ref, us, act,
                *, tm, tps, final):
    jj = pl.program_id(0) % tps
    xa = jnp.concatenate([
        jnp.where(jj == 0, 0.0, xp_ref[...]),
        x_ref[...],
        jnp.where(jj == tps - 1, 0.0, xn_ref[...]),
    ], axis=0)
    o_ref[...] = _ffn_body(xa, nrm_ref, wup_ref, cw_ref, cb_ref, wdn_ref, fin_ref, us, act, tm=tm, final=final)


def _ffn_scratch(tm):
    return [pltpu.VMEM((2, FF_COLS // LANES, ROW_PITCH * (tm + 2 * HALO), LANES), F32),
            pltpu.VMEM((tm, D_FF), BF16)]


def _ab_out_ffn_kernel(x_ref, xp_ref, xn_ref, ya_ref, yap_ref, yan_ref, hf_ref, hfp_ref, hfn_ref, hb_ref, hbp_ref,
                       hbn_ref, mo_ref, mop_ref, mon_ref, on_ref, wa_ref, wm_ref, nrm_ref, wup_ref, cw_ref, cb_ref,
                       wdn_ref, fin_ref, o_ref, us, act, *, tm, tps):
    jj = pl.program_id(0) % tps

    def rows(main, prev, nxt):
        return jnp.concatenate([main[...], prev[...], nxt[...]], axis=0)

    hf, hb, mo = rows(hf_ref, hfp_ref, hfn_ref), rows(hb_ref, hbp_ref, hbn_ref), rows(mo_ref, mop_ref, mon_ref)
    parts = []
    for hh in range(M_HEADS):
        hc = slice(hh * LANES, (hh + 1) * LANES)
        t = hf[:, hc] + hb[:, hc]
        t = t * lax.rsqrt(jnp.mean(t * t, axis=-1, keepdims=True) + EPS) * on_ref[:, hc]
        parts.append((jax.nn.sigmoid(mo[:, hc]) * t).astype(BF16))
    ym = jnp.concatenate(parts, axis=1)
    x1 = (rows(x_ref, xp_ref, xn_ref) + _dot(rows(ya_ref, yap_ref, yan_ref), wa_ref[...]) + _dot(ym, wm_ref[...]))
    xa = jnp.concatenate([
        jnp.where(jj == 0, 0.0, x1[tm + PAIR - HALO:tm + PAIR]),
        x1[0:tm],
        jnp.where(jj == tps - 1, 0.0, x1[tm + PAIR:tm + PAIR + HALO]),
    ], axis=0)
    o_ref[...] = _ffn_body(xa, nrm_ref, wup_ref, cw_ref, cb_ref, wdn_ref, fin_ref, us, act, tm=tm, final=False)


def _ab_out_ffn(x, ya, hf, hb, mo, on, wa, wm, nrm, wup, cw, cb, wdn, fin, *, seq, tm):
    n = x.shape[0]
    row = lambda i: (i, 0)

    def with_halo(width):
        return [pl.BlockSpec((tm, width), row), *_halo_specs(tm, n, width, PAIR)]

    consts = [on, wa, wm, nrm, wup, cw, cb, wdn, fin]
    return pl.pallas_call(
        functools.partial(_ab_out_ffn_kernel, tm=tm, tps=seq // tm),
        grid=(n // tm,),
        in_specs=with_halo(D_MODEL) + 4 * with_halo(M_WIDTH) + [_const_spec(c.shape) for c in consts],
        out_specs=pl.BlockSpec((tm, D_MODEL), row),
        out_shape=jax.ShapeDtypeStruct((n, D_MODEL), F32),
        scratch_shapes=_ffn_scratch(tm),
        compiler_params=_params("parallel"),
        name="ab_out_ffn",
    )(x, x, x, ya, ya, ya, hf, hf, hf, hb, hb, hb, mo, mo, mo, *consts)


def _ffn(x, nrm, wup, cw, cb, wdn, fin, *, seq, tm, final):
    n = x.shape[0]
    row = lambda i: (i, 0)
    return pl.pallas_call(
        functools.partial(_ffn_kernel, tm=tm, tps=seq // tm, final=final),
        grid=(n // tm,),
        in_specs=[
            pl.BlockSpec((tm, D_MODEL), row), *_halo_specs(tm, n, D_MODEL),
            _const_spec(nrm.shape), _const_spec(wup.shape), _const_spec(cw.shape), _const_spec(cb.shape),
            _const_spec(wdn.shape), _const_spec(fin.shape),
        ],
        out_specs=pl.BlockSpec((tm, D_MODEL), row),
        out_shape=jax.ShapeDtypeStruct((n, D_MODEL), F32),
        scratch_shapes=_ffn_scratch(tm),
        compiler_params=_params("parallel"),
        name="ffn_final" if final else "ffn",
    )(x, x, x, nrm, wup, cw, cb, wdn, fin)


def _gelu(x):
    return 0.5 * x * (1.0 + lax.erf(x * (2.0 ** -0.5)))


def _mix_c_kernel(x_ref, nrm_ref, win_ref, vn_ref, ws_ref, bs_ref, wout_ref, o_ref, zs, *, tm):
    x = x_ref[...]
    xn = _rms(x, nrm_ref[...]).astype(BF16)
    v = _gelu(_dot(xn, win_ref[:, D_MODEL:2 * D_MODEL]))
    u = _gelu(_dot(xn, win_ref[:, 0:D_MODEL]))
    vn = _rms(v, vn_ref[...]).astype(BF16)
    nch = tm // CHUNK
    for g in range(G_GROUPS):
        cols = slice(g * LANES, (g + 1) * LANES)
        rhs = jnp.concatenate([vn[c * CHUNK:(c + 1) * CHUNK, cols] for c in range(nch)], axis=1)
        sv = _dot(ws_ref[g], rhs) + bs_ref[:, g:g + 1]
        for c in range(nch):
            zs[c * CHUNK:(c + 1) * CHUNK, cols] = (
                u[c * CHUNK:(c + 1) * CHUNK, cols] * sv[:, c * LANES:(c + 1) * LANES]).astype(BF16)
    o_ref[...] = x + _dot(zs[...], wout_ref[...])


def _mix_c(x, nrm, win, vn, ws, bs, wout, *, tm):
    n = x.shape[0]
    row = lambda i: (i, 0)
    return pl.pallas_call(
        functools.partial(_mix_c_kernel, tm=tm),
        grid=(n // tm,),
        in_specs=[
            pl.BlockSpec((tm, D_MODEL), row),
            _const_spec(nrm.shape), _const_spec(win.shape), _const_spec(vn.shape), _const_spec(ws.shape),
            _const_spec(bs.shape), _const_spec(wout.shape),
        ],
        out_specs=pl.BlockSpec((tm, D_MODEL), row),
        out_shape=jax.ShapeDtypeStruct((n, D_MODEL), F32),
        scratch_shapes=[pltpu.VMEM((tm, D_MODEL), BF16)],
        compiler_params=_params("parallel"),
        name="mix_c",
    )(x, nrm, win, vn, ws, bs, wout)


def _rope_tables(seq):
    pos = jnp.arange(seq, dtype=F32)
    inv = 1.0 / (ROPE_THETA ** (jnp.arange(0, QK_ROPE, 2, dtype=F32) / QK_ROPE))
    ang = pos[:, None] * inv[None, :]
    cos, sin = jnp.cos(ang), jnp.sin(ang)
    one = jnp.ones((seq, QK_NOPE), F32)
    zero_n = jnp.zeros((seq, QK_NOPE), F32)
    pad = jnp.zeros((seq, LANES - QK_NOPE - QK_ROPE), F32)
    return (jnp.concatenate([one, cos, cos, pad], axis=1), jnp.concatenate([zero_n, sin, sin, pad], axis=1))


def _head_block(nope, x1, x2):
    pad = jnp.zeros((nope.shape[0], LANES - QK_NOPE - QK_ROPE), nope.dtype)
    return jnp.concatenate([nope, x1, x2, pad], axis=1)


def _prep_even(i, ab_norm, ab_w_in, mla_q_norm, mla_w_uq, mla_kv_norm, mla_w_ukv, mlstm_conv_w, mlstm_conv_b,
               mlstm_gate_bias, mlstm_out_norm, ab_w_out):
    w_in = ab_w_in[i]
    half = QK_ROPE // 2
    zq = jnp.zeros((D_MODEL, QK_NOPE), F32)
    kr1 = w_in[:, 384:384 + half]
    kr2 = w_in[:, 384 + half:416]
    gate_order = jnp.array([0, 1, 2, 3, 8, 9, 10, 11, 4, 5, 6, 7, 12, 13, 14, 15])
    gates = jnp.pad(w_in[:, 2464:2480][:, gate_order], ((0, 0), (0, LANES - 2 * N_STREAMS)))
    w_big = jnp.concatenate([
        w_in[:, 0:384],
        _head_block(zq, kr1, kr2),
        _head_block(zq, -kr2, kr1),
        gates,
        w_in[:, 416:2464],
    ], axis=1).astype(BF16)
    w_uq = mla_w_uq[i].reshape(Q_LORA, MLA_HEADS, QK_NOPE + QK_ROPE)
    zn = jnp.zeros((Q_LORA, QK_NOPE), F32)
    wqa = jnp.concatenate([_head_block(w_uq[:, h, :QK_NOPE], w_uq[:, h, QK_NOPE:QK_NOPE + half],
                                       w_uq[:, h, QK_NOPE + half:]) for h in range(MLA_HEADS)], axis=1)
    wqb = jnp.concatenate([_head_block(zn, -w_uq[:, h, QK_NOPE + half:], w_uq[:, h, QK_NOPE:QK_NOPE + half])
                           for h in range(MLA_HEADS)], axis=1)
    w_ukv = mla_w_ukv[i].reshape(KV_LORA, MLA_HEADS, QK_NOPE + V_HEAD)
    wka = jnp.pad(w_ukv[:, :, :QK_NOPE], ((0, 0), (0, 0), (0, LANES - QK_NOPE))).reshape(KV_LORA, MLA_HEADS * LANES)
    zv = jnp.zeros((KV_LORA, V_HEAD), F32)
    wv = jnp.concatenate([jnp.concatenate([w_ukv[:, h, QK_NOPE:], zv] if h % 2 == 0 else [zv, w_ukv[:, h, QK_NOPE:]],
                                          axis=1) for h in range(MLA_HEADS)], axis=1)
    one_even = jnp.zeros((LANES,), F32).at[V_HEAD].set(1.0)
    one_odd = jnp.zeros((LANES,), F32).at[0].set(1.0)
    vone = jnp.concatenate([one_even if h % 2 == 0 else one_odd for h in range(MLA_HEADS)])[None, :]
    w_out = ab_w_out[i].astype(BF16)
    return dict(
        nrm=ab_norm[i][None, :], w_big=w_big, qn=mla_q_norm[i][None, :], wqa=wqa.astype(BF16),
        wqb=wqb.astype(BF16), kvn=mla_kv_norm[i][None, :], wka=wka.astype(BF16), wv=wv.astype(BF16),
        gb=jnp.broadcast_to(mlstm_gate_bias[i][gate_order][:, None], (2 * N_STREAMS, LANES)), vone=vone,
        conv_w=mlstm_conv_w[i], conv_b=mlstm_conv_b[i][None, :], on=mlstm_out_norm[i][None, :],
        wa=w_out[:MLA_HEADS * V_HEAD], wm=w_out[MLA_HEADS * V_HEAD:])


def _pick(pref, seq):
    return min(pref, seq)


def _forward(x3, rope, even, odd, ffn, final_norm):
    batch, seq, _ = x3.shape
    x = x3.reshape(batch * seq, D_MODEL)
    tm = _pick(512, seq)
    cos_t, sin_t = rope
    e = even
    q, k, v, mq, mk, mv, mo, g = _ab_in(x, cos_t, sin_t, e, seq=seq, tm=_pick(1024, seq))
    tq = _pick(ATTN_TQ, seq)
    score_bytes = 2 * tq * seq * 4
    pair_bytes = 2 * 2 * seq * 2 * LANES * 2
    pairs = min(MLA_HEADS // 2, max(1, (ATTN_VMEM - score_bytes) // pair_bytes))
    while (MLA_HEADS // 2) % pairs:
        pairs -= 1
    ya = _attn(q, k, v, batch=batch, seq=seq, tq=tq, pairs=pairs)
    c_rows, dec, cols = _gates(g, batch=batch, seq=seq)
    hf, hb = _mlstm(mq, mk, mv, c_rows, dec, cols, batch=batch, seq=seq)
    f = ffn[0]
    x = _ab_out_ffn(x, ya, hf, hb, mo, e["on"], e["wa"], e["wm"], f["nrm"], f["wup"], f["cw"], f["cb"], f["wdn"],
                    final_norm, seq=seq, tm=tm)
    o = odd
    x = _mix_c(x, o["nrm"], o["win"], o["vn"], o["ws"], o["bs"], o["wout"], tm=_pick(1024, seq))
    f = ffn[1]
    x = _ffn(x, f["nrm"], f["wup"], f["cw"], f["cb"], f["wdn"], final_norm, seq=seq, tm=tm, final=True)
    return x.reshape(batch, seq, D_MODEL)


def _prep_rest(c_norm, c_w_in, c_v_norm, c_w_spatial, c_b_spatial, c_w_out, ffn_norm, ffn_w_up, ffn_conv_w,
               ffn_conv_b, ffn_w_down):
    odd = dict(nrm=c_norm[0][None, :], win=c_w_in[0].astype(BF16), vn=c_v_norm[0][None, :],
               ws=c_w_spatial[0].astype(BF16), bs=c_b_spatial[0].T, wout=c_w_out[0].astype(BF16))
    ffn = [dict(nrm=ffn_norm[l][None, :], wup=ffn_w_up[l].astype(BF16), cw=ffn_conv_w[l], cb=ffn_conv_b[l][None, :],
                wdn=ffn_w_down[l].astype(BF16)) for l in range(2)]
    return odd, ffn


def kernel(x_prompt, x_sample, ab_norm, ab_w_in, mla_q_norm, mla_w_uq, mla_kv_norm, mla_w_ukv, mlstm_conv_w,
           mlstm_conv_b, mlstm_gate_bias, mlstm_out_norm, ab_w_out, c_norm, c_w_in, c_v_norm, c_w_spatial,
           c_b_spatial, c_w_out, ffn_norm, ffn_w_up, ffn_conv_w, ffn_conv_b, ffn_w_down, final_norm):
    even = _prep_even(0, ab_norm, ab_w_in, mla_q_norm, mla_w_uq, mla_kv_norm, mla_w_ukv, mlstm_conv_w, mlstm_conv_b,
                      mlstm_gate_bias, mlstm_out_norm, ab_w_out)
    odd, ffn = _prep_rest(c_norm, c_w_in, c_v_norm, c_w_spatial, c_b_spatial, c_w_out, ffn_norm, ffn_w_up,
                          ffn_conv_w, ffn_conv_b, ffn_w_down)
    fin = final_norm[None, :]
    rope = _rope_tables(max(x_prompt.shape[1], x_sample.shape[1]))
    return (_forward(x_prompt, rope, even, odd, ffn, fin), _forward(x_sample, rope, even, odd, ffn, fin))
```

```python
import functools

import jax
import jax.numpy as jnp
from jax import lax
from jax.experimental import pallas as pl
from jax.experimental.pallas import tpu as pltpu

F32 = jnp.float32
BF16 = jnp.bfloat16

EPS = 1e-6
D_MODEL = 1024
MLA_HEADS = 8
Q_LORA = 256
KV_LORA = 128
QK_NOPE = 64
QK_ROPE = 32
V_HEAD = 64
ROPE_THETA = 10000.0
M_HEADS = 4
M_HEAD_DIM = 128
M_WIDTH = M_HEADS * M_HEAD_DIM
N_STREAMS = 2 * M_HEADS
CHUNK = 128
G_GROUPS = 8
D_FF = 2816
LANES = 128
SUBLANES = 8
HALO = SUBLANES
BLOCK_CHUNKS = 8
PAIR = 2 * SUBLANES
MXU_COLS = 256
ROW_PITCH = 2
FF_COLS = MXU_COLS
KEY_TILE = MXU_COLS
ATTN_TQ = 512
ATTN_VMEM = 48 * 1024 * 1024
LOG2_E = 1.4426950408889634
VMEM_LIMIT = 56 * 1024 * 1024

_C_Q, _C_KV, _C_KRA, _C_KRB, _C_G, _C_MQK, _C_MV, _C_MO, _C_END = (
    0, 256, 384, 512, 640, 768, 1792, 2304, 2816)


def _rms(x, g):
    return x * lax.rsqrt(jnp.mean(x * x, axis=-1, keepdims=True) + EPS) * g


def _const_spec(shape):
    nd = len(shape)
    return pl.BlockSpec(shape, lambda *_: (0,) * nd, pipeline_mode=pl.Buffered(1))


def _params(*sem):
    return pltpu.CompilerParams(dimension_semantics=sem, vmem_limit_bytes=VMEM_LIMIT)


def _dot(a, b):
    return jnp.dot(a, b, preferred_element_type=F32)


def _dot_nt(a, b):
    return lax.dot_general(a, b, (((1,), (1,)), ((), ())), preferred_element_type=F32)


def _halo_specs(tile_rows, n_rows, width, halo=HALO):
    tb = tile_rows // halo
    last = n_rows // halo - 1
    return (pl.BlockSpec((halo, width), lambda i: (jnp.maximum(i * tb - 1, 0), 0)),
            pl.BlockSpec((halo, width), lambda i: (jnp.minimum((i + 1) * tb, last), 0)))


def _conv3(y, u, rows, w_ref, b_ref, c0):
    tall = rows + 2 * HALO
    outs = []
    for s in range(y.shape[1] // LANES):
        lanes = slice(s * LANES, (s + 1) * LANES)
        cols = slice(c0 + s * LANES, c0 + (s + 1) * LANES)
        u[s, pl.ds(0, tall, stride=ROW_PITCH), :] = y[:, lanes]
        prv = u[s, pl.ds(ROW_PITCH * (HALO - 1), rows, stride=ROW_PITCH), :]
        nxt = u[s, pl.ds(ROW_PITCH * (HALO + 1), rows, stride=ROW_PITCH), :]
        outs.append(prv * w_ref[0:1, cols] + y[HALO:HALO + rows, lanes] * w_ref[1:2, cols]
                    + nxt * w_ref[2:3, cols] + b_ref[:, cols])
    return jnp.concatenate(outs, axis=1)


def _ab_in_kernel(x_ref, xp_ref, xn_ref, cos_ref, sin_ref, nrm_ref, w_ref, qn_ref, wqa_ref, wqb_ref, kvn_ref,
                  wka_ref, wv_ref, gb_ref, vone_ref, cw_ref, cb_ref,
                  q_ref, k_ref, v_ref, mq_ref, mk_ref, mv_ref, mo_ref, g_ref, us, *, tm, tps):
    jj = pl.program_id(0) % tps
    xa = jnp.concatenate([
        jnp.where(jj == 0, 0.0, xp_ref[...]),
        x_ref[...],
        jnp.where(jj == tps - 1, 0.0, xn_ref[...]),
    ], axis=0)
    xe = _rms(xa, nrm_ref[...]).astype(BF16)
    xn = xe[HALO:HALO + tm]

    def proj(a, b):
        return _dot(xn, w_ref[:, a:b])

    cq = proj(_C_Q, _C_KV)
    ckv_kra = proj(_C_KV, _C_KRB)
    krb_g = proj(_C_KRB, _C_MQK)

    nq_blocks = M_WIDTH // MXU_COLS
    for cb in range(2 * nq_blocks):
        cols = slice(cb * MXU_COLS, (cb + 1) * MXU_COLS)
        y = _conv3(_dot(xe, w_ref[:, _C_MQK + cb * MXU_COLS:_C_MQK + (cb + 1) * MXU_COLS]), us.at[cb % 2], tm,
                   cw_ref, cb_ref, cb * MXU_COLS)
        y = y * jax.nn.sigmoid(y)
        if cb < nq_blocks:
            mq_ref[:, cols] = y.astype(BF16)
        else:
            mk_ref[:, (cb - nq_blocks) * MXU_COLS:(cb - nq_blocks + 1) * MXU_COLS] = (
                y * (M_HEAD_DIM ** -0.5)).astype(BF16)
    mv_ref[...] = proj(_C_MV, _C_MO).astype(BF16)
    mo_ref[...] = proj(_C_MO, _C_END)
    gt = krb_g[:, LANES:2 * LANES].T
    for c in range(tm // CHUNK):
        g_ref[c] = gt[0:2 * N_STREAMS, c * CHUNK:(c + 1) * CHUNK] + gb_ref[:, 0:1]

    cos = cos_ref[...]
    sin = sin_ref[...]
    scale = (QK_NOPE + QK_ROPE) ** -0.5 * LOG2_E
    cqn = _rms(cq, qn_ref[...]).astype(BF16)
    qa = _dot(cqn, wqa_ref[...])
    qb = _dot(cqn, wqb_ref[...])
    cos_q = cos * scale
    sin_q = sin * scale
    for h in range(MLA_HEADS):
        blk = slice(h * LANES, (h + 1) * LANES)
        q_ref[:, blk] = (qa[:, blk] * cos_q + qb[:, blk] * sin_q).astype(BF16)

    ckvn = _rms(ckv_kra[:, 0:LANES], kvn_ref[...]).astype(BF16)
    ka = _dot(ckvn, wka_ref[...])
    kr = ckv_kra[:, LANES:2 * LANES] * cos + krb_g[:, 0:LANES] * sin
    for h in range(MLA_HEADS):
        blk = slice(h * LANES, (h + 1) * LANES)
        k_ref[:, blk] = (ka[:, blk] + kr).astype(BF16)
    v_ref[...] = (_dot(ckvn, wv_ref[...]) + vone_ref[...]).astype(BF16)


def _ab_in(x, cos_t, sin_t, e, *, seq, tm):
    n = x.shape[0]
    nt = n // tm
    tps = seq // tm
    row = lambda i: (i, 0)
    tab = lambda i: (i % tps, 0)
    consts = [e[k] for k in ("nrm", "w_big", "qn", "wqa", "wqb", "kvn", "wka", "wv", "gb", "vone", "conv_w",
                             "conv_b")]
    out_shape = (
        jax.ShapeDtypeStruct((n, MLA_HEADS * LANES), BF16),
        jax.ShapeDtypeStruct((n, MLA_HEADS * LANES), BF16),
        jax.ShapeDtypeStruct((n, MLA_HEADS * LANES), BF16),
        jax.ShapeDtypeStruct((n, M_WIDTH), BF16),
        jax.ShapeDtypeStruct((n, M_WIDTH), BF16),
        jax.ShapeDtypeStruct((n, M_WIDTH), BF16),
        jax.ShapeDtypeStruct((n, M_WIDTH), F32),
        jax.ShapeDtypeStruct((n // CHUNK, 2 * N_STREAMS, CHUNK), F32),
    )
    wide = pl.BlockSpec((tm, MLA_HEADS * LANES), row)
    narrow = pl.BlockSpec((tm, M_WIDTH), row)
    return pl.pallas_call(
        functools.partial(_ab_in_kernel, tm=tm, tps=tps),
        grid=(nt,),
        in_specs=[pl.BlockSpec((tm, D_MODEL), row), *_halo_specs(tm, n, D_MODEL),
                  pl.BlockSpec((tm, LANES), tab), pl.BlockSpec((tm, LANES), tab)]
                 + [_const_spec(c.shape) for c in consts],
        out_specs=(wide, wide, wide, narrow, narrow, narrow, narrow,
                   pl.BlockSpec((tm // CHUNK, 2 * N_STREAMS, CHUNK), lambda i: (i, 0, 0))),
        out_shape=out_shape,
        scratch_shapes=[pltpu.VMEM((2, MXU_COLS // LANES, ROW_PITCH * (tm + 2 * HALO), LANES), F32)],
        compiler_params=_params("parallel"),
        name="ab_in",
    )(x, x, x, cos_t, sin_t, *consts)


def _attn_kernel(q_ref, k_ref, v_ref, o_ref, s_scr, *, tq, pairs):
    nt = k_ref.shape[0] // KEY_TILE
    lane = lax.broadcasted_iota(jnp.int32, (tq, LANES), 1)
    for pr in range(pairs):
        mbs = []
        for h in range(2):
            hc = slice((2 * pr + h) * LANES, (2 * pr + h + 1) * LANES)
            q = q_ref[:, hc]
            mx = jnp.full((tq, LANES), -jnp.inf, F32)
            for t in range(nt):
                keys = slice(t * KEY_TILE, (t + 1) * KEY_TILE)
                s = _dot_nt(q, k_ref[keys, hc])
                s_scr[h, :, keys] = s
                for j in range(KEY_TILE // LANES):
                    mx = jnp.maximum(mx, s[:, j * LANES:(j + 1) * LANES])
            mbs.append(jnp.broadcast_to(jnp.max(mx, axis=-1, keepdims=True), (tq, LANES)))
        outs = []
        for h in range(2):
            hc = slice((2 * pr + h) * LANES, (2 * pr + h + 1) * LANES)
            mb = jnp.concatenate([mbs[h]] * (KEY_TILE // LANES), axis=1)
            acc = jnp.zeros((tq, LANES), F32)
            for t in range(nt):
                keys = slice(t * KEY_TILE, (t + 1) * KEY_TILE)
                p = jnp.exp2(s_scr[h, :, keys] - mb).astype(BF16)
                acc = acc + _dot(p, v_ref[keys, hc])
            ones_lane = V_HEAD if h == 0 else 0
            outs.append(acc / acc[:, ones_lane:ones_lane + 1])
        o_ref[:, pr * LANES:(pr + 1) * LANES] = jnp.where(lane < V_HEAD, outs[0], outs[1]).astype(BF16)


def _attn(q, k, v, *, batch, seq, tq, pairs):
    n = q.shape[0]
    nq = seq // tq
    steps = MLA_HEADS // 2 // pairs
    return pl.pallas_call(
        functools.partial(_attn_kernel, tq=tq, pairs=pairs),
        grid=(batch, steps, nq),
        in_specs=[
            pl.BlockSpec((tq, 2 * pairs * LANES), lambda b, p, i: (b * nq + i, p)),
            pl.BlockSpec((seq, 2 * pairs * LANES), lambda b, p, i: (b, p)),
            pl.BlockSpec((seq, 2 * pairs * LANES), lambda b, p, i: (b, p)),
        ],
        out_specs=pl.BlockSpec((tq, pairs * LANES), lambda b, p, i: (b * nq + i, p)),
        out_shape=jax.ShapeDtypeStruct((n, MLA_HEADS * V_HEAD), BF16),
        scratch_shapes=[pltpu.VMEM((2, tq, seq), F32)],
        compiler_params=_params("parallel", "parallel", "arbitrary"),
        name="attn",
    )(q, k, v)


def _log_sigmoid(x):
    return jnp.minimum(x, 0.0) - jnp.log1p(jnp.exp(-jnp.abs(x)))


def _gates_kernel(g_ref, c_ref, dec_ref, cols_ref, tot_s, mloc_s, mpf_s, mpb_s, *, nc):
    rows = nc * N_STREAMS
    shape = (rows, CHUNK)
    three = (nc, N_STREAMS, CHUNK)
    li = g_ref[:, 0:N_STREAMS, :].reshape(shape)
    lf = _log_sigmoid(g_ref[:, N_STREAMS:2 * N_STREAMS, :].reshape(shape))
    fwd = (lax.broadcasted_iota(jnp.int32, shape, 0) & M_HEADS) == 0
    lane = lax.broadcasted_iota(jnp.int32, shape, 1)
    shifts = (1, 2, 4, 8, 16, 32, 64)

    pre = lf
    for sh in shifts:
        pre = pre + jnp.where(lane >= sh, pltpu.roll(pre, sh, axis=1), 0.0)
    tot = jnp.broadcast_to(pre[:, CHUNK - 1:CHUNK], shape)
    b = jnp.where(fwd, pre, tot - pre + lf)
    c = li - b
    run_f = c
    run_b = c
    for sh in shifts:
        run_f = jnp.maximum(run_f, jnp.where(lane >= sh, pltpu.roll(run_f, sh, axis=1), -jnp.inf))
        run_b = jnp.maximum(run_b, jnp.where(lane < CHUNK - sh, pltpu.roll(run_b, CHUNK - sh, axis=1), -jnp.inf))
    run = jnp.where(fwd, run_f, run_b)
    m_loc = tot + jnp.broadcast_to(jnp.max(c, axis=1, keepdims=True), shape)

    tot_s[...] = tot.reshape(three)
    mloc_s[...] = m_loc.reshape(three)
    fwd8 = lax.broadcasted_iota(jnp.int32, (N_STREAMS, CHUNK), 0) < M_HEADS

    def step(i, m):
        jb = nc - 1 - i
        mpf_s[i] = m
        mpb_s[jb] = m
        return jnp.maximum(jnp.where(fwd8, tot_s[i], tot_s[jb]) + m, jnp.where(fwd8, mloc_s[i], mloc_s[jb]))

    lax.fori_loop(0, nc, step, jnp.zeros((N_STREAMS, CHUNK), F32))
    m_prev = jnp.where(fwd, mpf_s[...].reshape(shape), mpb_s[...].reshape(shape))

    m_run = jnp.maximum(m_prev, run)
    floor = jnp.exp(-(b + m_run))
    w = jnp.exp(tot + c - m_loc)
    m_new = jnp.maximum(tot + m_prev, m_loc)
    c_ref[:, 0:N_STREAMS, :] = c.reshape(three)
    c_ref[:, N_STREAMS:2 * N_STREAMS, :] = w.reshape(three)
    c_ref[:, 2 * N_STREAMS:3 * N_STREAMS, :] = m_prev.reshape(three)
    dec_ref[:, 0:N_STREAMS, :] = jnp.exp(tot + m_prev - m_new).reshape(three)
    dec_ref[:, N_STREAMS:2 * N_STREAMS, :] = jnp.exp(m_loc - m_new).reshape(three)
    quantities = [a.reshape(three) for a in (m_run, floor)]
    pad_rows = CHUNK - BLOCK_CHUNKS * 2 * N_STREAMS
    pad = [jnp.zeros((pad_rows, CHUNK), F32)] if pad_rows else []
    for blk in range(nc // BLOCK_CHUNKS):
        tile = jnp.concatenate(
            [a[blk * BLOCK_CHUNKS + cq] for cq in range(BLOCK_CHUNKS) for a in quantities] + pad, axis=0)
        cols_ref[blk] = tile.T


def _gates(g, *, batch, seq):
    nc = seq // CHUNK
    nchunks = g.shape[0]
    three = (nc, N_STREAMS, CHUNK)
    return pl.pallas_call(
        functools.partial(_gates_kernel, nc=nc),
        grid=(batch,),
        in_specs=[pl.BlockSpec((nc, 2 * N_STREAMS, CHUNK), lambda b: (b, 0, 0))],
        out_specs=(
            pl.BlockSpec((nc, 3 * N_STREAMS, CHUNK), lambda b: (b, 0, 0)),
            pl.BlockSpec((nc, 2 * N_STREAMS, CHUNK), lambda b: (b, 0, 0)),
            pl.BlockSpec((nc // BLOCK_CHUNKS, CHUNK, LANES), lambda b: (b, 0, 0)),
        ),
        out_shape=(
            jax.ShapeDtypeStruct((nchunks, 3 * N_STREAMS, CHUNK), F32),
            jax.ShapeDtypeStruct((nchunks, 2 * N_STREAMS, CHUNK), F32),
            jax.ShapeDtypeStruct((nchunks // BLOCK_CHUNKS, CHUNK, LANES), F32),
        ),
        scratch_shapes=[pltpu.VMEM(three, F32)] * 4,
        compiler_params=_params("parallel"),
        name="gates",
    )(g)


def _mlstm_kernel(*refs):
    ins, (hf_ref, hb_ref, st, bc) = (refs[0:6], refs[6:12]), refs[12:]

    @pl.when(pl.program_id(1) == 0)
    def _():
        st[...] = jnp.zeros_like(st)

    row = lax.broadcasted_iota(jnp.int32, (CHUNK, CHUNK), 0)
    col = lax.broadcasted_iota(jnp.int32, (CHUNK, CHUNK), 1)
    masks = (col <= row, col >= row)
    ones = (col == 0).astype(BF16)
    orders = (range(BLOCK_CHUNKS), range(BLOCK_CHUNKS - 1, -1, -1))
    heads = range(M_HEADS)
    hcs = [slice(hh * LANES, (hh + 1) * LANES) for hh in heads]
    for d in range(2):
        cols_t = ins[d][5][0]
        for cq in range(BLOCK_CHUNKS):
            for hh in heads:
                base = cq * 2 * N_STREAMS + d * M_HEADS + hh
                for qi in range(2):
                    lane_i = base + qi * N_STREAMS
                    bc[d, (qi * BLOCK_CHUNKS + cq) * M_HEADS + hh] = jnp.broadcast_to(
                        cols_t[:, lane_i:lane_i + 1], (CHUNK, LANES))
    qs, v1s, qk, c_loc = {}, {}, {}, {}
    for d in range(2):
        q_ref, k_ref = ins[d][0], ins[d][1]
        for cq in orders[d]:
            rows = slice(cq * CHUNK, (cq + 1) * CHUNK)
            for hh in heads:
                qs[d, cq, hh] = q_ref[rows, hcs[hh]]
                qk[d, cq, hh] = _dot_nt(qs[d, cq, hh], k_ref[rows, hcs[hh]])
    for d in range(2):
        k_ref, v_ref, c_ref = ins[d][1], ins[d][2], ins[d][3]
        for cq in orders[d]:
            rows = slice(cq * CHUNK, (cq + 1) * CHUNK)
            for hh in heads:
                s = d * M_HEADS + hh
                v1s[d, cq, hh] = jnp.concatenate([v_ref[rows, hcs[hh]], ones], axis=1)
                kw_t = (k_ref[rows, hcs[hh]].astype(F32).T
                        * c_ref[cq, N_STREAMS + s:N_STREAMS + s + 1, :]).astype(BF16)
                c_loc[d, cq, hh] = _dot(kw_t, v1s[d, cq, hh])
    for step in range(BLOCK_CHUNKS):
        for d in range(2):
            _mlstm_chunk(d, orders[d][step], ins[d][3], ins[d][4], (hf_ref, hb_ref)[d], st, bc, masks[d], hcs,
                         qs, v1s, qk, c_loc)


def _mlstm_chunk(d, cq, c_ref, dec_ref, h_ref, st, bc, mask, hcs, qs, v1s, qk, c_loc):
    rows = slice(cq * CHUNK, (cq + 1) * CHUNK)
    for hh in range(M_HEADS):
        s = d * M_HEADS + hh
        key = (d, cq, hh)
        state = st[d, hh]
        m_run = bc[d, cq * M_HEADS + hh]
        floor = bc[d, (BLOCK_CHUNKS + cq) * M_HEADS + hh]
        c_ext = jnp.concatenate([c_ref[cq, s:s + 1, :], c_ref[cq, 2 * N_STREAMS + s:2 * N_STREAMS + s + 1, :]],
                                axis=1)
        d_ext = jnp.exp(c_ext - jnp.concatenate([m_run, m_run], axis=1))
        p_ext = jnp.concatenate([qk[key] * jnp.where(mask, d_ext[:, 0:LANES], 0.0),
                                 qs[key].astype(F32) * d_ext[:, LANES:2 * LANES]], axis=1).astype(BF16)
        nd = _dot(p_ext, jnp.concatenate([v1s[key], state.astype(BF16)], axis=0))
        den = nd[:, LANES:LANES + 1]
        h_ref[rows, hcs[hh]] = nd[:, 0:LANES] / jnp.maximum(jnp.abs(den), floor)
        st[d, hh] = (dec_ref[cq, s:s + 1, 0:1] * state
                     + dec_ref[cq, N_STREAMS + s:N_STREAMS + s + 1, 0:1] * c_loc[key])


def _mlstm(mq, mk, mv, c_rows, dec, cols, *, batch, seq):
    n = mq.shape[0]
    ts = BLOCK_CHUNKS * CHUNK
    nblk = seq // ts

    def specs(direction):
        def blk(b, i):
            return b * nblk + (i if direction == 0 else nblk - 1 - i)

        tok = pl.BlockSpec((ts, M_WIDTH), lambda b, i: (blk(b, i), 0))
        return tok, [tok, tok, tok,
                     pl.BlockSpec((BLOCK_CHUNKS, 3 * N_STREAMS, CHUNK), lambda b, i: (blk(b, i), 0, 0)),
                     pl.BlockSpec((BLOCK_CHUNKS, 2 * N_STREAMS, CHUNK), lambda b, i: (blk(b, i), 0, 0)),
                     pl.BlockSpec((1, CHUNK, LANES), lambda b, i: (blk(b, i), 0, 0))]

    (tok_f, in_f), (tok_b, in_b) = specs(0), specs(1)
    h_shape = jax.ShapeDtypeStruct((n, M_WIDTH), F32)
    args = (mq, mk, mv, c_rows, dec, cols)
    return pl.pallas_call(
        _mlstm_kernel,
        grid=(batch, nblk),
        in_specs=in_f + in_b,
        out_specs=(tok_f, tok_b),
        out_shape=(h_shape, h_shape),
        scratch_shapes=[pltpu.VMEM((2, M_HEADS, M_HEAD_DIM, 2 * LANES), F32),
                        pltpu.VMEM((2, 2 * BLOCK_CHUNKS * M_HEADS, CHUNK, LANES), F32)],
        compiler_params=_params("parallel", "arbitrary"),
        name="mlstm",
    )(*args, *args)


def _ffn_body(xa, nrm_ref, wup_ref, cw_ref, cb_ref, wdn_ref, fin_ref, us, act, *, tm, final):
    xn = _rms(xa, nrm_ref[...]).astype(BF16)
    for c in range(D_FF // FF_COLS):
        g0 = c * FF_COLS
        v0 = D_FF + c * FF_COLS
        gate = _conv3(_dot(xn, wup_ref[:, g0:g0 + FF_COLS]), us.at[0], tm, cw_ref, cb_ref, g0)
        val = _conv3(_dot(xn, wup_ref[:, v0:v0 + FF_COLS]), us.at[1], tm, cw_ref, cb_ref, v0)
        act[:, g0:g0 + FF_COLS] = (gate * jax.nn.sigmoid(gate) * val).astype(BF16)
    y = xa[HALO:HALO + tm] + _dot(act[...], wdn_ref[...])
    if final:
        y = _rms(y, fin_ref[...])
    return y


def _ffn_kernel(x_ref, xp_ref, xn_ref, nrm_ref, wup_ref, cw_ref, cb_ref, wdn_ref, fin_ref, o_ref, us, act,
                *, tm, tps, final):
    jj = pl.program_id(0) % tps
    xa = jnp.concatenate([
        jnp.where(jj == 0, 0.0, xp_ref[...]),
        x_ref[...],
        jnp.where(jj == tps - 1, 0.0, xn_ref[...]),
    ], axis=0)
    o_ref[...] = _ffn_body(xa, nrm_ref, wup_ref, cw_ref, cb_ref, wdn_ref, fin_ref, us, act, tm=tm, final=final)


def _ffn_scratch(tm):
    return [pltpu.VMEM((2, FF_COLS // LANES, ROW_PITCH * (tm + 2 * HALO), LANES), F32),
            pltpu.VMEM((tm, D_FF), BF16)]


def _ab_out_ffn_kernel(x_ref, xp_ref, xn_ref, ya_ref, yap_ref, yan_ref, hf_ref, hfp_ref, hfn_ref, hb_ref, hbp_ref,
                       hbn_ref, mo_ref, mop_ref, mon_ref, on_ref, wa_ref, wm_ref, nrm_ref, wup_ref, cw_ref, cb_ref,
                       wdn_ref, fin_ref, o_ref, us, act, *, tm, tps):
    jj = pl.program_id(0) % tps

    def rows(main, prev, nxt):
        return jnp.concatenate([main[...], prev[...], nxt[...]], axis=0)

    hf, hb, mo = rows(hf_ref, hfp_ref, hfn_ref), rows(hb_ref, hbp_ref, hbn_ref), rows(mo_ref, mop_ref, mon_ref)
    parts = []
    for hh in range(M_HEADS):
        hc = slice(hh * LANES, (hh + 1) * LANES)
        t = hf[:, hc] + hb[:, hc]
        t = t * lax.rsqrt(jnp.mean(t * t, axis=-1, keepdims=True) + EPS) * on_ref[:, hc]
        parts.append((jax.nn.sigmoid(mo[:, hc]) * t).astype(BF16))
    ym = jnp.concatenate(parts, axis=1)
    x1 = (rows(x_ref, xp_ref, xn_ref) + _dot(rows(ya_ref, yap_ref, yan_ref), wa_ref[...]) + _dot(ym, wm_ref[...]))
    xa = jnp.concatenate([
        jnp.where(jj == 0, 0.0, x1[tm + PAIR - HALO:tm + PAIR]),
        x1[0:tm],
        jnp.where(jj == tps - 1, 0.0, x1[tm + PAIR:tm + PAIR + HALO]),
    ], axis=0)
    o_ref[...] = _ffn_body(xa, nrm_ref, wup_ref, cw_ref, cb_ref, wdn_ref, fin_ref, us, act, tm=tm, final=False)


def _ab_out_ffn(x, ya, hf, hb, mo, on, wa, wm, nrm, wup, cw, cb, wdn, fin, *, seq, tm):
    n = x.shape[0]
    row = lambda i: (i, 0)

    def with_halo(width):
        return [pl.BlockSpec((tm, width), row), *_halo_specs(tm, n, width, PAIR)]

    consts = [on, wa, wm, nrm, wup, cw, cb, wdn, fin]
    return pl.pallas_call(
        functools.partial(_ab_out_ffn_kernel, tm=tm, tps=seq // tm),
        grid=(n // tm,),
        in_specs=with_halo(D_MODEL) + 4 * with_halo(M_WIDTH) + [_const_spec(c.shape) for c in consts],
        out_specs=pl.BlockSpec((tm, D_MODEL), row),
        out_shape=jax.ShapeDtypeStruct((n, D_MODEL), F32),
        scratch_shapes=_ffn_scratch(tm),
        compiler_params=_params("parallel"),
        name="ab_out_ffn",
    )(x, x, x, ya, ya, ya, hf, hf, hf, hb, hb, hb, mo, mo, mo, *consts)


def _ffn(x, nrm, wup, cw, cb, wdn, fin, *, seq, tm, final):
    n = x.shape[0]
    row = lambda i: (i, 0)
    return pl.pallas_call(
        functools.partial(_ffn_kernel, tm=tm, tps=seq // tm, final=final),
        grid=(n // tm,),
        in_specs=[
            pl.BlockSpec((tm, D_MODEL), row), *_halo_specs(tm, n, D_MODEL),
            _const_spec(nrm.shape), _const_spec(wup.shape), _const_spec(cw.shape), _const_spec(cb.shape),
            _const_spec(wdn.shape), _const_spec(fin.shape),
        ],
        out_specs=pl.BlockSpec((tm, D_MODEL), row),
        out_shape=jax.ShapeDtypeStruct((n, D_MODEL), F32),
        scratch_shapes=_ffn_scratch(tm),
        compiler_params=_params("parallel"),
        name="ffn_final" if final else "ffn",
    )(x, x, x, nrm, wup, cw, cb, wdn, fin)


def _gelu(x):
    return 0.5 * x * (1.0 + lax.erf(x * (2.0 ** -0.5)))


def _mix_c_kernel(x_ref, nrm_ref, win_ref, vn_ref, ws_ref, bs_ref, wout_ref, o_ref, zs, *, tm):
    x = x_ref[...]
    xn = _rms(x, nrm_ref[...]).astype(BF16)
    v = _gelu(_dot(xn, win_ref[:, D_MODEL:2 * D_MODEL]))
    u = _gelu(_dot(xn, win_ref[:, 0:D_MODEL]))
    vn = _rms(v, vn_ref[...]).astype(BF16)
    nch = tm // CHUNK
    for g in range(G_GROUPS):
        cols = slice(g * LANES, (g + 1) * LANES)
        rhs = jnp.concatenate([vn[c * CHUNK:(c + 1) * CHUNK, cols] for c in range(nch)], axis=1)
        sv = _dot(ws_ref[g], rhs) + bs_ref[:, g:g + 1]
        for c in range(nch):
            zs[c * CHUNK:(c + 1) * CHUNK, cols] = (
                u[c * CHUNK:(c + 1) * CHUNK, cols] * sv[:, c * LANES:(c + 1) * LANES]).astype(BF16)
    o_ref[...] = x + _dot(zs[...], wout_ref[...])


def _mix_c(x, nrm, win, vn, ws, bs, wout, *, tm):
    n = x.shape[0]
    row = lambda i: (i, 0)
    return pl.pallas_call(
        functools.partial(_mix_c_kernel, tm=tm),
        grid=(n // tm,),
        in_specs=[
            pl.BlockSpec((tm, D_MODEL), row),
            _const_spec(nrm.shape), _const_spec(win.shape), _const_spec(vn.shape), _const_spec(ws.shape),
            _const_spec(bs.shape), _const_spec(wout.shape),
        ],
        out_specs=pl.BlockSpec((tm, D_MODEL), row),
        out_shape=jax.ShapeDtypeStruct((n, D_MODEL), F32),
        scratch_shapes=[pltpu.VMEM((tm, D_MODEL), BF16)],
        compiler_params=_params("parallel"),
        name="mix_c",
    )(x, nrm, win, vn, ws, bs, wout)


def _rope_tables(seq):
    pos = jnp.arange(seq, dtype=F32)
    inv = 1.0 / (ROPE_THETA ** (jnp.arange(0, QK_ROPE, 2, dtype=F32) / QK_ROPE))
    ang = pos[:, None] * inv[None, :]
    cos, sin = jnp.cos(ang), jnp.sin(ang)
    one = jnp.ones((seq, QK_NOPE), F32)
    zero_n = jnp.zeros((seq, QK_NOPE), F32)
    pad = jnp.zeros((seq, LANES - QK_NOPE - QK_ROPE), F32)
    return (jnp.concatenate([one, cos, cos, pad], axis=1), jnp.concatenate([zero_n, sin, sin, pad], axis=1))


def _head_block(nope, x1, x2):
    pad = jnp.zeros((nope.shape[0], LANES - QK_NOPE - QK_ROPE), nope.dtype)
    return jnp.concatenate([nope, x1, x2, pad], axis=1)


def _prep_even(i, ab_norm, ab_w_in, mla_q_norm, mla_w_uq, mla_kv_norm, mla_w_ukv, mlstm_conv_w, mlstm_conv_b,
               mlstm_gate_bias, mlstm_out_norm, ab_w_out):
    w_in = ab_w_in[i]
    half = QK_ROPE // 2
    zq = jnp.zeros((D_MODEL, QK_NOPE), F32)
    kr1 = w_in[:, 384:384 + half]
    kr2 = w_in[:, 384 + half:416]
    gate_order = jnp.array([0, 1, 2, 3, 8, 9, 10, 11, 4, 5, 6, 7, 12, 13, 14, 15])
    gates = jnp.pad(w_in[:, 2464:2480][:, gate_order], ((0, 0), (0, LANES - 2 * N_STREAMS)))
    w_big = jnp.concatenate([
        w_in[:, 0:384],
        _head_block(zq, kr1, kr2),
        _head_block(zq, -kr2, kr1),
        gates,
        w_in[:, 416:2464],
    ], axis=1).astype(BF16)
    w_uq = mla_w_uq[i].reshape(Q_LORA, MLA_HEADS, QK_NOPE + QK_ROPE)
    zn = jnp.zeros((Q_LORA, QK_NOPE), F32)
    wqa = jnp.concatenate([_head_block(w_uq[:, h, :QK_NOPE], w_uq[:, h, QK_NOPE:QK_NOPE + half],
                                       w_uq[:, h, QK_NOPE + half:]) for h in range(MLA_HEADS)], axis=1)
    wqb = jnp.concatenate([_head_block(zn, -w_uq[:, h, QK_NOPE + half:], w_uq[:, h, QK_NOPE:QK_NOPE + half])
                           for h in range(MLA_HEADS)], axis=1)
    w_ukv = mla_w_ukv[i].reshape(KV_LORA, MLA_HEADS, QK_NOPE + V_HEAD)
    wka = jnp.pad(w_ukv[:, :, :QK_NOPE], ((0, 0), (0, 0), (0, LANES - QK_NOPE))).reshape(KV_LORA, MLA_HEADS * LANES)
    zv = jnp.zeros((KV_LORA, V_HEAD), F32)
    wv = jnp.concatenate([jnp.concatenate([w_ukv[:, h, QK_NOPE:], zv] if h % 2 == 0 else [zv, w_ukv[:, h, QK_NOPE:]],
                                          axis=1) for h in range(MLA_HEADS)], axis=1)
    one_even = jnp.zeros((LANES,), F32).at[V_HEAD].set(1.0)
    one_odd = jnp.zeros((LANES,), F32).at[0].set(1.0)
    vone = jnp.concatenate([one_even if h % 2 == 0 else one_odd for h in range(MLA_HEADS)])[None, :]
    w_out = ab_w_out[i].astype(BF16)
    return dict(
        nrm=ab_norm[i][None, :], w_big=w_big, qn=mla_q_norm[i][None, :], wqa=wqa.astype(BF16),
        wqb=wqb.astype(BF16), kvn=mla_kv_norm[i][None, :], wka=wka.astype(BF16), wv=wv.astype(BF16),
        gb=jnp.broadcast_to(mlstm_gate_bias[i][gate_order][:, None], (2 * N_STREAMS, LANES)), vone=vone,
        conv_w=mlstm_conv_w[i], conv_b=mlstm_conv_b[i][None, :], on=mlstm_out_norm[i][None, :],
        wa=w_out[:MLA_HEADS * V_HEAD], wm=w_out[MLA_HEADS * V_HEAD:])


def _pick(pref, seq):
    return min(pref, seq)


def _forward(x3, rope, even, odd, ffn, final_norm):
    batch, seq, _ = x3.shape
    x = x3.reshape(batch * seq, D_MODEL)
    tm = _pick(512, seq)
    cos_t, sin_t = rope
    e = even
    q, k, v, mq, mk, mv, mo, g = _ab_in(x, cos_t, sin_t, e, seq=seq, tm=_pick(1024, seq))
    tq = _pick(ATTN_TQ, seq)
    score_bytes = 2 * tq * seq * 4
    pair_bytes = 2 * 2 * seq * 2 * LANES * 2
    pairs = min(MLA_HEADS // 2, max(1, (ATTN_VMEM - score_bytes) // pair_bytes))
    while (MLA_HEADS // 2) % pairs:
        pairs -= 1
    ya = _attn(q, k, v, batch=batch, seq=seq, tq=tq, pairs=pairs)
    c_rows, dec, cols = _gates(g, batch=batch, seq=seq)
    hf, hb = _mlstm(mq, mk, mv, c_rows, dec, cols, batch=batch, seq=seq)
    f = ffn[0]
    x = _ab_out_ffn(x, ya, hf, hb, mo, e["on"], e["wa"], e["wm"], f["nrm"], f["wup"], f["cw"], f["cb"], f["wdn"],
                    final_norm, seq=seq, tm=tm)
    o = odd
    x = _mix_c(x, o["nrm"], o["win"], o["vn"], o["ws"], o["bs"], o["wout"], tm=_pick(1024, seq))
    f = ffn[1]
    x = _ffn(x, f["nrm"], f["wup"], f["cw"], f["cb"], f["wdn"], final_norm, seq=seq, tm=_pick(1024, seq), final=True)
    return x.reshape(batch, seq, D_MODEL)


def _prep_rest(c_norm, c_w_in, c_v_norm, c_w_spatial, c_b_spatial, c_w_out, ffn_norm, ffn_w_up, ffn_conv_w,
               ffn_conv_b, ffn_w_down):
    odd = dict(nrm=c_norm[0][None, :], win=c_w_in[0].astype(BF16), vn=c_v_norm[0][None, :],
               ws=c_w_spatial[0].astype(BF16), bs=c_b_spatial[0].T, wout=c_w_out[0].astype(BF16))
    ffn = [dict(nrm=ffn_norm[l][None, :], wup=ffn_w_up[l].astype(BF16), cw=ffn_conv_w[l], cb=ffn_conv_b[l][None, :],
                wdn=ffn_w_down[l].astype(BF16)) for l in range(2)]
    return odd, ffn


def kernel(x_prompt, x_sample, ab_norm, ab_w_in, mla_q_norm, mla_w_uq, mla_kv_norm, mla_w_ukv, mlstm_conv_w,
           mlstm_conv_b, mlstm_gate_bias, mlstm_out_norm, ab_w_out, c_norm, c_w_in, c_v_norm, c_w_spatial,
           c_b_spatial, c_w_out, ffn_norm, ffn_w_up, ffn_conv_w, ffn_conv_b, ffn_w_down, final_norm):
    even = _prep_even(0, ab_norm, ab_w_in, mla_q_norm, mla_w_uq, mla_kv_norm, mla_w_ukv, mlstm_conv_w, mlstm_conv_b,
                      mlstm_gate_bias, mlstm_out_norm, ab_w_out)
    odd, ffn = _prep_rest(c_norm, c_w_in, c_v_norm, c_w_spatial, c_b_spatial, c_w_out, ffn_norm, ffn_w_up,
                          ffn_conv_w, ffn_conv_b, ffn_w_down)
    fin = final_norm[None, :]
    rope = _rope_tables(max(x_prompt.shape[1], x_sample.shape[1]))
    return (_forward(x_prompt, rope, even, odd, ffn, fin), _forward(x_sample, rope, even, odd, ffn, fin))
```

```python
import functools

import jax
import jax.numpy as jnp
from jax import lax
from jax.experimental import pallas as pl
from jax.experimental.pallas import tpu as pltpu

F32 = jnp.float32
BF16 = jnp.bfloat16

EPS = 1e-6
D_MODEL = 1024
MLA_HEADS = 8
Q_LORA = 256
KV_LORA = 128
QK_NOPE = 64
QK_ROPE = 32
V_HEAD = 64
ROPE_THETA = 10000.0
M_HEADS = 4
M_HEAD_DIM = 128
M_WIDTH = M_HEADS * M_HEAD_DIM
N_STREAMS = 2 * M_HEADS
CHUNK = 128
G_GROUPS = 8
D_FF = 2816
LANES = 128
SUBLANES = 8
HALO = SUBLANES
BLOCK_CHUNKS = 8
PAIR = 2 * SUBLANES
MXU_COLS = 256
ROW_PITCH = 2
FF_COLS = MXU_COLS
KEY_TILE = MXU_COLS
ATTN_TQ = 512
ATTN_VMEM = 48 * 1024 * 1024
LOG2_E = 1.4426950408889634
VMEM_LIMIT = 56 * 1024 * 1024

_C_Q, _C_KV, _C_KRA, _C_KRB, _C_G, _C_MQK, _C_MV, _C_MO, _C_END = (
    0, 256, 384, 512, 640, 768, 1792, 2304, 2816)


def _rms(x, g):
    return x * lax.rsqrt(jnp.mean(x * x, axis=-1, keepdims=True) + EPS) * g


def _const_spec(shape):
    nd = len(shape)
    return pl.BlockSpec(shape, lambda *_: (0,) * nd, pipeline_mode=pl.Buffered(1))


def _params(*sem):
    return pltpu.CompilerParams(dimension_semantics=sem, vmem_limit_bytes=VMEM_LIMIT)


def _dot(a, b):
    return jnp.dot(a, b, preferred_element_type=F32)


def _dot_nt(a, b):
    return lax.dot_general(a, b, (((1,), (1,)), ((), ())), preferred_element_type=F32)


def _halo_specs(tile_rows, n_rows, width, halo=HALO):
    tb = tile_rows // halo
    last = n_rows // halo - 1
    return (pl.BlockSpec((halo, width), lambda i: (jnp.maximum(i * tb - 1, 0), 0)),
            pl.BlockSpec((halo, width), lambda i: (jnp.minimum((i + 1) * tb, last), 0)))


def _conv3(y, u, rows, w_ref, b_ref, c0):
    tall = rows + 2 * HALO
    outs = []
    for s in range(y.shape[1] // LANES):
        lanes = slice(s * LANES, (s + 1) * LANES)
        cols = slice(c0 + s * LANES, c0 + (s + 1) * LANES)
        u[s, pl.ds(0, tall, stride=ROW_PITCH), :] = y[:, lanes]
        prv = u[s, pl.ds(ROW_PITCH * (HALO - 1), rows, stride=ROW_PITCH), :]
        nxt = u[s, pl.ds(ROW_PITCH * (HALO + 1), rows, stride=ROW_PITCH), :]
        outs.append(prv * w_ref[0:1, cols] + y[HALO:HALO + rows, lanes] * w_ref[1:2, cols]
                    + nxt * w_ref[2:3, cols] + b_ref[:, cols])
    return jnp.concatenate(outs, axis=1)


def _ab_in_kernel(x_ref, xp_ref, xn_ref, cos_ref, sin_ref, nrm_ref, w_ref, qn_ref, wqa_ref, wqb_ref, kvn_ref,
                  wka_ref, wv_ref, gb_ref, cw_ref, cb_ref,
                  q_ref, k_ref, v_ref, mq_ref, mk_ref, mv_ref, mo_ref, g_ref, us, *, tm, tps):
    jj = pl.program_id(0) % tps
    xa = jnp.concatenate([
        jnp.where(jj == 0, 0.0, xp_ref[...]),
        x_ref[...],
        jnp.where(jj == tps - 1, 0.0, xn_ref[...]),
    ], axis=0)
    xe = _rms(xa, nrm_ref[...]).astype(BF16)
    xn = xe[HALO:HALO + tm]

    def proj(a, b):
        return _dot(xn, w_ref[:, a:b])

    cq = proj(_C_Q, _C_KV)
    ckv_kra = proj(_C_KV, _C_KRB)
    krb_g = proj(_C_KRB, _C_MQK)

    nq_blocks = M_WIDTH // MXU_COLS
    for cb in range(2 * nq_blocks):
        cols = slice(cb * MXU_COLS, (cb + 1) * MXU_COLS)
        y = _conv3(_dot(xe, w_ref[:, _C_MQK + cb * MXU_COLS:_C_MQK + (cb + 1) * MXU_COLS]), us.at[cb % 2], tm,
                   cw_ref, cb_ref, cb * MXU_COLS)
        y = y * jax.nn.sigmoid(y)
        if cb < nq_blocks:
            mq_ref[:, cols] = y.astype(BF16)
        else:
            mk_ref[:, (cb - nq_blocks) * MXU_COLS:(cb - nq_blocks + 1) * MXU_COLS] = (
                y * (M_HEAD_DIM ** -0.5)).astype(BF16)
    mv_ref[...] = proj(_C_MV, _C_MO).astype(BF16)
    mo_ref[...] = proj(_C_MO, _C_END)
    gt = krb_g[:, LANES:2 * LANES].T
    for c in range(tm // CHUNK):
        g_ref[c] = gt[0:2 * N_STREAMS, c * CHUNK:(c + 1) * CHUNK] + gb_ref[:, 0:1]

    cos = cos_ref[...]
    sin = sin_ref[...]
    scale = (QK_NOPE + QK_ROPE) ** -0.5 * LOG2_E
    cqn = _rms(cq, qn_ref[...]).astype(BF16)
    qa = _dot(cqn, wqa_ref[...])
    qb = _dot(cqn, wqb_ref[...])
    cos_q = cos * scale
    sin_q = sin * scale
    for h in range(MLA_HEADS):
        blk = slice(h * LANES, (h + 1) * LANES)
        q_ref[:, blk] = (qa[:, blk] * cos_q + qb[:, blk] * sin_q).astype(BF16)

    ckvn = _rms(ckv_kra[:, 0:LANES], kvn_ref[...]).astype(BF16)
    ka = _dot(ckvn, wka_ref[...])
    kr = ckv_kra[:, LANES:2 * LANES] * cos + krb_g[:, 0:LANES] * sin
    for h in range(MLA_HEADS):
        blk = slice(h * LANES, (h + 1) * LANES)
        k_ref[:, blk] = (ka[:, blk] + kr).astype(BF16)
    vd = _dot(ckvn, wv_ref[...])
    lane = lax.broadcasted_iota(jnp.int32, (tm, LANES), 1)
    for p in range(MLA_HEADS // 2):
        pair = vd[:, p * LANES:(p + 1) * LANES]
        v_ref[:, 2 * p * LANES:(2 * p + 1) * LANES] = jnp.where(
            lane < V_HEAD, pair, (lane == V_HEAD).astype(F32)).astype(BF16)
        v_ref[:, (2 * p + 1) * LANES:(2 * p + 2) * LANES] = jnp.where(
            lane >= V_HEAD, pair, (lane == 0).astype(F32)).astype(BF16)


def _ab_in(x, cos_t, sin_t, e, *, seq, tm):
    n = x.shape[0]
    nt = n // tm
    tps = seq // tm
    row = lambda i: (i, 0)
    tab = lambda i: (i % tps, 0)
    consts = [e[k] for k in ("nrm", "w_big", "qn", "wqa", "wqb", "kvn", "wka", "wv", "gb", "conv_w",
                             "conv_b")]
    out_shape = (
        jax.ShapeDtypeStruct((n, MLA_HEADS * LANES), BF16),
        jax.ShapeDtypeStruct((n, MLA_HEADS * LANES), BF16),
        jax.ShapeDtypeStruct((n, MLA_HEADS * LANES), BF16),
        jax.ShapeDtypeStruct((n, M_WIDTH), BF16),
        jax.ShapeDtypeStruct((n, M_WIDTH), BF16),
        jax.ShapeDtypeStruct((n, M_WIDTH), BF16),
        jax.ShapeDtypeStruct((n, M_WIDTH), F32),
        jax.ShapeDtypeStruct((n // CHUNK, 2 * N_STREAMS, CHUNK), F32),
    )
    wide = pl.BlockSpec((tm, MLA_HEADS * LANES), row)
    narrow = pl.BlockSpec((tm, M_WIDTH), row)
    return pl.pallas_call(
        functools.partial(_ab_in_kernel, tm=tm, tps=tps),
        grid=(nt,),
        in_specs=[pl.BlockSpec((tm, D_MODEL), row), *_halo_specs(tm, n, D_MODEL),
                  pl.BlockSpec((tm, LANES), tab), pl.BlockSpec((tm, LANES), tab)]
                 + [_const_spec(c.shape) for c in consts],
        out_specs=(wide, wide, wide, narrow, narrow, narrow, narrow,
                   pl.BlockSpec((tm // CHUNK, 2 * N_STREAMS, CHUNK), lambda i: (i, 0, 0))),
        out_shape=out_shape,
        scratch_shapes=[pltpu.VMEM((2, MXU_COLS // LANES, ROW_PITCH * (tm + 2 * HALO), LANES), F32)],
        compiler_params=_params("parallel"),
        name="ab_in",
    )(x, x, x, cos_t, sin_t, *consts)


def _attn_kernel(q_ref, k_ref, v_ref, o_ref, s_scr, *, tq, pairs):
    nt = k_ref.shape[0] // KEY_TILE
    lane = lax.broadcasted_iota(jnp.int32, (tq, LANES), 1)
    for pr in range(pairs):
        mbs = []
        for h in range(2):
            hc = slice((2 * pr + h) * LANES, (2 * pr + h + 1) * LANES)
            q = q_ref[:, hc]
            mx = jnp.full((tq, LANES), -jnp.inf, F32)
            for t in range(nt):
                keys = slice(t * KEY_TILE, (t + 1) * KEY_TILE)
                s = _dot_nt(q, k_ref[keys, hc])
                s_scr[h, :, keys] = s
                for j in range(KEY_TILE // LANES):
                    mx = jnp.maximum(mx, s[:, j * LANES:(j + 1) * LANES])
            mbs.append(jnp.broadcast_to(jnp.max(mx, axis=-1, keepdims=True), (tq, LANES)))
        outs = []
        for h in range(2):
            hc = slice((2 * pr + h) * LANES, (2 * pr + h + 1) * LANES)
            mb = jnp.concatenate([mbs[h]] * (KEY_TILE // LANES), axis=1)
            acc = jnp.zeros((tq, LANES), F32)
            for t in range(nt):
                keys = slice(t * KEY_TILE, (t + 1) * KEY_TILE)
                p = jnp.exp2(s_scr[h, :, keys] - mb).astype(BF16)
                acc = acc + _dot(p, v_ref[keys, hc])
            ones_lane = V_HEAD if h == 0 else 0
            outs.append(acc / acc[:, ones_lane:ones_lane + 1])
        o_ref[:, pr * LANES:(pr + 1) * LANES] = jnp.where(lane < V_HEAD, outs[0], outs[1]).astype(BF16)


def _attn(q, k, v, *, batch, seq, tq, pairs):
    n = q.shape[0]
    nq = seq // tq
    steps = MLA_HEADS // 2 // pairs
    return pl.pallas_call(
        functools.partial(_attn_kernel, tq=tq, pairs=pairs),
        grid=(batch, steps, nq),
        in_specs=[
            pl.BlockSpec((tq, 2 * pairs * LANES), lambda b, p, i: (b * nq + i, p)),
            pl.BlockSpec((seq, 2 * pairs * LANES), lambda b, p, i: (b, p)),
            pl.BlockSpec((seq, 2 * pairs * LANES), lambda b, p, i: (b, p)),
        ],
        out_specs=pl.BlockSpec((tq, pairs * LANES), lambda b, p, i: (b * nq + i, p)),
        out_shape=jax.ShapeDtypeStruct((n, MLA_HEADS * V_HEAD), BF16),
        scratch_shapes=[pltpu.VMEM((2, tq, seq), F32)],
        compiler_params=_params("parallel", "parallel", "arbitrary"),
        name="attn",
    )(q, k, v)


def _log_sigmoid(x):
    return jnp.minimum(x, 0.0) - jnp.log1p(jnp.exp(-jnp.abs(x)))


def _gates_kernel(g_ref, c_ref, dec_ref, cols_ref, tot_s, mloc_s, mpf_s, mpb_s, *, nc):
    rows = nc * N_STREAMS
    shape = (rows, CHUNK)
    three = (nc, N_STREAMS, CHUNK)
    li = g_ref[:, 0:N_STREAMS, :].reshape(shape)
    lf = _log_sigmoid(g_ref[:, N_STREAMS:2 * N_STREAMS, :].reshape(shape))
    fwd = (lax.broadcasted_iota(jnp.int32, shape, 0) & M_HEADS) == 0
    lane = lax.broadcasted_iota(jnp.int32, shape, 1)
    shifts = (1, 2, 4, 8, 16, 32, 64)

    pre = lf
    for sh in shifts:
        pre = pre + jnp.where(lane >= sh, pltpu.roll(pre, sh, axis=1), 0.0)
    tot = jnp.broadcast_to(pre[:, CHUNK - 1:CHUNK], shape)
    b = jnp.where(fwd, pre, tot - pre + lf)
    c = li - b
    run_f = c
    run_b = c
    for sh in shifts:
        run_f = jnp.maximum(run_f, jnp.where(lane >= sh, pltpu.roll(run_f, sh, axis=1), -jnp.inf))
        run_b = jnp.maximum(run_b, jnp.where(lane < CHUNK - sh, pltpu.roll(run_b, CHUNK - sh, axis=1), -jnp.inf))
    run = jnp.where(fwd, run_f, run_b)
    m_loc = tot + jnp.broadcast_to(jnp.max(c, axis=1, keepdims=True), shape)

    tot_s[...] = tot.reshape(three)
    mloc_s[...] = m_loc.reshape(three)
    fwd8 = lax.broadcasted_iota(jnp.int32, (N_STREAMS, CHUNK), 0) < M_HEADS

    def step(i, m):
        jb = nc - 1 - i
        mpf_s[i] = m
        mpb_s[jb] = m
        return jnp.maximum(jnp.where(fwd8, tot_s[i], tot_s[jb]) + m, jnp.where(fwd8, mloc_s[i], mloc_s[jb]))

    lax.fori_loop(0, nc, step, jnp.zeros((N_STREAMS, CHUNK), F32))
    m_prev = jnp.where(fwd, mpf_s[...].reshape(shape), mpb_s[...].reshape(shape))

    m_run = jnp.maximum(m_prev, run)
    floor = jnp.exp(-(b + m_run))
    w = jnp.exp(tot + c - m_loc)
    m_new = jnp.maximum(tot + m_prev, m_loc)
    c_ref[:, 0:N_STREAMS, :] = c.reshape(three)
    c_ref[:, N_STREAMS:2 * N_STREAMS, :] = w.reshape(three)
    c_ref[:, 2 * N_STREAMS:3 * N_STREAMS, :] = m_prev.reshape(three)
    dec_ref[:, 0:N_STREAMS, :] = jnp.exp(tot + m_prev - m_new).reshape(three)
    dec_ref[:, N_STREAMS:2 * N_STREAMS, :] = jnp.exp(m_loc - m_new).reshape(three)
    quantities = [a.reshape(three) for a in (m_run, floor)]
    pad_rows = CHUNK - BLOCK_CHUNKS * 2 * N_STREAMS
    pad = [jnp.zeros((pad_rows, CHUNK), F32)] if pad_rows else []
    for blk in range(nc // BLOCK_CHUNKS):
        tile = jnp.concatenate(
            [a[blk * BLOCK_CHUNKS + cq] for cq in range(BLOCK_CHUNKS) for a in quantities] + pad, axis=0)
        cols_ref[blk] = tile.T


def _gates(g, *, batch, seq):
    nc = seq // CHUNK
    nchunks = g.shape[0]
    three = (nc, N_STREAMS, CHUNK)
    return pl.pallas_call(
        functools.partial(_gates_kernel, nc=nc),
        grid=(batch,),
        in_specs=[pl.BlockSpec((nc, 2 * N_STREAMS, CHUNK), lambda b: (b, 0, 0))],
        out_specs=(
            pl.BlockSpec((nc, 3 * N_STREAMS, CHUNK), lambda b: (b, 0, 0)),
            pl.BlockSpec((nc, 2 * N_STREAMS, CHUNK), lambda b: (b, 0, 0)),
            pl.BlockSpec((nc // BLOCK_CHUNKS, CHUNK, LANES), lambda b: (b, 0, 0)),
        ),
        out_shape=(
            jax.ShapeDtypeStruct((nchunks, 3 * N_STREAMS, CHUNK), F32),
            jax.ShapeDtypeStruct((nchunks, 2 * N_STREAMS, CHUNK), F32),
            jax.ShapeDtypeStruct((nchunks // BLOCK_CHUNKS, CHUNK, LANES), F32),
        ),
        scratch_shapes=[pltpu.VMEM(three, F32)] * 4,
        compiler_params=_params("parallel"),
        name="gates",
    )(g)


def _mlstm_kernel(*refs):
    ins, (hf_ref, hb_ref, st, bc) = (refs[0:6], refs[6:12]), refs[12:]

    @pl.when(pl.program_id(1) == 0)
    def _():
        st[...] = jnp.zeros_like(st)

    row = lax.broadcasted_iota(jnp.int32, (CHUNK, CHUNK), 0)
    col = lax.broadcasted_iota(jnp.int32, (CHUNK, CHUNK), 1)
    masks = (col <= row, col >= row)
    ones = (col == 0).astype(BF16)
    orders = (range(BLOCK_CHUNKS), range(BLOCK_CHUNKS - 1, -1, -1))
    heads = range(M_HEADS)
    hcs = [slice(hh * LANES, (hh + 1) * LANES) for hh in heads]
    for d in range(2):
        cols_t = ins[d][5][0]
        for cq in range(BLOCK_CHUNKS):
            for hh in heads:
                base = cq * 2 * N_STREAMS + d * M_HEADS + hh
                for qi in range(2):
                    lane_i = base + qi * N_STREAMS
                    bc[d, (qi * BLOCK_CHUNKS + cq) * M_HEADS + hh] = jnp.broadcast_to(
                        cols_t[:, lane_i:lane_i + 1], (CHUNK, LANES))
    qs, v1s, qk, c_loc = {}, {}, {}, {}
    for d in range(2):
        q_ref, k_ref = ins[d][0], ins[d][1]
        for cq in orders[d]:
            rows = slice(cq * CHUNK, (cq + 1) * CHUNK)
            for hh in heads:
                qs[d, cq, hh] = q_ref[rows, hcs[hh]]
                qk[d, cq, hh] = _dot_nt(qs[d, cq, hh], k_ref[rows, hcs[hh]])
    for d in range(2):
        k_ref, v_ref, c_ref = ins[d][1], ins[d][2], ins[d][3]
        for cq in orders[d]:
            rows = slice(cq * CHUNK, (cq + 1) * CHUNK)
            for hh in heads:
                s = d * M_HEADS + hh
                v1s[d, cq, hh] = jnp.concatenate([v_ref[rows, hcs[hh]], ones], axis=1)
                kw_t = (k_ref[rows, hcs[hh]].astype(F32).T
                        * c_ref[cq, N_STREAMS + s:N_STREAMS + s + 1, :]).astype(BF16)
                c_loc[d, cq, hh] = _dot(kw_t, v1s[d, cq, hh])
    for step in range(BLOCK_CHUNKS):
        for d in range(2):
            _mlstm_chunk(d, orders[d][step], ins[d][3], ins[d][4], (hf_ref, hb_ref)[d], st, bc, masks[d], hcs,
                         qs, v1s, qk, c_loc)


def _mlstm_chunk(d, cq, c_ref, dec_ref, h_ref, st, bc, mask, hcs, qs, v1s, qk, c_loc):
    rows = slice(cq * CHUNK, (cq + 1) * CHUNK)
    for hh in range(M_HEADS):
        s = d * M_HEADS + hh
        key = (d, cq, hh)
        state = st[d, hh]
        m_run = bc[d, cq * M_HEADS + hh]
        floor = bc[d, (BLOCK_CHUNKS + cq) * M_HEADS + hh]
        c_ext = jnp.concatenate([c_ref[cq, s:s + 1, :], c_ref[cq, 2 * N_STREAMS + s:2 * N_STREAMS + s + 1, :]],
                                axis=1)
        d_ext = jnp.exp(c_ext - jnp.concatenate([m_run, m_run], axis=1))
        p_ext = jnp.concatenate([qk[key] * jnp.where(mask, d_ext[:, 0:LANES], 0.0),
                                 qs[key].astype(F32) * d_ext[:, LANES:2 * LANES]], axis=1).astype(BF16)
        nd = _dot(p_ext, jnp.concatenate([v1s[key], state.astype(BF16)], axis=0))
        den = nd[:, LANES:LANES + 1]
        h_ref[rows, hcs[hh]] = nd[:, 0:LANES] / jnp.maximum(jnp.abs(den), floor)
        st[d, hh] = (dec_ref[cq, s:s + 1, 0:1] * state
                     + dec_ref[cq, N_STREAMS + s:N_STREAMS + s + 1, 0:1] * c_loc[key])


def _mlstm(mq, mk, mv, c_rows, dec, cols, *, batch, seq):
    n = mq.shape[0]
    ts = BLOCK_CHUNKS * CHUNK
    nblk = seq // ts

    def specs(direction):
        def blk(b, i):
            return b * nblk + (i if direction == 0 else nblk - 1 - i)

        tok = pl.BlockSpec((ts, M_WIDTH), lambda b, i: (blk(b, i), 0))
        return tok, [tok, tok, tok,
                     pl.BlockSpec((BLOCK_CHUNKS, 3 * N_STREAMS, CHUNK), lambda b, i: (blk(b, i), 0, 0)),
                     pl.BlockSpec((BLOCK_CHUNKS, 2 * N_STREAMS, CHUNK), lambda b, i: (blk(b, i), 0, 0)),
                     pl.BlockSpec((1, CHUNK, LANES), lambda b, i: (blk(b, i), 0, 0))]

    (tok_f, in_f), (tok_b, in_b) = specs(0), specs(1)
    h_shape = jax.ShapeDtypeStruct((n, M_WIDTH), F32)
    args = (mq, mk, mv, c_rows, dec, cols)
    return pl.pallas_call(
        _mlstm_kernel,
        grid=(batch, nblk),
        in_specs=in_f + in_b,
        out_specs=(tok_f, tok_b),
        out_shape=(h_shape, h_shape),
        scratch_shapes=[pltpu.VMEM((2, M_HEADS, M_HEAD_DIM, 2 * LANES), F32),
                        pltpu.VMEM((2, 2 * BLOCK_CHUNKS * M_HEADS, CHUNK, LANES), F32)],
        compiler_params=_params("parallel", "arbitrary"),
        name="mlstm",
    )(*args, *args)


def _ffn_body(xa, nrm_ref, wup_ref, cw_ref, cb_ref, wdn_ref, fin_ref, us, act, *, tm, final):
    xn = _rms(xa, nrm_ref[...]).astype(BF16)
    for c in range(D_FF // FF_COLS):
        g0 = c * FF_COLS
        v0 = D_FF + c * FF_COLS
        gate = _conv3(_dot(xn, wup_ref[:, g0:g0 + FF_COLS]), us.at[0], tm, cw_ref, cb_ref, g0)
        val = _conv3(_dot(xn, wup_ref[:, v0:v0 + FF_COLS]), us.at[1], tm, cw_ref, cb_ref, v0)
        act[:, g0:g0 + FF_COLS] = (gate * jax.nn.sigmoid(gate) * val).astype(BF16)
    y = xa[HALO:HALO + tm] + _dot(act[...], wdn_ref[...])
    if final:
        y = _rms(y, fin_ref[...])
    return y


def _ffn_kernel(x_ref, xp_ref, xn_ref, nrm_ref, wup_ref, cw_ref, cb_ref, wdn_ref, fin_ref, o_ref, us, act,
                *, tm, tps, final):
    jj = pl.program_id(0) % tps
    xa = jnp.concatenate([
        jnp.where(jj == 0, 0.0, xp_ref[...]),
        x_ref[...],
        jnp.where(jj == tps - 1, 0.0, xn_ref[...]),
    ], axis=0)
    o_ref[...] = _ffn_body(xa, nrm_ref, wup_ref, cw_ref, cb_ref, wdn_ref, fin_ref, us, act, tm=tm, final=final)


def _ffn_scratch(tm):
    return [pltpu.VMEM((2, FF_COLS // LANES, ROW_PITCH * (tm + 2 * HALO), LANES), F32),
            pltpu.VMEM((tm, D_FF), BF16)]


def _ab_out_ffn_kernel(x_ref, xp_ref, xn_ref, ya_ref, yap_ref, yan_ref, hf_ref, hfp_ref, hfn_ref, hb_ref, hbp_ref,
                       hbn_ref, mo_ref, mop_ref, mon_ref, on_ref, wa_ref, wm_ref, nrm_ref, wup_ref, cw_ref, cb_ref,
                       wdn_ref, fin_ref, o_ref, us, act, *, tm, tps):
    jj = pl.program_id(0) % tps

    def rows(main, prev, nxt):
        return jnp.concatenate([main[...], prev[...], nxt[...]], axis=0)

    hf, hb, mo = rows(hf_ref, hfp_ref, hfn_ref), rows(hb_ref, hbp_ref, hbn_ref), rows(mo_ref, mop_ref, mon_ref)
    parts = []
    for hh in range(M_HEADS):
        hc = slice(hh * LANES, (hh + 1) * LANES)
        t = hf[:, hc] + hb[:, hc]
        t = t * lax.rsqrt(jnp.mean(t * t, axis=-1, keepdims=True) + EPS) * on_ref[:, hc]
        parts.append((jax.nn.sigmoid(mo[:, hc]) * t).astype(BF16))
    ym = jnp.concatenate(parts, axis=1)
    x1 = (rows(x_ref, xp_ref, xn_ref) + _dot(rows(ya_ref, yap_ref, yan_ref), wa_ref[...]) + _dot(ym, wm_ref[...]))
    xa = jnp.concatenate([
        jnp.where(jj == 0, 0.0, x1[tm + PAIR - HALO:tm + PAIR]),
        x1[0:tm],
        jnp.where(jj == tps - 1, 0.0, x1[tm + PAIR:tm + PAIR + HALO]),
    ], axis=0)
    o_ref[...] = _ffn_body(xa, nrm_ref, wup_ref, cw_ref, cb_ref, wdn_ref, fin_ref, us, act, tm=tm, final=False)


def _ab_out_ffn(x, ya, hf, hb, mo, on, wa, wm, nrm, wup, cw, cb, wdn, fin, *, seq, tm):
    n = x.shape[0]
    row = lambda i: (i, 0)

    def with_halo(width):
        return [pl.BlockSpec((tm, width), row), *_halo_specs(tm, n, width, PAIR)]

    consts = [on, wa, wm, nrm, wup, cw, cb, wdn, fin]
    return pl.pallas_call(
        functools.partial(_ab_out_ffn_kernel, tm=tm, tps=seq // tm),
        grid=(n // tm,),
        in_specs=with_halo(D_MODEL) + 4 * with_halo(M_WIDTH) + [_const_spec(c.shape) for c in consts],
        out_specs=pl.BlockSpec((tm, D_MODEL), row),
        out_shape=jax.ShapeDtypeStruct((n, D_MODEL), F32),
        scratch_shapes=_ffn_scratch(tm),
        compiler_params=_params("parallel"),
        name="ab_out_ffn",
    )(x, x, x, ya, ya, ya, hf, hf, hf, hb, hb, hb, mo, mo, mo, *consts)


def _ffn(x, nrm, wup, cw, cb, wdn, fin, *, seq, tm, final):
    n = x.shape[0]
    row = lambda i: (i, 0)
    return pl.pallas_call(
        functools.partial(_ffn_kernel, tm=tm, tps=seq // tm, final=final),
        grid=(n // tm,),
        in_specs=[
            pl.BlockSpec((tm, D_MODEL), row), *_halo_specs(tm, n, D_MODEL),
            _const_spec(nrm.shape), _const_spec(wup.shape), _const_spec(cw.shape), _const_spec(cb.shape),
            _const_spec(wdn.shape), _const_spec(fin.shape),
        ],
        out_specs=pl.BlockSpec((tm, D_MODEL), row),
        out_shape=jax.ShapeDtypeStruct((n, D_MODEL), F32),
        scratch_shapes=_ffn_scratch(tm),
        compiler_params=_params("parallel"),
        name="ffn_final" if final else "ffn",
    )(x, x, x, nrm, wup, cw, cb, wdn, fin)


def _gelu(x):
    return 0.5 * x * (1.0 + lax.erf(x * (2.0 ** -0.5)))


def _mix_c_kernel(x_ref, nrm_ref, win_ref, vn_ref, ws_ref, bs_ref, wout_ref, o_ref, zs, *, tm):
    x = x_ref[...]
    xn = _rms(x, nrm_ref[...]).astype(BF16)
    v = _gelu(_dot(xn, win_ref[:, D_MODEL:2 * D_MODEL]))
    u = _gelu(_dot(xn, win_ref[:, 0:D_MODEL]))
    vn = _rms(v, vn_ref[...]).astype(BF16)
    nch = tm // CHUNK
    for g in range(G_GROUPS):
        cols = slice(g * LANES, (g + 1) * LANES)
        rhs = jnp.concatenate([vn[c * CHUNK:(c + 1) * CHUNK, cols] for c in range(nch)], axis=1)
        sv = _dot(ws_ref[g], rhs) + bs_ref[:, g:g + 1]
        for c in range(nch):
            zs[c * CHUNK:(c + 1) * CHUNK, cols] = (
                u[c * CHUNK:(c + 1) * CHUNK, cols] * sv[:, c * LANES:(c + 1) * LANES]).astype(BF16)
    o_ref[...] = x + _dot(zs[...], wout_ref[...])


def _mix_c(x, nrm, win, vn, ws, bs, wout, *, tm):
    n = x.shape[0]
    row = lambda i: (i, 0)
    return pl.pallas_call(
        functools.partial(_mix_c_kernel, tm=tm),
        grid=(n // tm,),
        in_specs=[
            pl.BlockSpec((tm, D_MODEL), row),
            _const_spec(nrm.shape), _const_spec(win.shape), _const_spec(vn.shape), _const_spec(ws.shape),
            _const_spec(bs.shape), _const_spec(wout.shape),
        ],
        out_specs=pl.BlockSpec((tm, D_MODEL), row),
        out_shape=jax.ShapeDtypeStruct((n, D_MODEL), F32),
        scratch_shapes=[pltpu.VMEM((tm, D_MODEL), BF16)],
        compiler_params=_params("parallel"),
        name="mix_c",
    )(x, nrm, win, vn, ws, bs, wout)


def _rope_tables(seq):
    pos = jnp.arange(seq, dtype=F32)
    inv = 1.0 / (ROPE_THETA ** (jnp.arange(0, QK_ROPE, 2, dtype=F32) / QK_ROPE))
    ang = pos[:, None] * inv[None, :]
    cos, sin = jnp.cos(ang), jnp.sin(ang)
    one = jnp.ones((seq, QK_NOPE), F32)
    zero_n = jnp.zeros((seq, QK_NOPE), F32)
    pad = jnp.zeros((seq, LANES - QK_NOPE - QK_ROPE), F32)
    return (jnp.concatenate([one, cos, cos, pad], axis=1), jnp.concatenate([zero_n, sin, sin, pad], axis=1))


def _head_block(nope, x1, x2):
    pad = jnp.zeros((nope.shape[0], LANES - QK_NOPE - QK_ROPE), nope.dtype)
    return jnp.concatenate([nope, x1, x2, pad], axis=1)


def _prep_even(i, ab_norm, ab_w_in, mla_q_norm, mla_w_uq, mla_kv_norm, mla_w_ukv, mlstm_conv_w, mlstm_conv_b,
               mlstm_gate_bias, mlstm_out_norm, ab_w_out):
    w_in = ab_w_in[i]
    half = QK_ROPE // 2
    zq = jnp.zeros((D_MODEL, QK_NOPE), F32)
    kr1 = w_in[:, 384:384 + half]
    kr2 = w_in[:, 384 + half:416]
    gate_order = jnp.array([0, 1, 2, 3, 8, 9, 10, 11, 4, 5, 6, 7, 12, 13, 14, 15])
    gates = jnp.pad(w_in[:, 2464:2480][:, gate_order], ((0, 0), (0, LANES - 2 * N_STREAMS)))
    w_big = jnp.concatenate([
        w_in[:, 0:384],
        _head_block(zq, kr1, kr2),
        _head_block(zq, -kr2, kr1),
        gates,
        w_in[:, 416:2464],
    ], axis=1).astype(BF16)
    w_uq = mla_w_uq[i].reshape(Q_LORA, MLA_HEADS, QK_NOPE + QK_ROPE)
    zn = jnp.zeros((Q_LORA, QK_NOPE), F32)
    wqa = jnp.concatenate([_head_block(w_uq[:, h, :QK_NOPE], w_uq[:, h, QK_NOPE:QK_NOPE + half],
                                       w_uq[:, h, QK_NOPE + half:]) for h in range(MLA_HEADS)], axis=1)
    wqb = jnp.concatenate([_head_block(zn, -w_uq[:, h, QK_NOPE + half:], w_uq[:, h, QK_NOPE:QK_NOPE + half])
                           for h in range(MLA_HEADS)], axis=1)
    w_ukv = mla_w_ukv[i].reshape(KV_LORA, MLA_HEADS, QK_NOPE + V_HEAD)
    wka = jnp.pad(w_ukv[:, :, :QK_NOPE], ((0, 0), (0, 0), (0, LANES - QK_NOPE))).reshape(KV_LORA, MLA_HEADS * LANES)
    wv = w_ukv[:, :, QK_NOPE:].reshape(KV_LORA, MLA_HEADS * V_HEAD)
    w_out = ab_w_out[i].astype(BF16)
    return dict(
        nrm=ab_norm[i][None, :], w_big=w_big, qn=mla_q_norm[i][None, :], wqa=wqa.astype(BF16),
        wqb=wqb.astype(BF16), kvn=mla_kv_norm[i][None, :], wka=wka.astype(BF16), wv=wv.astype(BF16),
        gb=jnp.broadcast_to(mlstm_gate_bias[i][gate_order][:, None], (2 * N_STREAMS, LANES)),
        conv_w=mlstm_conv_w[i], conv_b=mlstm_conv_b[i][None, :], on=mlstm_out_norm[i][None, :],
        wa=w_out[:MLA_HEADS * V_HEAD], wm=w_out[MLA_HEADS * V_HEAD:])


def _pick(pref, seq):
    return min(pref, seq)


def _forward(x3, rope, even, odd, ffn, final_norm):
    batch, seq, _ = x3.shape
    x = x3.reshape(batch * seq, D_MODEL)
    tm = _pick(512, seq)
    cos_t, sin_t = rope
    e = even
    q, k, v, mq, mk, mv, mo, g = _ab_in(x, cos_t, sin_t, e, seq=seq, tm=_pick(1024, seq))
    tq = _pick(ATTN_TQ, seq)
    score_bytes = 2 * tq * seq * 4
    pair_bytes = 2 * 2 * seq * 2 * LANES * 2
    pairs = min(MLA_HEADS // 2, max(1, (ATTN_VMEM - score_bytes) // pair_bytes))
    while (MLA_HEADS // 2) % pairs:
        pairs -= 1
    ya = _attn(q, k, v, batch=batch, seq=seq, tq=tq, pairs=pairs)
    c_rows, dec, cols = _gates(g, batch=batch, seq=seq)
    hf, hb = _mlstm(mq, mk, mv, c_rows, dec, cols, batch=batch, seq=seq)
    f = ffn[0]
    x = _ab_out_ffn(x, ya, hf, hb, mo, e["on"], e["wa"], e["wm"], f["nrm"], f["wup"], f["cw"], f["cb"], f["wdn"],
                    final_norm, seq=seq, tm=tm)
    o = odd
    x = _mix_c(x, o["nrm"], o["win"], o["vn"], o["ws"], o["bs"], o["wout"], tm=_pick(1024, seq))
    f = ffn[1]
    x = _ffn(x, f["nrm"], f["wup"], f["cw"], f["cb"], f["wdn"], final_norm, seq=seq, tm=_pick(1024, seq), final=True)
    return x.reshape(batch, seq, D_MODEL)


def _prep_rest(c_norm, c_w_in, c_v_norm, c_w_spatial, c_b_spatial, c_w_out, ffn_norm, ffn_w_up, ffn_conv_w,
               ffn_conv_b, ffn_w_down):
    odd = dict(nrm=c_norm[0][None, :], win=c_w_in[0].astype(BF16), vn=c_v_norm[0][None, :],
               ws=c_w_spatial[0].astype(BF16), bs=c_b_spatial[0].T, wout=c_w_out[0].astype(BF16))
    ffn = [dict(nrm=ffn_norm[l][None, :], wup=ffn_w_up[l].astype(BF16), cw=ffn_conv_w[l], cb=ffn_conv_b[l][None, :],
                wdn=ffn_w_down[l].astype(BF16)) for l in range(2)]
    return odd, ffn


def kernel(x_prompt, x_sample, ab_norm, ab_w_in, mla_q_norm, mla_w_uq, mla_kv_norm, mla_w_ukv, mlstm_conv_w,
           mlstm_conv_b, mlstm_gate_bias, mlstm_out_norm, ab_w_out, c_norm, c_w_in, c_v_norm, c_w_spatial,
           c_b_spatial, c_w_out, ffn_norm, ffn_w_up, ffn_conv_w, ffn_conv_b, ffn_w_down, final_norm):
    even = _prep_even(0, ab_norm, ab_w_in, mla_q_norm, mla_w_uq, mla_kv_norm, mla_w_ukv, mlstm_conv_w, mlstm_conv_b,
                      mlstm_gate_bias, mlstm_out_norm, ab_w_out)
    odd, ffn = _prep_rest(c_norm, c_w_in, c_v_norm, c_w_spatial, c_b_spatial, c_w_out, ffn_norm, ffn_w_up,
                          ffn_conv_w, ffn_conv_b, ffn_w_down)
    fin = final_norm[None, :]
    rope = _rope_tables(max(x_prompt.shape[1], x_sample.shape[1]))
    return (_forward(x_prompt, rope, even, odd, ffn, fin), _forward(x_sample, rope, even, odd, ffn, fin))
```

```python
import functools

import jax
import jax.numpy as jnp
from jax import lax
from jax.experimental import pallas as pl
from jax.experimental.pallas import tpu as pltpu

F32 = jnp.float32
BF16 = jnp.bfloat16

EPS = 1e-6
D_MODEL = 1024
MLA_HEADS = 8
Q_LORA = 256
KV_LORA = 128
QK_NOPE = 64
QK_ROPE = 32
V_HEAD = 64
ROPE_THETA = 10000.0
M_HEADS = 4
M_HEAD_DIM = 128
M_WIDTH = M_HEADS * M_HEAD_DIM
N_STREAMS = 2 * M_HEADS
CHUNK = 128
G_GROUPS = 8
D_FF = 2816
LANES = 128
SUBLANES = 8
HALO = SUBLANES
BLOCK_CHUNKS = 8
PAIR = 2 * SUBLANES
MXU_COLS = 256
ROW_PITCH = 2
FF_COLS = MXU_COLS
KEY_TILE = MXU_COLS
ATTN_TQ = 512
ATTN_VMEM = 48 * 1024 * 1024
LOG2_E = 1.4426950408889634
VMEM_LIMIT = 56 * 1024 * 1024

_C_Q, _C_KV, _C_KRA, _C_KRB, _C_G, _C_MQK, _C_MV, _C_MO, _C_END = (
    0, 256, 384, 512, 640, 768, 1792, 2304, 2816)


def _rms(x, g):
    return x * lax.rsqrt(jnp.mean(x * x, axis=-1, keepdims=True) + EPS) * g


def _const_spec(shape):
    nd = len(shape)
    return pl.BlockSpec(shape, lambda *_: (0,) * nd, pipeline_mode=pl.Buffered(1))


def _params(*sem):
    return pltpu.CompilerParams(dimension_semantics=sem, vmem_limit_bytes=VMEM_LIMIT)


def _dot(a, b):
    return jnp.dot(a, b, preferred_element_type=F32)


def _dot_nt(a, b):
    return lax.dot_general(a, b, (((1,), (1,)), ((), ())), preferred_element_type=F32)


def _halo_specs(tile_rows, n_rows, width, halo=HALO):
    tb = tile_rows // halo
    last = n_rows // halo - 1
    return (pl.BlockSpec((halo, width), lambda i: (jnp.maximum(i * tb - 1, 0), 0)),
            pl.BlockSpec((halo, width), lambda i: (jnp.minimum((i + 1) * tb, last), 0)))


def _conv3(y, u, rows, w_ref, b_ref, c0):
    tall = rows + 2 * HALO
    outs = []
    for s in range(y.shape[1] // LANES):
        lanes = slice(s * LANES, (s + 1) * LANES)
        cols = slice(c0 + s * LANES, c0 + (s + 1) * LANES)
        u[s, pl.ds(0, tall, stride=ROW_PITCH), :] = y[:, lanes]
        prv = u[s, pl.ds(ROW_PITCH * (HALO - 1), rows, stride=ROW_PITCH), :]
        nxt = u[s, pl.ds(ROW_PITCH * (HALO + 1), rows, stride=ROW_PITCH), :]
        outs.append(prv * w_ref[0:1, cols] + y[HALO:HALO + rows, lanes] * w_ref[1:2, cols]
                    + nxt * w_ref[2:3, cols] + b_ref[:, cols])
    return jnp.concatenate(outs, axis=1)


def _ab_in_kernel(x_ref, xp_ref, xn_ref, cos_ref, sin_ref, nrm_ref, w_ref, qn_ref, wqa_ref, wqb_ref, kvn_ref,
                  wka_ref, wv_ref, gb_ref, cw_ref, cb_ref,
                  q_ref, k_ref, v_ref, mq_ref, mk_ref, mv_ref, mo_ref, g_ref, us, *, tm, tps):
    jj = pl.program_id(0) % tps
    xa = jnp.concatenate([
        jnp.where(jj == 0, 0.0, xp_ref[...]),
        x_ref[...],
        jnp.where(jj == tps - 1, 0.0, xn_ref[...]),
    ], axis=0)
    xe = _rms(xa, nrm_ref[...]).astype(BF16)
    xn = xe[HALO:HALO + tm]

    def proj(a, b):
        return _dot(xn, w_ref[:, a:b])

    cq = proj(_C_Q, _C_KV)
    ckv_kra = proj(_C_KV, _C_KRB)
    krb_g = proj(_C_KRB, _C_MQK)

    nq_blocks = M_WIDTH // MXU_COLS
    for cb in range(2 * nq_blocks):
        cols = slice(cb * MXU_COLS, (cb + 1) * MXU_COLS)
        y = _conv3(_dot(xe, w_ref[:, _C_MQK + cb * MXU_COLS:_C_MQK + (cb + 1) * MXU_COLS]), us.at[cb % 2], tm,
                   cw_ref, cb_ref, cb * MXU_COLS)
        y = y * jax.nn.sigmoid(y)
        if cb < nq_blocks:
            mq_ref[:, cols] = y.astype(BF16)
        else:
            mk_ref[:, (cb - nq_blocks) * MXU_COLS:(cb - nq_blocks + 1) * MXU_COLS] = (
                y * (M_HEAD_DIM ** -0.5)).astype(BF16)
    mv_ref[...] = proj(_C_MV, _C_MO).astype(BF16)
    mo_ref[...] = proj(_C_MO, _C_END)
    gt = krb_g[:, LANES:2 * LANES].T
    for c in range(tm // CHUNK):
        g_ref[c] = gt[0:2 * N_STREAMS, c * CHUNK:(c + 1) * CHUNK] + gb_ref[:, 0:1]

    cos = cos_ref[...]
    sin = sin_ref[...]
    scale = (QK_NOPE + QK_ROPE) ** -0.5 * LOG2_E
    cqn = _rms(cq, qn_ref[...]).astype(BF16)
    qa = _dot(cqn, wqa_ref[...])
    qb = _dot(cqn, wqb_ref[...])
    half = LANES // 2
    cos_q = (cos * scale, pltpu.roll(cos, half, axis=1) * scale)
    sin_q = (sin * scale, pltpu.roll(sin, half, axis=1) * scale)
    for h in range(MLA_HEADS):
        blk = slice(h * LANES, (h + 1) * LANES)
        q_ref[:, blk] = (qa[:, blk] * cos_q[h % 2] + qb[:, blk] * sin_q[h % 2]).astype(BF16)

    ckvn = _rms(ckv_kra[:, 0:LANES], kvn_ref[...]).astype(BF16)
    kd = _dot(ckvn, wka_ref[...])
    kr = ckv_kra[:, LANES:2 * LANES] * cos + krb_g[:, 0:LANES] * sin
    kr_odd = pltpu.roll(kr, half, axis=1)
    lane = lax.broadcasted_iota(jnp.int32, (tm, LANES), 1)
    for p in range(MLA_HEADS // 2):
        pair = kd[:, p * LANES:(p + 1) * LANES]
        k_ref[:, 2 * p * LANES:(2 * p + 1) * LANES] = jnp.where(lane < half, pair, kr).astype(BF16)
        k_ref[:, (2 * p + 1) * LANES:(2 * p + 2) * LANES] = jnp.where(lane >= half, pair, kr_odd).astype(BF16)
    vd = _dot(ckvn, wv_ref[...])
    for p in range(MLA_HEADS // 2):
        pair = vd[:, p * LANES:(p + 1) * LANES]
        v_ref[:, 2 * p * LANES:(2 * p + 1) * LANES] = jnp.where(
            lane < V_HEAD, pair, (lane == V_HEAD).astype(F32)).astype(BF16)
        v_ref[:, (2 * p + 1) * LANES:(2 * p + 2) * LANES] = jnp.where(
            lane >= V_HEAD, pair, (lane == 0).astype(F32)).astype(BF16)


def _ab_in(x, cos_t, sin_t, e, *, seq, tm):
    n = x.shape[0]
    nt = n // tm
    tps = seq // tm
    row = lambda i: (i, 0)
    tab = lambda i: (i % tps, 0)
    consts = [e[k] for k in ("nrm", "w_big", "qn", "wqa", "wqb", "kvn", "wka", "wv", "gb", "conv_w",
                             "conv_b")]
    out_shape = (
        jax.ShapeDtypeStruct((n, MLA_HEADS * LANES), BF16),
        jax.ShapeDtypeStruct((n, MLA_HEADS * LANES), BF16),
        jax.ShapeDtypeStruct((n, MLA_HEADS * LANES), BF16),
        jax.ShapeDtypeStruct((n, M_WIDTH), BF16),
        jax.ShapeDtypeStruct((n, M_WIDTH), BF16),
        jax.ShapeDtypeStruct((n, M_WIDTH), BF16),
        jax.ShapeDtypeStruct((n, M_WIDTH), F32),
        jax.ShapeDtypeStruct((n // CHUNK, 2 * N_STREAMS, CHUNK), F32),
    )
    wide = pl.BlockSpec((tm, MLA_HEADS * LANES), row)
    narrow = pl.BlockSpec((tm, M_WIDTH), row)
    return pl.pallas_call(
        functools.partial(_ab_in_kernel, tm=tm, tps=tps),
        grid=(nt,),
        in_specs=[pl.BlockSpec((tm, D_MODEL), row), *_halo_specs(tm, n, D_MODEL),
                  pl.BlockSpec((tm, LANES), tab), pl.BlockSpec((tm, LANES), tab)]
                 + [_const_spec(c.shape) for c in consts],
        out_specs=(wide, wide, wide, narrow, narrow, narrow, narrow,
                   pl.BlockSpec((tm // CHUNK, 2 * N_STREAMS, CHUNK), lambda i: (i, 0, 0))),
        out_shape=out_shape,
        scratch_shapes=[pltpu.VMEM((2, MXU_COLS // LANES, ROW_PITCH * (tm + 2 * HALO), LANES), F32)],
        compiler_params=_params("parallel"),
        name="ab_in",
    )(x, x, x, cos_t, sin_t, *consts)


def _attn_kernel(q_ref, k_ref, v_ref, o_ref, s_scr, *, tq, pairs):
    nt = k_ref.shape[0] // KEY_TILE
    lane = lax.broadcasted_iota(jnp.int32, (tq, LANES), 1)
    for pr in range(pairs):
        mbs = []
        for h in range(2):
            hc = slice((2 * pr + h) * LANES, (2 * pr + h + 1) * LANES)
            q = q_ref[:, hc]
            mx = jnp.full((tq, LANES), -jnp.inf, F32)
            for t in range(nt):
                keys = slice(t * KEY_TILE, (t + 1) * KEY_TILE)
                s = _dot_nt(q, k_ref[keys, hc])
                s_scr[h, :, keys] = s
                for j in range(KEY_TILE // LANES):
                    mx = jnp.maximum(mx, s[:, j * LANES:(j + 1) * LANES])
            mbs.append(jnp.broadcast_to(jnp.max(mx, axis=-1, keepdims=True), (tq, LANES)))
        outs = []
        for h in range(2):
            hc = slice((2 * pr + h) * LANES, (2 * pr + h + 1) * LANES)
            mb = jnp.concatenate([mbs[h]] * (KEY_TILE // LANES), axis=1)
            acc = jnp.zeros((tq, LANES), F32)
            for t in range(nt):
                keys = slice(t * KEY_TILE, (t + 1) * KEY_TILE)
                p = jnp.exp2(s_scr[h, :, keys] - mb).astype(BF16)
                acc = acc + _dot(p, v_ref[keys, hc])
            ones_lane = V_HEAD if h == 0 else 0
            outs.append(acc / acc[:, ones_lane:ones_lane + 1])
        o_ref[:, pr * LANES:(pr + 1) * LANES] = jnp.where(lane < V_HEAD, outs[0], outs[1]).astype(BF16)


def _attn(q, k, v, *, batch, seq, tq, pairs):
    n = q.shape[0]
    nq = seq // tq
    steps = MLA_HEADS // 2 // pairs
    return pl.pallas_call(
        functools.partial(_attn_kernel, tq=tq, pairs=pairs),
        grid=(batch, steps, nq),
        in_specs=[
            pl.BlockSpec((tq, 2 * pairs * LANES), lambda b, p, i: (b * nq + i, p)),
            pl.BlockSpec((seq, 2 * pairs * LANES), lambda b, p, i: (b, p)),
            pl.BlockSpec((seq, 2 * pairs * LANES), lambda b, p, i: (b, p)),
        ],
        out_specs=pl.BlockSpec((tq, pairs * LANES), lambda b, p, i: (b * nq + i, p)),
        out_shape=jax.ShapeDtypeStruct((n, MLA_HEADS * V_HEAD), BF16),
        scratch_shapes=[pltpu.VMEM((2, tq, seq), F32)],
        compiler_params=_params("parallel", "parallel", "arbitrary"),
        name="attn",
    )(q, k, v)


def _log_sigmoid(x):
    return jnp.minimum(x, 0.0) - jnp.log1p(jnp.exp(-jnp.abs(x)))


def _gates_kernel(g_ref, c_ref, dec_ref, cols_ref, tot_s, mloc_s, mpf_s, mpb_s, *, nc):
    rows = nc * N_STREAMS
    shape = (rows, CHUNK)
    three = (nc, N_STREAMS, CHUNK)
    li = g_ref[:, 0:N_STREAMS, :].reshape(shape)
    lf = _log_sigmoid(g_ref[:, N_STREAMS:2 * N_STREAMS, :].reshape(shape))
    fwd = (lax.broadcasted_iota(jnp.int32, shape, 0) & M_HEADS) == 0
    lane = lax.broadcasted_iota(jnp.int32, shape, 1)
    shifts = (1, 2, 4, 8, 16, 32, 64)

    pre = lf
    for sh in shifts:
        pre = pre + jnp.where(lane >= sh, pltpu.roll(pre, sh, axis=1), 0.0)
    tot = jnp.broadcast_to(pre[:, CHUNK - 1:CHUNK], shape)
    b = jnp.where(fwd, pre, tot - pre + lf)
    c = li - b
    run_f = c
    run_b = c
    for sh in shifts:
        run_f = jnp.maximum(run_f, jnp.where(lane >= sh, pltpu.roll(run_f, sh, axis=1), -jnp.inf))
        run_b = jnp.maximum(run_b, jnp.where(lane < CHUNK - sh, pltpu.roll(run_b, CHUNK - sh, axis=1), -jnp.inf))
    run = jnp.where(fwd, run_f, run_b)
    m_loc = tot + jnp.broadcast_to(jnp.max(c, axis=1, keepdims=True), shape)

    tot_s[...] = tot.reshape(three)
    mloc_s[...] = m_loc.reshape(three)
    fwd8 = lax.broadcasted_iota(jnp.int32, (N_STREAMS, CHUNK), 0) < M_HEADS

    def step(i, m):
        jb = nc - 1 - i
        mpf_s[i] = m
        mpb_s[jb] = m
        return jnp.maximum(jnp.where(fwd8, tot_s[i], tot_s[jb]) + m, jnp.where(fwd8, mloc_s[i], mloc_s[jb]))

    lax.fori_loop(0, nc, step, jnp.zeros((N_STREAMS, CHUNK), F32))
    m_prev = jnp.where(fwd, mpf_s[...].reshape(shape), mpb_s[...].reshape(shape))

    m_run = jnp.maximum(m_prev, run)
    floor = jnp.exp(-(b + m_run))
    w = jnp.exp(tot + c - m_loc)
    m_new = jnp.maximum(tot + m_prev, m_loc)
    c_ref[:, 0:N_STREAMS, :] = c.reshape(three)
    c_ref[:, N_STREAMS:2 * N_STREAMS, :] = w.reshape(three)
    c_ref[:, 2 * N_STREAMS:3 * N_STREAMS, :] = m_prev.reshape(three)
    dec_ref[:, 0:N_STREAMS, :] = jnp.exp(tot + m_prev - m_new).reshape(three)
    dec_ref[:, N_STREAMS:2 * N_STREAMS, :] = jnp.exp(m_loc - m_new).reshape(three)
    quantities = [a.reshape(three) for a in (m_run, floor)]
    pad_rows = CHUNK - BLOCK_CHUNKS * 2 * N_STREAMS
    pad = [jnp.zeros((pad_rows, CHUNK), F32)] if pad_rows else []
    for blk in range(nc // BLOCK_CHUNKS):
        tile = jnp.concatenate(
            [a[blk * BLOCK_CHUNKS + cq] for cq in range(BLOCK_CHUNKS) for a in quantities] + pad, axis=0)
        cols_ref[blk] = tile.T


def _gates(g, *, batch, seq):
    nc = seq // CHUNK
    nchunks = g.shape[0]
    three = (nc, N_STREAMS, CHUNK)
    return pl.pallas_call(
        functools.partial(_gates_kernel, nc=nc),
        grid=(batch,),
        in_specs=[pl.BlockSpec((nc, 2 * N_STREAMS, CHUNK), lambda b: (b, 0, 0))],
        out_specs=(
            pl.BlockSpec((nc, 3 * N_STREAMS, CHUNK), lambda b: (b, 0, 0)),
            pl.BlockSpec((nc, 2 * N_STREAMS, CHUNK), lambda b: (b, 0, 0)),
            pl.BlockSpec((nc // BLOCK_CHUNKS, CHUNK, LANES), lambda b: (b, 0, 0)),
        ),
        out_shape=(
            jax.ShapeDtypeStruct((nchunks, 3 * N_STREAMS, CHUNK), F32),
            jax.ShapeDtypeStruct((nchunks, 2 * N_STREAMS, CHUNK), F32),
            jax.ShapeDtypeStruct((nchunks // BLOCK_CHUNKS, CHUNK, LANES), F32),
        ),
        scratch_shapes=[pltpu.VMEM(three, F32)] * 4,
        compiler_params=_params("parallel"),
        name="gates",
    )(g)


def _mlstm_kernel(*refs):
    ins, (hf_ref, hb_ref, st, bc) = (refs[0:6], refs[6:12]), refs[12:]

    @pl.when(pl.program_id(1) == 0)
    def _():
        st[...] = jnp.zeros_like(st)

    row = lax.broadcasted_iota(jnp.int32, (CHUNK, CHUNK), 0)
    col = lax.broadcasted_iota(jnp.int32, (CHUNK, CHUNK), 1)
    masks = (col <= row, col >= row)
    ones = (col == 0).astype(BF16)
    orders = (range(BLOCK_CHUNKS), range(BLOCK_CHUNKS - 1, -1, -1))
    heads = range(M_HEADS)
    hcs = [slice(hh * LANES, (hh + 1) * LANES) for hh in heads]
    for d in range(2):
        cols_t = ins[d][5][0]
        for cq in range(BLOCK_CHUNKS):
            for hh in heads:
                base = cq * 2 * N_STREAMS + d * M_HEADS + hh
                for qi in range(2):
                    lane_i = base + qi * N_STREAMS
                    bc[d, (qi * BLOCK_CHUNKS + cq) * M_HEADS + hh] = jnp.broadcast_to(
                        cols_t[:, lane_i:lane_i + 1], (CHUNK, LANES))
    qs, v1s, qk, c_loc = {}, {}, {}, {}
    for d in range(2):
        q_ref, k_ref = ins[d][0], ins[d][1]
        for cq in orders[d]:
            rows = slice(cq * CHUNK, (cq + 1) * CHUNK)
            for hh in heads:
                qs[d, cq, hh] = q_ref[rows, hcs[hh]]
                qk[d, cq, hh] = _dot_nt(qs[d, cq, hh], k_ref[rows, hcs[hh]])
    for d in range(2):
        k_ref, v_ref, c_ref = ins[d][1], ins[d][2], ins[d][3]
        for cq in orders[d]:
            rows = slice(cq * CHUNK, (cq + 1) * CHUNK)
            for hh in heads:
                s = d * M_HEADS + hh
                v1s[d, cq, hh] = jnp.concatenate([v_ref[rows, hcs[hh]], ones], axis=1)
                kw_t = (k_ref[rows, hcs[hh]].astype(F32).T
                        * c_ref[cq, N_STREAMS + s:N_STREAMS + s + 1, :]).astype(BF16)
                c_loc[d, cq, hh] = _dot(kw_t, v1s[d, cq, hh])
    for step in range(BLOCK_CHUNKS):
        for d in range(2):
            _mlstm_chunk(d, orders[d][step], ins[d][3], ins[d][4], (hf_ref, hb_ref)[d], st, bc, masks[d], hcs,
                         qs, v1s, qk, c_loc)


def _mlstm_chunk(d, cq, c_ref, dec_ref, h_ref, st, bc, mask, hcs, qs, v1s, qk, c_loc):
    rows = slice(cq * CHUNK, (cq + 1) * CHUNK)
    for hh in range(M_HEADS):
        s = d * M_HEADS + hh
        key = (d, cq, hh)
        state = st[d, hh]
        m_run = bc[d, cq * M_HEADS + hh]
        floor = bc[d, (BLOCK_CHUNKS + cq) * M_HEADS + hh]
        c_ext = jnp.concatenate([c_ref[cq, s:s + 1, :], c_ref[cq, 2 * N_STREAMS + s:2 * N_STREAMS + s + 1, :]],
                                axis=1)
        d_ext = jnp.exp(c_ext - jnp.concatenate([m_run, m_run], axis=1))
        p_ext = jnp.concatenate([qk[key] * jnp.where(mask, d_ext[:, 0:LANES], 0.0),
                                 qs[key].astype(F32) * d_ext[:, LANES:2 * LANES]], axis=1).astype(BF16)
        nd = _dot(p_ext, jnp.concatenate([v1s[key], state.astype(BF16)], axis=0))
        den = nd[:, LANES:LANES + 1]
        h_ref[rows, hcs[hh]] = nd[:, 0:LANES] / jnp.maximum(jnp.abs(den), floor)
        st[d, hh] = (dec_ref[cq, s:s + 1, 0:1] * state
                     + dec_ref[cq, N_STREAMS + s:N_STREAMS + s + 1, 0:1] * c_loc[key])


def _mlstm(mq, mk, mv, c_rows, dec, cols, *, batch, seq):
    n = mq.shape[0]
    ts = BLOCK_CHUNKS * CHUNK
    nblk = seq // ts

    def specs(direction):
        def blk(b, i):
            return b * nblk + (i if direction == 0 else nblk - 1 - i)

        tok = pl.BlockSpec((ts, M_WIDTH), lambda b, i: (blk(b, i), 0))
        return tok, [tok, tok, tok,
                     pl.BlockSpec((BLOCK_CHUNKS, 3 * N_STREAMS, CHUNK), lambda b, i: (blk(b, i), 0, 0)),
                     pl.BlockSpec((BLOCK_CHUNKS, 2 * N_STREAMS, CHUNK), lambda b, i: (blk(b, i), 0, 0)),
                     pl.BlockSpec((1, CHUNK, LANES), lambda b, i: (blk(b, i), 0, 0))]

    (tok_f, in_f), (tok_b, in_b) = specs(0), specs(1)
    h_shape = jax.ShapeDtypeStruct((n, M_WIDTH), F32)
    args = (mq, mk, mv, c_rows, dec, cols)
    return pl.pallas_call(
        _mlstm_kernel,
        grid=(batch, nblk),
        in_specs=in_f + in_b,
        out_specs=(tok_f, tok_b),
        out_shape=(h_shape, h_shape),
        scratch_shapes=[pltpu.VMEM((2, M_HEADS, M_HEAD_DIM, 2 * LANES), F32),
                        pltpu.VMEM((2, 2 * BLOCK_CHUNKS * M_HEADS, CHUNK, LANES), F32)],
        compiler_params=_params("parallel", "arbitrary"),
        name="mlstm",
    )(*args, *args)


def _ffn_body(xa, nrm_ref, wup_ref, cw_ref, cb_ref, wdn_ref, fin_ref, us, act, *, tm, final):
    xn = _rms(xa, nrm_ref[...]).astype(BF16)
    for c in range(D_FF // FF_COLS):
        g0 = c * FF_COLS
        v0 = D_FF + c * FF_COLS
        gate = _conv3(_dot(xn, wup_ref[:, g0:g0 + FF_COLS]), us.at[0], tm, cw_ref, cb_ref, g0)
        val = _conv3(_dot(xn, wup_ref[:, v0:v0 + FF_COLS]), us.at[1], tm, cw_ref, cb_ref, v0)
        act[:, g0:g0 + FF_COLS] = (gate * jax.nn.sigmoid(gate) * val).astype(BF16)
    y = xa[HALO:HALO + tm] + _dot(act[...], wdn_ref[...])
    if final:
        y = _rms(y, fin_ref[...])
    return y


def _ffn_kernel(x_ref, xp_ref, xn_ref, nrm_ref, wup_ref, cw_ref, cb_ref, wdn_ref, fin_ref, o_ref, us, act,
                *, tm, tps, final):
    jj = pl.program_id(0) % tps
    xa = jnp.concatenate([
        jnp.where(jj == 0, 0.0, xp_ref[...]),
        x_ref[...],
        jnp.where(jj == tps - 1, 0.0, xn_ref[...]),
    ], axis=0)
    o_ref[...] = _ffn_body(xa, nrm_ref, wup_ref, cw_ref, cb_ref, wdn_ref, fin_ref, us, act, tm=tm, final=final)


def _ffn_scratch(tm):
    return [pltpu.VMEM((2, FF_COLS // LANES, ROW_PITCH * (tm + 2 * HALO), LANES), F32),
            pltpu.VMEM((tm, D_FF), BF16)]


def _ab_out_ffn_kernel(x_ref, xp_ref, xn_ref, ya_ref, yap_ref, yan_ref, hf_ref, hfp_ref, hfn_ref, hb_ref, hbp_ref,
                       hbn_ref, mo_ref, mop_ref, mon_ref, on_ref, wa_ref, wm_ref, nrm_ref, wup_ref, cw_ref, cb_ref,
                       wdn_ref, fin_ref, o_ref, us, act, *, tm, tps):
    jj = pl.program_id(0) % tps

    def rows(main, prev, nxt):
        return jnp.concatenate([main[...], prev[...], nxt[...]], axis=0)

    hf, hb, mo = rows(hf_ref, hfp_ref, hfn_ref), rows(hb_ref, hbp_ref, hbn_ref), rows(mo_ref, mop_ref, mon_ref)
    parts = []
    for hh in range(M_HEADS):
        hc = slice(hh * LANES, (hh + 1) * LANES)
        t = hf[:, hc] + hb[:, hc]
        t = t * lax.rsqrt(jnp.mean(t * t, axis=-1, keepdims=True) + EPS) * on_ref[:, hc]
        parts.append((jax.nn.sigmoid(mo[:, hc]) * t).astype(BF16))
    ym = jnp.concatenate(parts, axis=1)
    x1 = (rows(x_ref, xp_ref, xn_ref) + _dot(rows(ya_ref, yap_ref, yan_ref), wa_ref[...]) + _dot(ym, wm_ref[...]))
    xa = jnp.concatenate([
        jnp.where(jj == 0, 0.0, x1[tm + PAIR - HALO:tm + PAIR]),
        x1[0:tm],
        jnp.where(jj == tps - 1, 0.0, x1[tm + PAIR:tm + PAIR + HALO]),
    ], axis=0)
    o_ref[...] = _ffn_body(xa, nrm_ref, wup_ref, cw_ref, cb_ref, wdn_ref, fin_ref, us, act, tm=tm, final=False)


def _ab_out_ffn(x, ya, hf, hb, mo, on, wa, wm, nrm, wup, cw, cb, wdn, fin, *, seq, tm):
    n = x.shape[0]
    row = lambda i: (i, 0)

    def with_halo(width):
        return [pl.BlockSpec((tm, width), row), *_halo_specs(tm, n, width, PAIR)]

    consts = [on, wa, wm, nrm, wup, cw, cb, wdn, fin]
    return pl.pallas_call(
        functools.partial(_ab_out_ffn_kernel, tm=tm, tps=seq // tm),
        grid=(n // tm,),
        in_specs=with_halo(D_MODEL) + 4 * with_halo(M_WIDTH) + [_const_spec(c.shape) for c in consts],
        out_specs=pl.BlockSpec((tm, D_MODEL), row),
        out_shape=jax.ShapeDtypeStruct((n, D_MODEL), F32),
        scratch_shapes=_ffn_scratch(tm),
        compiler_params=_params("parallel"),
        name="ab_out_ffn",
    )(x, x, x, ya, ya, ya, hf, hf, hf, hb, hb, hb, mo, mo, mo, *consts)


def _ffn(x, nrm, wup, cw, cb, wdn, fin, *, seq, tm, final):
    n = x.shape[0]
    row = lambda i: (i, 0)
    return pl.pallas_call(
        functools.partial(_ffn_kernel, tm=tm, tps=seq // tm, final=final),
        grid=(n // tm,),
        in_specs=[
            pl.BlockSpec((tm, D_MODEL), row), *_halo_specs(tm, n, D_MODEL),
            _const_spec(nrm.shape), _const_spec(wup.shape), _const_spec(cw.shape), _const_spec(cb.shape),
            _const_spec(wdn.shape), _const_spec(fin.shape),
        ],
        out_specs=pl.BlockSpec((tm, D_MODEL), row),
        out_shape=jax.ShapeDtypeStruct((n, D_MODEL), F32),
        scratch_shapes=_ffn_scratch(tm),
        compiler_params=_params("parallel"),
        name="ffn_final" if final else "ffn",
    )(x, x, x, nrm, wup, cw, cb, wdn, fin)


def _gelu(x):
    return 0.5 * x * (1.0 + lax.erf(x * (2.0 ** -0.5)))


def _mix_c_kernel(x_ref, nrm_ref, win_ref, vn_ref, ws_ref, bs_ref, wout_ref, o_ref, zs, *, tm):
    x = x_ref[...]
    xn = _rms(x, nrm_ref[...]).astype(BF16)
    v = _gelu(_dot(xn, win_ref[:, D_MODEL:2 * D_MODEL]))
    u = _gelu(_dot(xn, win_ref[:, 0:D_MODEL]))
    vn = _rms(v, vn_ref[...]).astype(BF16)
    nch = tm // CHUNK
    for g in range(G_GROUPS):
        cols = slice(g * LANES, (g + 1) * LANES)
        rhs = jnp.concatenate([vn[c * CHUNK:(c + 1) * CHUNK, cols] for c in range(nch)], axis=1)
        sv = _dot(ws_ref[g], rhs) + bs_ref[:, g:g + 1]
        for c in range(nch):
            zs[c * CHUNK:(c + 1) * CHUNK, cols] = (
                u[c * CHUNK:(c + 1) * CHUNK, cols] * sv[:, c * LANES:(c + 1) * LANES]).astype(BF16)
    o_ref[...] = x + _dot(zs[...], wout_ref[...])


def _mix_c(x, nrm, win, vn, ws, bs, wout, *, tm):
    n = x.shape[0]
    row = lambda i: (i, 0)
    return pl.pallas_call(
        functools.partial(_mix_c_kernel, tm=tm),
        grid=(n // tm,),
        in_specs=[
            pl.BlockSpec((tm, D_MODEL), row),
            _const_spec(nrm.shape), _const_spec(win.shape), _const_spec(vn.shape), _const_spec(ws.shape),
            _const_spec(bs.shape), _const_spec(wout.shape),
        ],
        out_specs=pl.BlockSpec((tm, D_MODEL), row),
        out_shape=jax.ShapeDtypeStruct((n, D_MODEL), F32),
        scratch_shapes=[pltpu.VMEM((tm, D_MODEL), BF16)],
        compiler_params=_params("parallel"),
        name="mix_c",
    )(x, nrm, win, vn, ws, bs, wout)


def _rope_tables(seq):
    pos = jnp.arange(seq, dtype=F32)
    inv = 1.0 / (ROPE_THETA ** (jnp.arange(0, QK_ROPE, 2, dtype=F32) / QK_ROPE))
    ang = pos[:, None] * inv[None, :]
    cos, sin = jnp.cos(ang), jnp.sin(ang)
    one = jnp.ones((seq, QK_NOPE), F32)
    zero_n = jnp.zeros((seq, QK_NOPE), F32)
    pad = jnp.zeros((seq, LANES - QK_NOPE - QK_ROPE), F32)
    return (jnp.concatenate([one, cos, cos, pad], axis=1), jnp.concatenate([zero_n, sin, sin, pad], axis=1))


def _head_block(nope, x1, x2):
    pad = jnp.zeros((nope.shape[0], LANES - QK_NOPE - QK_ROPE), nope.dtype)
    return jnp.concatenate([nope, x1, x2, pad], axis=1)


def _prep_even(i, ab_norm, ab_w_in, mla_q_norm, mla_w_uq, mla_kv_norm, mla_w_ukv, mlstm_conv_w, mlstm_conv_b,
               mlstm_gate_bias, mlstm_out_norm, ab_w_out):
    w_in = ab_w_in[i]
    half = QK_ROPE // 2
    zq = jnp.zeros((D_MODEL, QK_NOPE), F32)
    kr1 = w_in[:, 384:384 + half]
    kr2 = w_in[:, 384 + half:416]
    gate_order = jnp.array([0, 1, 2, 3, 8, 9, 10, 11, 4, 5, 6, 7, 12, 13, 14, 15])
    gates = jnp.pad(w_in[:, 2464:2480][:, gate_order], ((0, 0), (0, LANES - 2 * N_STREAMS)))
    w_big = jnp.concatenate([
        w_in[:, 0:384],
        _head_block(zq, kr1, kr2),
        _head_block(zq, -kr2, kr1),
        gates,
        w_in[:, 416:2464],
    ], axis=1).astype(BF16)
    w_uq = mla_w_uq[i].reshape(Q_LORA, MLA_HEADS, QK_NOPE + QK_ROPE)
    zn = jnp.zeros((Q_LORA, QK_NOPE), F32)
    def placed(h, blk):
        return blk if h % 2 == 0 else jnp.roll(blk, LANES // 2, axis=1)

    wqa = jnp.concatenate([placed(h, _head_block(w_uq[:, h, :QK_NOPE], w_uq[:, h, QK_NOPE:QK_NOPE + half],
                                                 w_uq[:, h, QK_NOPE + half:])) for h in range(MLA_HEADS)], axis=1)
    wqb = jnp.concatenate([placed(h, _head_block(zn, -w_uq[:, h, QK_NOPE + half:],
                                                 w_uq[:, h, QK_NOPE:QK_NOPE + half]))
                           for h in range(MLA_HEADS)], axis=1)
    w_ukv = mla_w_ukv[i].reshape(KV_LORA, MLA_HEADS, QK_NOPE + V_HEAD)
    wka = w_ukv[:, :, :QK_NOPE].reshape(KV_LORA, MLA_HEADS * QK_NOPE)
    wv = w_ukv[:, :, QK_NOPE:].reshape(KV_LORA, MLA_HEADS * V_HEAD)
    w_out = ab_w_out[i].astype(BF16)
    return dict(
        nrm=ab_norm[i][None, :], w_big=w_big, qn=mla_q_norm[i][None, :], wqa=wqa.astype(BF16),
        wqb=wqb.astype(BF16), kvn=mla_kv_norm[i][None, :], wka=wka.astype(BF16), wv=wv.astype(BF16),
        gb=jnp.broadcast_to(mlstm_gate_bias[i][gate_order][:, None], (2 * N_STREAMS, LANES)),
        conv_w=mlstm_conv_w[i], conv_b=mlstm_conv_b[i][None, :], on=mlstm_out_norm[i][None, :],
        wa=w_out[:MLA_HEADS * V_HEAD], wm=w_out[MLA_HEADS * V_HEAD:])


def _pick(pref, seq):
    return min(pref, seq)


def _forward(x3, rope, even, odd, ffn, final_norm):
    batch, seq, _ = x3.shape
    x = x3.reshape(batch * seq, D_MODEL)
    tm = _pick(512, seq)
    cos_t, sin_t = rope
    e = even
    q, k, v, mq, mk, mv, mo, g = _ab_in(x, cos_t, sin_t, e, seq=seq, tm=_pick(1024, seq))
    tq = _pick(ATTN_TQ, seq)
    score_bytes = 2 * tq * seq * 4
    pair_bytes = 2 * 2 * seq * 2 * LANES * 2
    pairs = min(MLA_HEADS // 2, max(1, (ATTN_VMEM - score_bytes) // pair_bytes))
    while (MLA_HEADS // 2) % pairs:
        pairs -= 1
    ya = _attn(q, k, v, batch=batch, seq=seq, tq=tq, pairs=pairs)
    c_rows, dec, cols = _gates(g, batch=batch, seq=seq)
    hf, hb = _mlstm(mq, mk, mv, c_rows, dec, cols, batch=batch, seq=seq)
    f = ffn[0]
    x = _ab_out_ffn(x, ya, hf, hb, mo, e["on"], e["wa"], e["wm"], f["nrm"], f["wup"], f["cw"], f["cb"], f["wdn"],
                    final_norm, seq=seq, tm=tm)
    o = odd
    x = _mix_c(x, o["nrm"], o["win"], o["vn"], o["ws"], o["bs"], o["wout"], tm=_pick(1024, seq))
    f = ffn[1]
    x = _ffn(x, f["nrm"], f["wup"], f["cw"], f["cb"], f["wdn"], final_norm, seq=seq, tm=_pick(1024, seq), final=True)
    return x.reshape(batch, seq, D_MODEL)


def _prep_rest(c_norm, c_w_in, c_v_norm, c_w_spatial, c_b_spatial, c_w_out, ffn_norm, ffn_w_up, ffn_conv_w,
               ffn_conv_b, ffn_w_down):
    odd = dict(nrm=c_norm[0][None, :], win=c_w_in[0].astype(BF16), vn=c_v_norm[0][None, :],
               ws=c_w_spatial[0].astype(BF16), bs=c_b_spatial[0].T, wout=c_w_out[0].astype(BF16))
    ffn = [dict(nrm=ffn_norm[l][None, :], wup=ffn_w_up[l].astype(BF16), cw=ffn_conv_w[l], cb=ffn_conv_b[l][None, :],
                wdn=ffn_w_down[l].astype(BF16)) for l in range(2)]
    return odd, ffn


def kernel(x_prompt, x_sample, ab_norm, ab_w_in, mla_q_norm, mla_w_uq, mla_kv_norm, mla_w_ukv, mlstm_conv_w,
           mlstm_conv_b, mlstm_gate_bias, mlstm_out_norm, ab_w_out, c_norm, c_w_in, c_v_norm, c_w_spatial,
           c_b_spatial, c_w_out, ffn_norm, ffn_w_up, ffn_conv_w, ffn_conv_b, ffn_w_down, final_norm):
    even = _prep_even(0, ab_norm, ab_w_in, mla_q_norm, mla_w_uq, mla_kv_norm, mla_w_ukv, mlstm_conv_w, mlstm_conv_b,
                      mlstm_gate_bias, mlstm_out_norm, ab_w_out)
    odd, ffn = _prep_rest(c_norm, c_w_in, c_v_norm, c_w_spatial, c_b_spatial, c_w_out, ffn_norm, ffn_w_up,
                          ffn_conv_w, ffn_conv_b, ffn_w_down)
    fin = final_norm[None, :]
    rope = _rope_tables(max(x_prompt.shape[1], x_sample.shape[1]))
    return (_forward(x_prompt, rope, even, odd, ffn, fin), _forward(x_sample, rope, even, odd, ffn, fin))
```

```python
import functools

import jax
import jax.numpy as jnp
from jax import lax
from jax.experimental import pallas as pl
from jax.experimental.pallas import tpu as pltpu

F32 = jnp.float32
BF16 = jnp.bfloat16

EPS = 1e-6
D_MODEL = 1024
MLA_HEADS = 8
Q_LORA = 256
KV_LORA = 128
QK_NOPE = 64
QK_ROPE = 32
V_HEAD = 64
ROPE_THETA = 10000.0
M_HEADS = 4
M_HEAD_DIM = 128
M_WIDTH = M_HEADS * M_HEAD_DIM
N_STREAMS = 2 * M_HEADS
CHUNK = 128
G_GROUPS = 8
D_FF = 2816
LANES = 128
SUBLANES = 8
HALO = SUBLANES
BLOCK_CHUNKS = 8
PAIR = 2 * SUBLANES
MXU_COLS = 256
ROW_PITCH = 2
FF_COLS = MXU_COLS
KEY_TILE = MXU_COLS
ATTN_TQ = 512
ATTN_SCORE_BYTES = 32 * 1024 * 1024
ATTN_VMEM = 48 * 1024 * 1024
LOG2_E = 1.4426950408889634
VMEM_LIMIT = 56 * 1024 * 1024

_C_Q, _C_KV, _C_KRA, _C_KRB, _C_G, _C_MQK, _C_MV, _C_MO, _C_END = (
    0, 256, 384, 512, 640, 768, 1792, 2304, 2816)


def _rms(x, g):
    return x * lax.rsqrt(jnp.mean(x * x, axis=-1, keepdims=True) + EPS) * g


def _const_spec(shape):
    nd = len(shape)
    return pl.BlockSpec(shape, lambda *_: (0,) * nd, pipeline_mode=pl.Buffered(1))


def _params(*sem):
    return pltpu.CompilerParams(dimension_semantics=sem, vmem_limit_bytes=VMEM_LIMIT)


def _dot(a, b):
    return jnp.dot(a, b, preferred_element_type=F32)


def _dot_nt(a, b):
    return lax.dot_general(a, b, (((1,), (1,)), ((), ())), preferred_element_type=F32)


def _halo_specs(tile_rows, n_rows, width, halo=HALO):
    tb = tile_rows // halo
    last = n_rows // halo - 1
    return (pl.BlockSpec((halo, width), lambda i: (jnp.maximum(i * tb - 1, 0), 0)),
            pl.BlockSpec((halo, width), lambda i: (jnp.minimum((i + 1) * tb, last), 0)))


def _conv3(y, u, rows, w_ref, b_ref, c0):
    tall = rows + 2 * HALO
    outs = []
    for s in range(y.shape[1] // LANES):
        lanes = slice(s * LANES, (s + 1) * LANES)
        cols = slice(c0 + s * LANES, c0 + (s + 1) * LANES)
        u[s, pl.ds(0, tall, stride=ROW_PITCH), :] = y[:, lanes]
        prv = u[s, pl.ds(ROW_PITCH * (HALO - 1), rows, stride=ROW_PITCH), :]
        nxt = u[s, pl.ds(ROW_PITCH * (HALO + 1), rows, stride=ROW_PITCH), :]
        outs.append(prv * w_ref[0:1, cols] + y[HALO:HALO + rows, lanes] * w_ref[1:2, cols]
                    + nxt * w_ref[2:3, cols] + b_ref[:, cols])
    return jnp.concatenate(outs, axis=1)


def _ab_in_kernel(x_ref, xp_ref, xn_ref, cos_ref, sin_ref, nrm_ref, w_ref, qn_ref, wqa_ref, wqb_ref, kvn_ref,
                  wka_ref, wv_ref, gb_ref, cw_ref, cb_ref,
                  q_ref, k_ref, v_ref, mq_ref, mk_ref, mv_ref, mo_ref, g_ref, us, *, tm, tps):
    jj = pl.program_id(0) % tps
    xa = jnp.concatenate([
        jnp.where(jj == 0, 0.0, xp_ref[...]),
        x_ref[...],
        jnp.where(jj == tps - 1, 0.0, xn_ref[...]),
    ], axis=0)
    xe = _rms(xa, nrm_ref[...]).astype(BF16)
    xn = xe[HALO:HALO + tm]

    def proj(a, b):
        return _dot(xn, w_ref[:, a:b])

    cq = proj(_C_Q, _C_KV)
    ckv_kra = proj(_C_KV, _C_KRB)
    krb_g = proj(_C_KRB, _C_MQK)

    nq_blocks = M_WIDTH // MXU_COLS
    for cb in range(2 * nq_blocks):
        cols = slice(cb * MXU_COLS, (cb + 1) * MXU_COLS)
        y = _conv3(_dot(xe, w_ref[:, _C_MQK + cb * MXU_COLS:_C_MQK + (cb + 1) * MXU_COLS]), us.at[cb % 2], tm,
                   cw_ref, cb_ref, cb * MXU_COLS)
        y = y * jax.nn.sigmoid(y)
        if cb < nq_blocks:
            mq_ref[:, cols] = y.astype(BF16)
        else:
            mk_ref[:, (cb - nq_blocks) * MXU_COLS:(cb - nq_blocks + 1) * MXU_COLS] = (
                y * (M_HEAD_DIM ** -0.5)).astype(BF16)
    mv_ref[...] = proj(_C_MV, _C_MO).astype(BF16)
    mo_ref[...] = proj(_C_MO, _C_END)
    gt = krb_g[:, LANES:2 * LANES].T
    for c in range(tm // CHUNK):
        g_ref[c] = gt[0:2 * N_STREAMS, c * CHUNK:(c + 1) * CHUNK] + gb_ref[:, 0:1]

    cos = cos_ref[...]
    sin = sin_ref[...]
    scale = (QK_NOPE + QK_ROPE) ** -0.5 * LOG2_E
    cqn = _rms(cq, qn_ref[...]).astype(BF16)
    qa = _dot(cqn, wqa_ref[...])
    qb = _dot(cqn, wqb_ref[...])
    cos_q = cos * scale
    sin_q = sin * scale
    for h in range(MLA_HEADS):
        blk = slice(h * LANES, (h + 1) * LANES)
        q_ref[:, blk] = (qa[:, blk] * cos_q + qb[:, blk] * sin_q).astype(BF16)

    ckvn = _rms(ckv_kra[:, 0:LANES], kvn_ref[...]).astype(BF16)
    ka = _dot(ckvn, wka_ref[...])
    kr = ckv_kra[:, LANES:2 * LANES] * cos + krb_g[:, 0:LANES] * sin
    for h in range(MLA_HEADS):
        blk = slice(h * LANES, (h + 1) * LANES)
        k_ref[:, blk] = (ka[:, blk] + kr).astype(BF16)
    vd = _dot(ckvn, wv_ref[...])
    lane = lax.broadcasted_iota(jnp.int32, (tm, LANES), 1)
    for p in range(MLA_HEADS // 2):
        pair = vd[:, p * LANES:(p + 1) * LANES]
        v_ref[:, 2 * p * LANES:(2 * p + 1) * LANES] = jnp.where(
            lane < V_HEAD, pair, (lane == V_HEAD).astype(F32)).astype(BF16)
        v_ref[:, (2 * p + 1) * LANES:(2 * p + 2) * LANES] = jnp.where(
            lane >= V_HEAD, pair, (lane == 0).astype(F32)).astype(BF16)


def _ab_in(x, cos_t, sin_t, e, *, seq, tm):
    n = x.shape[0]
    nt = n // tm
    tps = seq // tm
    row = lambda i: (i, 0)
    tab = lambda i: (i % tps, 0)
    consts = [e[k] for k in ("nrm", "w_big", "qn", "wqa", "wqb", "kvn", "wka", "wv", "gb", "conv_w",
                             "conv_b")]
    out_shape = (
        jax.ShapeDtypeStruct((n, MLA_HEADS * LANES), BF16),
        jax.ShapeDtypeStruct((n, MLA_HEADS * LANES), BF16),
        jax.ShapeDtypeStruct((n, MLA_HEADS * LANES), BF16),
        jax.ShapeDtypeStruct((n, M_WIDTH), BF16),
        jax.ShapeDtypeStruct((n, M_WIDTH), BF16),
        jax.ShapeDtypeStruct((n, M_WIDTH), BF16),
        jax.ShapeDtypeStruct((n, M_WIDTH), F32),
        jax.ShapeDtypeStruct((n // CHUNK, 2 * N_STREAMS, CHUNK), F32),
    )
    wide = pl.BlockSpec((tm, MLA_HEADS * LANES), row)
    narrow = pl.BlockSpec((tm, M_WIDTH), row)
    return pl.pallas_call(
        functools.partial(_ab_in_kernel, tm=tm, tps=tps),
        grid=(nt,),
        in_specs=[pl.BlockSpec((tm, D_MODEL), row), *_halo_specs(tm, n, D_MODEL),
                  pl.BlockSpec((tm, LANES), tab), pl.BlockSpec((tm, LANES), tab)]
                 + [_const_spec(c.shape) for c in consts],
        out_specs=(wide, wide, wide, narrow, narrow, narrow, narrow,
                   pl.BlockSpec((tm // CHUNK, 2 * N_STREAMS, CHUNK), lambda i: (i, 0, 0))),
        out_shape=out_shape,
        scratch_shapes=[pltpu.VMEM((2, MXU_COLS // LANES, ROW_PITCH * (tm + 2 * HALO), LANES), F32)],
        compiler_params=_params("parallel"),
        name="ab_in",
    )(x, x, x, cos_t, sin_t, *consts)


def _attn_kernel(q_ref, k_ref, v_ref, o_ref, s_scr, *, tq, pairs):
    nt = k_ref.shape[0] // KEY_TILE
    lane = lax.broadcasted_iota(jnp.int32, (tq, LANES), 1)
    for pr in range(pairs):
        mbs = []
        for h in range(2):
            hc = slice((2 * pr + h) * LANES, (2 * pr + h + 1) * LANES)
            q = q_ref[:, hc]
            mx = jnp.full((tq, LANES), -jnp.inf, F32)
            for t in range(nt):
                keys = slice(t * KEY_TILE, (t + 1) * KEY_TILE)
                s = _dot_nt(q, k_ref[keys, hc])
                s_scr[h, :, keys] = s
                for j in range(KEY_TILE // LANES):
                    mx = jnp.maximum(mx, s[:, j * LANES:(j + 1) * LANES])
            mbs.append(jnp.broadcast_to(jnp.max(mx, axis=-1, keepdims=True), (tq, LANES)))
        outs = []
        for h in range(2):
            hc = slice((2 * pr + h) * LANES, (2 * pr + h + 1) * LANES)
            mb = jnp.concatenate([mbs[h]] * (KEY_TILE // LANES), axis=1)
            acc = jnp.zeros((tq, LANES), F32)
            for t in range(nt):
                keys = slice(t * KEY_TILE, (t + 1) * KEY_TILE)
                p = jnp.exp2(s_scr[h, :, keys] - mb).astype(BF16)
                acc = acc + _dot(p, v_ref[keys, hc])
            ones_lane = V_HEAD if h == 0 else 0
            outs.append(acc / acc[:, ones_lane:ones_lane + 1])
        o_ref[:, pr * LANES:(pr + 1) * LANES] = jnp.where(lane < V_HEAD, outs[0], outs[1]).astype(BF16)


def _attn(q, k, v, *, batch, seq, tq, pairs):
    n = q.shape[0]
    nq = seq // tq
    steps = MLA_HEADS // 2 // pairs
    return pl.pallas_call(
        functools.partial(_attn_kernel, tq=tq, pairs=pairs),
        grid=(batch, steps, nq),
        in_specs=[
            pl.BlockSpec((tq, 2 * pairs * LANES), lambda b, p, i: (b * nq + i, p)),
            pl.BlockSpec((seq, 2 * pairs * LANES), lambda b, p, i: (b, p), pipeline_mode=pl.Buffered(1)),
            pl.BlockSpec((seq, 2 * pairs * LANES), lambda b, p, i: (b, p), pipeline_mode=pl.Buffered(1)),
        ],
        out_specs=pl.BlockSpec((tq, pairs * LANES), lambda b, p, i: (b * nq + i, p)),
        out_shape=jax.ShapeDtypeStruct((n, MLA_HEADS * V_HEAD), BF16),
        scratch_shapes=[pltpu.VMEM((2, tq, seq), F32)],
        compiler_params=_params("parallel", "parallel", "arbitrary"),
        name="attn",
    )(q, k, v)


def _log_sigmoid(x):
    return jnp.minimum(x, 0.0) - jnp.log1p(jnp.exp(-jnp.abs(x)))


def _gates_kernel(g_ref, c_ref, dec_ref, cols_ref, tot_s, mloc_s, mpf_s, mpb_s, *, nc):
    rows = nc * N_STREAMS
    shape = (rows, CHUNK)
    three = (nc, N_STREAMS, CHUNK)
    li = g_ref[:, 0:N_STREAMS, :].reshape(shape)
    lf = _log_sigmoid(g_ref[:, N_STREAMS:2 * N_STREAMS, :].reshape(shape))
    fwd = (lax.broadcasted_iota(jnp.int32, shape, 0) & M_HEADS) == 0
    lane = lax.broadcasted_iota(jnp.int32, shape, 1)
    shifts = (1, 2, 4, 8, 16, 32, 64)

    pre = lf
    for sh in shifts:
        pre = pre + jnp.where(lane >= sh, pltpu.roll(pre, sh, axis=1), 0.0)
    tot = jnp.broadcast_to(pre[:, CHUNK - 1:CHUNK], shape)
    b = jnp.where(fwd, pre, tot - pre + lf)
    c = li - b
    run_f = c
    run_b = c
    for sh in shifts:
        run_f = jnp.maximum(run_f, jnp.where(lane >= sh, pltpu.roll(run_f, sh, axis=1), -jnp.inf))
        run_b = jnp.maximum(run_b, jnp.where(lane < CHUNK - sh, pltpu.roll(run_b, CHUNK - sh, axis=1), -jnp.inf))
    run = jnp.where(fwd, run_f, run_b)
    m_loc = tot + jnp.broadcast_to(jnp.max(c, axis=1, keepdims=True), shape)

    tot_s[...] = tot.reshape(three)
    mloc_s[...] = m_loc.reshape(three)
    fwd8 = lax.broadcasted_iota(jnp.int32, (N_STREAMS, CHUNK), 0) < M_HEADS

    def step(i, m):
        jb = nc - 1 - i
        mpf_s[i] = m
        mpb_s[jb] = m
        return jnp.maximum(jnp.where(fwd8, tot_s[i], tot_s[jb]) + m, jnp.where(fwd8, mloc_s[i], mloc_s[jb]))

    lax.fori_loop(0, nc, step, jnp.zeros((N_STREAMS, CHUNK), F32))
    m_prev = jnp.where(fwd, mpf_s[...].reshape(shape), mpb_s[...].reshape(shape))

    m_run = jnp.maximum(m_prev, run)
    floor = jnp.exp(-(b + m_run))
    w = jnp.exp(tot + c - m_loc)
    m_new = jnp.maximum(tot + m_prev, m_loc)
    c_ref[:, 0:N_STREAMS, :] = c.reshape(three)
    c_ref[:, N_STREAMS:2 * N_STREAMS, :] = w.reshape(three)
    c_ref[:, 2 * N_STREAMS:3 * N_STREAMS, :] = m_prev.reshape(three)
    dec_ref[:, 0:N_STREAMS, :] = jnp.exp(tot + m_prev - m_new).reshape(three)
    dec_ref[:, N_STREAMS:2 * N_STREAMS, :] = jnp.exp(m_loc - m_new).reshape(three)
    quantities = [a.reshape(three) for a in (m_run, floor)]
    pad_rows = CHUNK - BLOCK_CHUNKS * 2 * N_STREAMS
    pad = [jnp.zeros((pad_rows, CHUNK), F32)] if pad_rows else []
    for blk in range(nc // BLOCK_CHUNKS):
        tile = jnp.concatenate(
            [a[blk * BLOCK_CHUNKS + cq] for cq in range(BLOCK_CHUNKS) for a in quantities] + pad, axis=0)
        cols_ref[blk] = tile.T


def _gates(g, *, batch, seq):
    nc = seq // CHUNK
    nchunks = g.shape[0]
    three = (nc, N_STREAMS, CHUNK)
    return pl.pallas_call(
        functools.partial(_gates_kernel, nc=nc),
        grid=(batch,),
        in_specs=[pl.BlockSpec((nc, 2 * N_STREAMS, CHUNK), lambda b: (b, 0, 0))],
        out_specs=(
            pl.BlockSpec((nc, 3 * N_STREAMS, CHUNK), lambda b: (b, 0, 0)),
            pl.BlockSpec((nc, 2 * N_STREAMS, CHUNK), lambda b: (b, 0, 0)),
            pl.BlockSpec((nc // BLOCK_CHUNKS, CHUNK, LANES), lambda b: (b, 0, 0)),
        ),
        out_shape=(
            jax.ShapeDtypeStruct((nchunks, 3 * N_STREAMS, CHUNK), F32),
            jax.ShapeDtypeStruct((nchunks, 2 * N_STREAMS, CHUNK), F32),
            jax.ShapeDtypeStruct((nchunks // BLOCK_CHUNKS, CHUNK, LANES), F32),
        ),
        scratch_shapes=[pltpu.VMEM(three, F32)] * 4,
        compiler_params=_params("parallel"),
        name="gates",
    )(g)


def _mlstm_kernel(*refs):
    ins, (hf_ref, hb_ref, st, bc) = (refs[0:6], refs[6:12]), refs[12:]

    @pl.when(pl.program_id(1) == 0)
    def _():
        st[...] = jnp.zeros_like(st)

    row = lax.broadcasted_iota(jnp.int32, (CHUNK, CHUNK), 0)
    col = lax.broadcasted_iota(jnp.int32, (CHUNK, CHUNK), 1)
    masks = (col <= row, col >= row)
    ones = (col == 0).astype(BF16)
    orders = (range(BLOCK_CHUNKS), range(BLOCK_CHUNKS - 1, -1, -1))
    heads = range(M_HEADS)
    hcs = [slice(hh * LANES, (hh + 1) * LANES) for hh in heads]
    for d in range(2):
        cols_t = ins[d][5][0]
        for cq in range(BLOCK_CHUNKS):
            for hh in heads:
                base = cq * 2 * N_STREAMS + d * M_HEADS + hh
                for qi in range(2):
                    lane_i = base + qi * N_STREAMS
                    bc[d, (qi * BLOCK_CHUNKS + cq) * M_HEADS + hh] = jnp.broadcast_to(
                        cols_t[:, lane_i:lane_i + 1], (CHUNK, LANES))
    qs, v1s, qk, c_loc = {}, {}, {}, {}
    for d in range(2):
        q_ref, k_ref = ins[d][0], ins[d][1]
        for cq in orders[d]:
            rows = slice(cq * CHUNK, (cq + 1) * CHUNK)
            for hh in heads:
                qs[d, cq, hh] = q_ref[rows, hcs[hh]]
                qk[d, cq, hh] = _dot_nt(qs[d, cq, hh], k_ref[rows, hcs[hh]])
    for d in range(2):
        k_ref, v_ref, c_ref = ins[d][1], ins[d][2], ins[d][3]
        for cq in orders[d]:
            rows = slice(cq * CHUNK, (cq + 1) * CHUNK)
            for hh in heads:
                s = d * M_HEADS + hh
                v1s[d, cq, hh] = jnp.concatenate([v_ref[rows, hcs[hh]], ones], axis=1)
                kw_t = (k_ref[rows, hcs[hh]].astype(F32).T
                        * c_ref[cq, N_STREAMS + s:N_STREAMS + s + 1, :]).astype(BF16)
                c_loc[d, cq, hh] = _dot(kw_t, v1s[d, cq, hh])
    for step in range(BLOCK_CHUNKS):
        for d in range(2):
            _mlstm_chunk(d, orders[d][step], ins[d][3], ins[d][4], (hf_ref, hb_ref)[d], st, bc, masks[d], hcs,
                         qs, v1s, qk, c_loc)


def _mlstm_chunk(d, cq, c_ref, dec_ref, h_ref, st, bc, mask, hcs, qs, v1s, qk, c_loc):
    rows = slice(cq * CHUNK, (cq + 1) * CHUNK)
    for hh in range(M_HEADS):
        s = d * M_HEADS + hh
        key = (d, cq, hh)
        state = st[d, hh]
        m_run = bc[d, cq * M_HEADS + hh]
        floor = bc[d, (BLOCK_CHUNKS + cq) * M_HEADS + hh]
        c_ext = jnp.concatenate([c_ref[cq, s:s + 1, :], c_ref[cq, 2 * N_STREAMS + s:2 * N_STREAMS + s + 1, :]],
                                axis=1)
        d_ext = jnp.exp(c_ext - jnp.concatenate([m_run, m_run], axis=1))
        p_ext = jnp.concatenate([qk[key] * jnp.where(mask, d_ext[:, 0:LANES], 0.0),
                                 qs[key].astype(F32) * d_ext[:, LANES:2 * LANES]], axis=1).astype(BF16)
        nd = _dot(p_ext, jnp.concatenate([v1s[key], state.astype(BF16)], axis=0))
        den = nd[:, LANES:LANES + 1]
        h_ref[rows, hcs[hh]] = nd[:, 0:LANES] / jnp.maximum(jnp.abs(den), floor)
        st[d, hh] = (dec_ref[cq, s:s + 1, 0:1] * state
                     + dec_ref[cq, N_STREAMS + s:N_STREAMS + s + 1, 0:1] * c_loc[key])


def _mlstm(mq, mk, mv, c_rows, dec, cols, *, batch, seq):
    n = mq.shape[0]
    ts = BLOCK_CHUNKS * CHUNK
    nblk = seq // ts

    def specs(direction):
        def blk(b, i):
            return b * nblk + (i if direction == 0 else nblk - 1 - i)

        tok = pl.BlockSpec((ts, M_WIDTH), lambda b, i: (blk(b, i), 0))
        return tok, [tok, tok, tok,
                     pl.BlockSpec((BLOCK_CHUNKS, 3 * N_STREAMS, CHUNK), lambda b, i: (blk(b, i), 0, 0)),
                     pl.BlockSpec((BLOCK_CHUNKS, 2 * N_STREAMS, CHUNK), lambda b, i: (blk(b, i), 0, 0)),
                     pl.BlockSpec((1, CHUNK, LANES), lambda b, i: (blk(b, i), 0, 0))]

    (tok_f, in_f), (tok_b, in_b) = specs(0), specs(1)
    h_shape = jax.ShapeDtypeStruct((n, M_WIDTH), F32)
    args = (mq, mk, mv, c_rows, dec, cols)
    return pl.pallas_call(
        _mlstm_kernel,
        grid=(batch, nblk),
        in_specs=in_f + in_b,
        out_specs=(tok_f, tok_b),
        out_shape=(h_shape, h_shape),
        scratch_shapes=[pltpu.VMEM((2, M_HEADS, M_HEAD_DIM, 2 * LANES), F32),
                        pltpu.VMEM((2, 2 * BLOCK_CHUNKS * M_HEADS, CHUNK, LANES), F32)],
        compiler_params=_params("parallel", "arbitrary"),
        name="mlstm",
    )(*args, *args)


def _ffn_body(xa, nrm_ref, wup_ref, cw_ref, cb_ref, wdn_ref, fin_ref, us, act, *, tm, final):
    xn = _rms(xa, nrm_ref[...]).astype(BF16)
    for c in range(D_FF // FF_COLS):
        g0 = c * FF_COLS
        v0 = D_FF + c * FF_COLS
        gate = _conv3(_dot(xn, wup_ref[:, g0:g0 + FF_COLS]), us.at[0], tm, cw_ref, cb_ref, g0)
        val = _conv3(_dot(xn, wup_ref[:, v0:v0 + FF_COLS]), us.at[1], tm, cw_ref, cb_ref, v0)
        act[:, g0:g0 + FF_COLS] = (gate * jax.nn.sigmoid(gate) * val).astype(BF16)
    y = xa[HALO:HALO + tm] + _dot(act[...], wdn_ref[...])
    if final:
        y = _rms(y, fin_ref[...])
    return y


def _ffn_kernel(x_ref, xp_ref, xn_ref, nrm_ref, wup_ref, cw_ref, cb_ref, wdn_ref, fin_ref, o_ref, us, act,
                *, tm, tps, final):
    jj = pl.program_id(0) % tps
    xa = jnp.concatenate([
        jnp.where(jj == 0, 0.0, xp_ref[...]),
        x_ref[...],
        jnp.where(jj == tps - 1, 0.0, xn_ref[...]),
    ], axis=0)
    o_ref[...] = _ffn_body(xa, nrm_ref, wup_ref, cw_ref, cb_ref, wdn_ref, fin_ref, us, act, tm=tm, final=final)


def _ffn_scratch(tm):
    return [pltpu.VMEM((2, FF_COLS // LANES, ROW_PITCH * (tm + 2 * HALO), LANES), F32),
            pltpu.VMEM((tm, D_FF), BF16)]


def _ab_out_ffn_kernel(x_ref, xp_ref, xn_ref, ya_ref, yap_ref, yan_ref, hf_ref, hfp_ref, hfn_ref, hb_ref, hbp_ref,
                       hbn_ref, mo_ref, mop_ref, mon_ref, on_ref, wa_ref, wm_ref, nrm_ref, wup_ref, cw_ref, cb_ref,
                       wdn_ref, fin_ref, o_ref, us, act, *, tm, tps):
    jj = pl.program_id(0) % tps

    def rows(main, prev, nxt):
        return jnp.concatenate([main[...], prev[...], nxt[...]], axis=0)

    hf, hb, mo = rows(hf_ref, hfp_ref, hfn_ref), rows(hb_ref, hbp_ref, hbn_ref), rows(mo_ref, mop_ref, mon_ref)
    parts = []
    for hh in range(M_HEADS):
        hc = slice(hh * LANES, (hh + 1) * LANES)
        t = hf[:, hc] + hb[:, hc]
        t = t * lax.rsqrt(jnp.mean(t * t, axis=-1, keepdims=True) + EPS) * on_ref[:, hc]
        parts.append((jax.nn.sigmoid(mo[:, hc]) * t).astype(BF16))
    ym = jnp.concatenate(parts, axis=1)
    x1 = (rows(x_ref, xp_ref, xn_ref) + _dot(rows(ya_ref, yap_ref, yan_ref), wa_ref[...]) + _dot(ym, wm_ref[...]))
    xa = jnp.concatenate([
        jnp.where(jj == 0, 0.0, x1[tm + PAIR - HALO:tm + PAIR]),
        x1[0:tm],
        jnp.where(jj == tps - 1, 0.0, x1[tm + PAIR:tm + PAIR + HALO]),
    ], axis=0)
    o_ref[...] = _ffn_body(xa, nrm_ref, wup_ref, cw_ref, cb_ref, wdn_ref, fin_ref, us, act, tm=tm, final=False)


def _ab_out_ffn(x, ya, hf, hb, mo, on, wa, wm, nrm, wup, cw, cb, wdn, fin, *, seq, tm):
    n = x.shape[0]
    row = lambda i: (i, 0)

    def with_halo(width):
        return [pl.BlockSpec((tm, width), row), *_halo_specs(tm, n, width, PAIR)]

    consts = [on, wa, wm, nrm, wup, cw, cb, wdn, fin]
    return pl.pallas_call(
        functools.partial(_ab_out_ffn_kernel, tm=tm, tps=seq // tm),
        grid=(n // tm,),
        in_specs=with_halo(D_MODEL) + 4 * with_halo(M_WIDTH) + [_const_spec(c.shape) for c in consts],
        out_specs=pl.BlockSpec((tm, D_MODEL), row),
        out_shape=jax.ShapeDtypeStruct((n, D_MODEL), F32),
        scratch_shapes=_ffn_scratch(tm),
        compiler_params=_params("parallel"),
        name="ab_out_ffn",
    )(x, x, x, ya, ya, ya, hf, hf, hf, hb, hb, hb, mo, mo, mo, *consts)


def _ffn(x, nrm, wup, cw, cb, wdn, fin, *, seq, tm, final):
    n = x.shape[0]
    row = lambda i: (i, 0)
    return pl.pallas_call(
        functools.partial(_ffn_kernel, tm=tm, tps=seq // tm, final=final),
        grid=(n // tm,),
        in_specs=[
            pl.BlockSpec((tm, D_MODEL), row), *_halo_specs(tm, n, D_MODEL),
            _const_spec(nrm.shape), _const_spec(wup.shape), _const_spec(cw.shape), _const_spec(cb.shape),
            _const_spec(wdn.shape), _const_spec(fin.shape),
        ],
        out_specs=pl.BlockSpec((tm, D_MODEL), row),
        out_shape=jax.ShapeDtypeStruct((n, D_MODEL), F32),
        scratch_shapes=_ffn_scratch(tm),
        compiler_params=_params("parallel"),
        name="ffn_final" if final else "ffn",
    )(x, x, x, nrm, wup, cw, cb, wdn, fin)


def _gelu(x):
    return 0.5 * x * (1.0 + lax.erf(x * (2.0 ** -0.5)))


def _mix_c_kernel(x_ref, nrm_ref, win_ref, vn_ref, ws_ref, bs_ref, wout_ref, o_ref, zs, *, tm):
    x = x_ref[...]
    xn = _rms(x, nrm_ref[...]).astype(BF16)
    v = _gelu(_dot(xn, win_ref[:, D_MODEL:2 * D_MODEL]))
    u = _gelu(_dot(xn, win_ref[:, 0:D_MODEL]))
    vn = _rms(v, vn_ref[...]).astype(BF16)
    nch = tm // CHUNK
    for g in range(G_GROUPS):
        cols = slice(g * LANES, (g + 1) * LANES)
        rhs = jnp.concatenate([vn[c * CHUNK:(c + 1) * CHUNK, cols] for c in range(nch)], axis=1)
        sv = _dot(ws_ref[g], rhs) + bs_ref[:, g:g + 1]
        for c in range(nch):
            zs[c * CHUNK:(c + 1) * CHUNK, cols] = (
                u[c * CHUNK:(c + 1) * CHUNK, cols] * sv[:, c * LANES:(c + 1) * LANES]).astype(BF16)
    o_ref[...] = x + _dot(zs[...], wout_ref[...])


def _mix_c(x, nrm, win, vn, ws, bs, wout, *, tm):
    n = x.shape[0]
    row = lambda i: (i, 0)
    return pl.pallas_call(
        functools.partial(_mix_c_kernel, tm=tm),
        grid=(n // tm,),
        in_specs=[
            pl.BlockSpec((tm, D_MODEL), row),
            _const_spec(nrm.shape), _const_spec(win.shape), _const_spec(vn.shape), _const_spec(ws.shape),
            _const_spec(bs.shape), _const_spec(wout.shape),
        ],
        out_specs=pl.BlockSpec((tm, D_MODEL), row),
        out_shape=jax.ShapeDtypeStruct((n, D_MODEL), F32),
        scratch_shapes=[pltpu.VMEM((tm, D_MODEL), BF16)],
        compiler_params=_params("parallel"),
        name="mix_c",
    )(x, nrm, win, vn, ws, bs, wout)


def _rope_tables(seq):
    pos = jnp.arange(seq, dtype=F32)
    inv = 1.0 / (ROPE_THETA ** (jnp.arange(0, QK_ROPE, 2, dtype=F32) / QK_ROPE))
    ang = pos[:, None] * inv[None, :]
    cos, sin = jnp.cos(ang), jnp.sin(ang)
    one = jnp.ones((seq, QK_NOPE), F32)
    zero_n = jnp.zeros((seq, QK_NOPE), F32)
    pad = jnp.zeros((seq, LANES - QK_NOPE - QK_ROPE), F32)
    return (jnp.concatenate([one, cos, cos, pad], axis=1), jnp.concatenate([zero_n, sin, sin, pad], axis=1))


def _head_block(nope, x1, x2):
    pad = jnp.zeros((nope.shape[0], LANES - QK_NOPE - QK_ROPE), nope.dtype)
    return jnp.concatenate([nope, x1, x2, pad], axis=1)


def _prep_even(i, ab_norm, ab_w_in, mla_q_norm, mla_w_uq, mla_kv_norm, mla_w_ukv, mlstm_conv_w, mlstm_conv_b,
               mlstm_gate_bias, mlstm_out_norm, ab_w_out):
    w_in = ab_w_in[i]
    half = QK_ROPE // 2
    zq = jnp.zeros((D_MODEL, QK_NOPE), F32)
    kr1 = w_in[:, 384:384 + half]
    kr2 = w_in[:, 384 + half:416]
    gate_order = jnp.array([0, 1, 2, 3, 8, 9, 10, 11, 4, 5, 6, 7, 12, 13, 14, 15])
    gates = jnp.pad(w_in[:, 2464:2480][:, gate_order], ((0, 0), (0, LANES - 2 * N_STREAMS)))
    w_big = jnp.concatenate([
        w_in[:, 0:384],
        _head_block(zq, kr1, kr2),
        _head_block(zq, -kr2, kr1),
        gates,
        w_in[:, 416:2464],
    ], axis=1).astype(BF16)
    w_uq = mla_w_uq[i].reshape(Q_LORA, MLA_HEADS, QK_NOPE + QK_ROPE)
    zn = jnp.zeros((Q_LORA, QK_NOPE), F32)
    wqa = jnp.concatenate([_head_block(w_uq[:, h, :QK_NOPE], w_uq[:, h, QK_NOPE:QK_NOPE + half],
                                       w_uq[:, h, QK_NOPE + half:]) for h in range(MLA_HEADS)], axis=1)
    wqb = jnp.concatenate([_head_block(zn, -w_uq[:, h, QK_NOPE + half:], w_uq[:, h, QK_NOPE:QK_NOPE + half])
                           for h in range(MLA_HEADS)], axis=1)
    w_ukv = mla_w_ukv[i].reshape(KV_LORA, MLA_HEADS, QK_NOPE + V_HEAD)
    wka = jnp.pad(w_ukv[:, :, :QK_NOPE], ((0, 0), (0, 0), (0, LANES - QK_NOPE))).reshape(KV_LORA, MLA_HEADS * LANES)
    wv = w_ukv[:, :, QK_NOPE:].reshape(KV_LORA, MLA_HEADS * V_HEAD)
    w_out = ab_w_out[i].astype(BF16)
    return dict(
        nrm=ab_norm[i][None, :], w_big=w_big, qn=mla_q_norm[i][None, :], wqa=wqa.astype(BF16),
        wqb=wqb.astype(BF16), kvn=mla_kv_norm[i][None, :], wka=wka.astype(BF16), wv=wv.astype(BF16),
        gb=jnp.broadcast_to(mlstm_gate_bias[i][gate_order][:, None], (2 * N_STREAMS, LANES)),
        conv_w=mlstm_conv_w[i], conv_b=mlstm_conv_b[i][None, :], on=mlstm_out_norm[i][None, :],
        wa=w_out[:MLA_HEADS * V_HEAD], wm=w_out[MLA_HEADS * V_HEAD:])


def _pick(pref, seq):
    return min(pref, seq)


def _forward(x3, rope, even, odd, ffn, final_norm):
    batch, seq, _ = x3.shape
    x = x3.reshape(batch * seq, D_MODEL)
    tm = _pick(512, seq)
    cos_t, sin_t = rope
    e = even
    q, k, v, mq, mk, mv, mo, g = _ab_in(x, cos_t, sin_t, e, seq=seq, tm=_pick(1024, seq))
    tq = _pick(min(ATTN_TQ, ATTN_SCORE_BYTES // (2 * seq * 4)), seq)
    score_bytes = 2 * tq * seq * 4
    pair_bytes = 2 * seq * 2 * LANES * 2
    pairs = min(MLA_HEADS // 2, max(1, (ATTN_VMEM - score_bytes) // pair_bytes))
    while (MLA_HEADS // 2) % pairs:
        pairs -= 1
    ya = _attn(q, k, v, batch=batch, seq=seq, tq=tq, pairs=pairs)
    c_rows, dec, cols = _gates(g, batch=batch, seq=seq)
    hf, hb = _mlstm(mq, mk, mv, c_rows, dec, cols, batch=batch, seq=seq)
    f = ffn[0]
    x = _ab_out_ffn(x, ya, hf, hb, mo, e["on"], e["wa"], e["wm"], f["nrm"], f["wup"], f["cw"], f["cb"], f["wdn"],
                    final_norm, seq=seq, tm=tm)
    o = odd
    x = _mix_c(x, o["nrm"], o["win"], o["vn"], o["ws"], o["bs"], o["wout"], tm=_pick(1024, seq))
    f = ffn[1]
    x = _ffn(x, f["nrm"], f["wup"], f["cw"], f["cb"], f["wdn"], final_norm, seq=seq, tm=_pick(1024, seq), final=True)
    return x.reshape(batch, seq, D_MODEL)


def _prep_rest(c_norm, c_w_in, c_v_norm, c_w_spatial, c_b_spatial, c_w_out, ffn_norm, ffn_w_up, ffn_conv_w,
               ffn_conv_b, ffn_w_down):
    odd = dict(nrm=c_norm[0][None, :], win=c_w_in[0].astype(BF16), vn=c_v_norm[0][None, :],
               ws=c_w_spatial[0].astype(BF16), bs=c_b_spatial[0].T, wout=c_w_out[0].astype(BF16))
    ffn = [dict(nrm=ffn_norm[l][None, :], wup=ffn_w_up[l].astype(BF16), cw=ffn_conv_w[l], cb=ffn_conv_b[l][None, :],
                wdn=ffn_w_down[l].astype(BF16)) for l in range(2)]
    return odd, ffn


def kernel(x_prompt, x_sample, ab_norm, ab_w_in, mla_q_norm, mla_w_uq, mla_kv_norm, mla_w_ukv, mlstm_conv_w,
           mlstm_conv_b, mlstm_gate_bias, mlstm_out_norm, ab_w_out, c_norm, c_w_in, c_v_norm, c_w_spatial,
           c_b_spatial, c_w_out, ffn_norm, ffn_w_up, ffn_conv_w, ffn_conv_b, ffn_w_down, final_norm):
    even = _prep_even(0, ab_norm, ab_w_in, mla_q_norm, mla_w_uq, mla_kv_norm, mla_w_ukv, mlstm_conv_w, mlstm_conv_b,
                      mlstm_gate_bias, mlstm_out_norm, ab_w_out)
    odd, ffn = _prep_rest(c_norm, c_w_in, c_v_norm, c_w_spatial, c_b_spatial, c_w_out, ffn_norm, ffn_w_up,
                          ffn_conv_w, ffn_conv_b, ffn_w_down)
    fin = final_norm[None, :]
    rope = _rope_tables(max(x_prompt.shape[1], x_sample.shape[1]))
    return (_forward(x_prompt, rope, even, odd, ffn, fin), _forward(x_sample, rope, even, odd, ffn, fin))
```
